```python
import jax, jax.numpy as jnp
from jax import lax
import numpy as np

D_MODEL = 2048
BATCH = 2
SEQ = 4096
DEPTH = 1

A_HEADS = 8
A_KEY_DIM = 128
A_VAL_DIM = 128
A_KEY_WIDTH = A_HEADS * A_KEY_DIM
A_WIDTH = A_HEADS * A_VAL_DIM
A_CHUNK = 32
B_GROUPS = 8
B_GROUP_DIM = 128
B_WIDTH = B_GROUPS * B_GROUP_DIM
B_CHUNK = 128
D_FF = 4 * D_MODEL
N_MOD = 6
EPS = 1e-6

IN_SIZES = (A_KEY_WIDTH, A_KEY_WIDTH, A_KEY_WIDTH, A_WIDTH, A_WIDTH, 2 * B_WIDTH, D_MODEL, D_MODEL)
IN_WIDTH = sum(IN_SIZES)
IN_SPLIT_POINTS = tuple(int(s) for s in np.cumsum(IN_SIZES)[:-1])

kernel_name = 'hybrid_hgrn2_sgu_block'


def rms_norm(x, g):
    xf = x.astype(jnp.float32)
    y = xf * lax.rsqrt(jnp.mean(xf * xf, axis=-1, keepdims=True) + EPS)
    return (y * g.astype(jnp.float32)).astype(x.dtype)


def layer_norm(x, g):
    xf = x.astype(jnp.float32)
    mu = jnp.mean(xf, axis=-1, keepdims=True)
    d = xf - mu
    y = d * lax.rsqrt(jnp.mean(d * d, axis=-1, keepdims=True) + EPS)
    return (y * g.astype(jnp.float32)).astype(x.dtype)


def to_heads(t, d):
    b, l, _ = t.shape
    return t.reshape(b, l, A_HEADS, d).transpose(0, 2, 1, 3)


def gated_state_scan(q, k, v, log_f):
    bn, h, l, dk = q.shape
    dv = v.shape[-1]
    n = l // A_CHUNK
    rs = lambda t: t.reshape(bn, h, n, A_CHUNK, t.shape[-1])
    q, k, v, log_f = rs(q), rs(k), rs(v), rs(log_f)
    b = jnp.cumsum(log_f, axis=3)
    b_last = b[:, :, :, -1:, :]
    q_dec = q * jnp.exp(b)
    k_dec = k * jnp.exp(-b)
    k_end = k * jnp.exp(b_last - b)
    mask = jnp.tril(jnp.ones((A_CHUNK, A_CHUNK), dtype=bool))
    att = jnp.einsum('bhnck,bhnsk->bhncs', q_dec, k_dec)
    att = jnp.where(mask, att, 0.0)
    o_intra = jnp.einsum('bhncs,bhnsv->bhncv', att, v)
    decay = jnp.exp(b_last[:, :, :, 0, :])

    def step(state, xs):
        qd, ke, vc, dc = xs
        o = jnp.einsum('bhck,bhkv->bhcv', qd, state)
        state = dc[..., None] * state + jnp.einsum('bhck,bhcv->bhkv', ke, vc)
        return state, o

    xs = (jnp.moveaxis(q_dec, 2, 0), jnp.moveaxis(k_end, 2, 0),
          jnp.moveaxis(v, 2, 0), jnp.moveaxis(decay, 2, 0))
    s0 = jnp.zeros((bn, h, dk, dv), jnp.float32)
    _, o_inter = lax.scan(step, s0, xs)
    o = o_intra + jnp.moveaxis(o_inter, 0, 2)
    return o.reshape(bn, h, l, dv)


def hgrn2_forget(logit, lb):
    lf = logit.astype(jnp.float32)
    log_f = jnp.log(lb + (1.0 - lb) * jax.nn.sigmoid(lf))
    k = (1.0 - lb) * jax.nn.sigmoid(-lf)
    return k, log_f


def hgrn2_bidir(q, f_fw, f_bw, i_v, o_gate, lb_fw, lb_bw, g_norm):
    bn, l, _ = q.shape
    k_fw, lf_fw = hgrn2_forget(f_fw, lb_fw)
    k_bw, lf_bw = hgrn2_forget(f_bw, lb_bw)
    qh = to_heads(q.astype(jnp.float32) * (A_KEY_DIM ** -0.5), A_KEY_DIM)
    vh = to_heads(i_v.astype(jnp.float32), A_VAL_DIM)
    o_fw = gated_state_scan(qh, to_heads(k_fw, A_KEY_DIM), vh, to_heads(lf_fw, A_KEY_DIM))
    fl = lambda t: jnp.flip(t, axis=2)
    o_bw = fl(gated_state_scan(fl(qh), fl(to_heads(k_bw, A_KEY_DIM)), fl(vh),
                               fl(to_heads(lf_bw, A_KEY_DIM))))
    o = rms_norm(o_fw + o_bw, g_norm)
    o = o.transpose(0, 2, 1, 3).reshape(bn, l, A_WIDTH).astype(o_gate.dtype)
    return o * jax.nn.silu(o_gate)


def chunked_sgu(z, g_v, w_s, b_s):
    bn, l, _ = z.shape
    z = jax.nn.gelu(z, approximate=False)
    u, v = jnp.split(z, 2, axis=-1)
    v = layer_norm(v, g_v)
    v = v.reshape(bn, l // B_CHUNK, B_CHUNK, B_GROUPS, B_GROUP_DIM)
    vm = jnp.einsum('gts,bnsgc->bntgc', w_s, v) + b_s.T[None, None, :, :, None]
    return u * vm.reshape(bn, l, B_WIDTH)


def setup_inputs(seed: int = 0) -> dict:
    key = jax.random.key(seed)
    ks = jax.random.split(key, 20)
    nrm = lambda k, shape, s: jax.random.normal(k, shape, jnp.float32) * s
    gain = lambda k, shape: 1.0 + nrm(k, shape, 0.02)
    return {
        'x': nrm(ks[0], (BATCH, SEQ, D_MODEL), 1.0),
        'c': nrm(ks[1], (BATCH, D_MODEL), 1.0),
        'w_ada': nrm(ks[2], (DEPTH, D_MODEL, N_MOD * D_MODEL), D_MODEL ** -0.5),
        'b_ada': nrm(ks[3], (DEPTH, N_MOD * D_MODEL), 0.02),
        'g_pre_mix': gain(ks[4], (DEPTH, D_MODEL)),
        'g_post_mix': gain(ks[5], (DEPTH, D_MODEL)),
        'g_pre_ffn': gain(ks[6], (DEPTH, D_MODEL)),
        'g_post_ffn': gain(ks[7], (DEPTH, D_MODEL)),
        'w_in': nrm(ks[8], (DEPTH, D_MODEL, IN_WIDTH), D_MODEL ** -0.5),
        'lb_logits': nrm(ks[9], (2, DEPTH + 1, A_KEY_WIDTH), 0.1),
        'g_hgrn_norm': gain(ks[10], (DEPTH, A_VAL_DIM)),
        'w_a_out': nrm(ks[11], (DEPTH, A_WIDTH, D_MODEL), A_WIDTH ** -0.5),
        'g_sgu_norm': gain(ks[12], (DEPTH, B_WIDTH)),
        'w_spatial': nrm(ks[13], (DEPTH, B_GROUPS, B_CHUNK, B_CHUNK), B_CHUNK ** -0.5),
        'b_spatial': nrm(ks[14], (DEPTH, B_GROUPS, B_CHUNK), 0.02),
        'w_b_out': nrm(ks[15], (DEPTH, B_WIDTH, D_MODEL), B_WIDTH ** -0.5),
        'w_o': nrm(ks[16], (DEPTH, D_MODEL, D_MODEL), D_MODEL ** -0.5),
        'w_ff1': nrm(ks[17], (DEPTH, D_MODEL, D_FF), D_MODEL ** -0.5),
        'w_ff2': nrm(ks[18], (DEPTH, D_FF, D_MODEL), D_FF ** -0.5),
    }


def reference(x, c, w_ada, b_ada, g_pre_mix, g_post_mix, g_pre_ffn, g_post_ffn, w_in,
              lb_logits, g_hgrn_norm, w_a_out, g_sgu_norm, w_spatial, b_spatial,
              w_b_out, w_o, w_ff1, w_ff2):
    lb_all = jnp.cumsum(jax.nn.softmax(lb_logits.astype(jnp.float32), axis=1), axis=1)
    h = x
    for l in range(DEPTH):
        mod = jax.nn.silu(c) @ w_ada[l] + b_ada[l]
        sh1, sc1, gt1, sh2, sc2, gt2 = [m[:, None, :] for m in jnp.split(mod, N_MOD, axis=-1)]
        a = rms_norm(h, g_pre_mix[l]) * (1 + sc1) + sh1
        proj = a @ w_in[l]
        q, f_fw, f_bw, i_v, o_gate, z, gate_a, gate_b = jnp.split(proj, IN_SPLIT_POINTS, axis=-1)
        y_a = hgrn2_bidir(q, f_fw, f_bw, i_v, o_gate, lb_all[0, l], lb_all[1, l],
                          g_hgrn_norm[l]) @ w_a_out[l]
        y_b = chunked_sgu(z, g_sgu_norm[l], w_spatial[l], b_spatial[l]) @ w_b_out[l]
        merged = jax.nn.sigmoid(gate_a) * y_a + jax.nn.sigmoid(gate_b) * y_b
        h = h + gt1 * rms_norm(merged @ w_o[l], g_post_mix[l])
        a = rms_norm(h, g_pre_ffn[l]) * (1 + sc2) + sh2
        ff = jnp.square(jax.nn.relu(a @ w_ff1[l])) @ w_ff2[l]
        h = h + gt2 * rms_norm(ff, g_post_ffn[l])
    return h
```

```python
import functools

import jax
import jax.numpy as jnp
from jax import lax
from jax.experimental import pallas as pl
from jax.experimental.pallas import tpu as pltpu

F32 = jnp.float32
BF16 = jnp.bfloat16
EPS = 1e-6

A_HEADS = 8
A_DIM = 128
A_WIDTH = A_HEADS * A_DIM
B_GROUPS = 8
B_CHUNK = 128
B_WIDTH = B_GROUPS * 128
N_MOD = 6
SEG = 1024
SEG_Q, SEG_FFW, SEG_FBW, SEG_V, SEG_OG, SEG_ZU, SEG_ZV, SEG_GA, SEG_GB = 0, 1, 2, 3, 4, 5, 6, 7, 9

HGRN_CHUNK = 64
V7X_VMEM_BYTES = 64 * 1024 * 1024


def _vmem_limit(estimate_bytes):
    return int(min(estimate_bytes, V7X_VMEM_BYTES - 4 * 1024 * 1024))


def _sigmoid(x):
    return 1.0 / (1.0 + jnp.exp(-x))


def _dot(a, b):
    return jnp.dot(a, b, preferred_element_type=F32)


def _dot_nt(a, b):
    return lax.dot_general(a, b, (((1,), (1,)), ((), ())), preferred_element_type=F32)


def _dot_tn(a, b):
    return lax.dot_general(a, b, (((0,), (0,)), ((), ())), preferred_element_type=F32)


def _mod_kernel(c_ref, w_ref, b_ref, o_ref):
    c = c_ref[...]
    s = c * _sigmoid(c)
    o_ref[...] = _dot(s.astype(BF16), w_ref[...].astype(BF16)) + b_ref[...]


def _modulation(c8, w_ada, b_ada):
    d, n = w_ada.shape
    tn = 1024
    return pl.pallas_call(
        _mod_kernel,
        grid=(n // tn,),
        in_specs=[
            pl.BlockSpec((8, d), lambda j: (0, 0)),
            pl.BlockSpec((d, tn), lambda j: (0, j)),
            pl.BlockSpec((1, tn), lambda j: (0, j)),
        ],
        out_specs=pl.BlockSpec((8, tn), lambda j: (0, j)),
        out_shape=jax.ShapeDtypeStruct((8, n), F32),
        compiler_params=pltpu.CompilerParams(
            dimension_semantics=("arbitrary",),
            vmem_limit_bytes=_vmem_limit(2 * d * tn * 4 + d * tn * 2 + (8 << 20))),
        name="mod",
    )(c8, w_ada, b_ada.reshape(1, n))


def _inproj_kernel(x_ref, mod_ref, g_ref, w_ref, o_ref, a_scr, *, q_scale):
    j = pl.program_id(1)

    @pl.when(j == 0)
    def _():
        x = x_ref[...]
        ms = jnp.mean(x * x, axis=-1, keepdims=True)
        y = x * lax.rsqrt(ms + EPS) * g_ref[...]
        sh = mod_ref[0, 0:1, :]
        sc = mod_ref[0, 1:2, :]
        a_scr[...] = (y * (1.0 + sc) + sh).astype(BF16)

    acc = _dot(a_scr[...], w_ref[...])

    @pl.when(j == SEG_Q)
    def _():
        o_ref[...] = (acc * q_scale).astype(BF16)

    @pl.when((j == SEG_FFW) | (j == SEG_FBW) | (j == SEG_V))
    def _():
        o_ref[...] = acc.astype(BF16)

    @pl.when(j == SEG_OG)
    def _():
        o_ref[...] = (acc * _sigmoid(acc)).astype(BF16)

    @pl.when((j == SEG_ZU) | (j == SEG_ZV))
    def _():
        o_ref[...] = (0.5 * acc * (1.0 + lax.erf(acc * (2.0 ** -0.5)))).astype(BF16)

    @pl.when(j >= SEG_GA)
    def _():
        o_ref[...] = _sigmoid(acc).astype(BF16)


def _inproj(x2, mod3, g, w_bf16, rows_per_batch):
    m, d = x2.shape
    n = w_bf16.shape[1]
    tm, tn = 1024, SEG
    bpb = rows_per_batch // tm
    est = 2 * tm * d * 4 + tm * d * 2 + 2 * d * tn * 2 + 2 * tm * tn * 2 + 4 * tm * tn * 4 + (6 << 20)
    return pl.pallas_call(
        functools.partial(_inproj_kernel, q_scale=A_DIM ** -0.5),
        grid=(m // tm, n // tn),
        in_specs=[
            pl.BlockSpec((tm, d), lambda i, j: (i, 0)),
            pl.BlockSpec((1, N_MOD, d), lambda i, j: (i // bpb, 0, 0)),
            pl.BlockSpec((1, d), lambda i, j: (0, 0)),
            pl.BlockSpec((d, tn), lambda i, j: (0, j)),
        ],
        out_specs=pl.BlockSpec((tm, tn), lambda i, j: (i, j)),
        out_shape=jax.ShapeDtypeStruct((m, n), BF16),
        scratch_shapes=[pltpu.VMEM((tm, d), BF16)],
        compiler_params=pltpu.CompilerParams(
            dimension_semantics=("arbitrary", "arbitrary"),
            vmem_limit_bytes=_vmem_limit(est)),
        name="inproj",
    )(x2, mod3, g.reshape(1, d), w_bf16)


def _chunk_cumsum(x, reverse):
    t = x.shape[0]
    row = lax.broadcasted_iota(jnp.int32, x.shape, 0)
    s = 1
    while s < t:
        if reverse:
            shifted = pltpu.roll(x, t - s, axis=0)
            x = x + jnp.where(row < t - s, shifted, 0.0)
        else:
            shifted = pltpu.roll(x, s, axis=0)
            x = x + jnp.where(row >= s, shifted, 0.0)
        s *= 2
    return x


def _hgrn_kernel(q_ref, ffw_ref, fbw_ref, v_ref, og_ref, lbl_ref, gn_ref, o_ref,
                 ofw_scr, obw_scr, st_scr, *, heads_per_step, seq_len):
    t = HGRN_CHUNK
    n_chunks = seq_len // t
    half = t // 2

    lbl = lbl_ref[...]
    lbs = []
    for d in range(2):
        l0 = lbl[2 * d:2 * d + 1, :]
        l1 = lbl[2 * d + 1:2 * d + 2, :]
        mx = jnp.maximum(l0, l1)
        e0 = jnp.exp(l0 - mx)
        e1 = jnp.exp(l1 - mx)
        lbs.append(e0 / (e0 + e1))

    st_scr[...] = jnp.zeros_like(st_scr)

    row = lax.broadcasted_iota(jnp.int32, (t, t), 0)
    col = lax.broadcasted_iota(jnp.int32, (t, t), 1)
    masks = (col <= row, col >= row)

    def chain(c, h, d):
        rows = pl.ds(pl.multiple_of(c * t, t), t)
        hs = slice(h * A_DIM, (h + 1) * A_DIM)
        f_ref = ffw_ref if d == 0 else fbw_ref
        x = f_ref[rows, hs].astype(F32)
        q = q_ref[rows, hs].astype(F32)
        v = v_ref[rows, hs]
        lb = lbs[d][:, hs]
        r = _sigmoid(x)
        lf = jnp.log(lb + (1.0 - lb) * r)
        k = (1.0 - lb) * (1.0 - r)
        b = _chunk_cumsum(lf, reverse=(d == 1))
        if d == 0:
            b_end = b[t - 1:t, :]
            b_mid = b[half - 1:half, :]
        else:
            b_end = b[0:1, :]
            b_mid = b[half:half + 1, :]
        qd = (q * jnp.exp(b)).astype(BF16)
        ke = (k * jnp.exp(b_end - b)).astype(BF16)
        qm = (q * jnp.exp(b - b_mid)).astype(BF16)
        km = (k * jnp.exp(b_mid - b)).astype(BF16)
        att = jnp.where(masks[d], _dot_nt(qm, km), 0.0).astype(BF16)
        idx = 2 * h + d
        s_t = st_scr[idx]
        o = _dot(att, v) + _dot_nt(qd, s_t.astype(BF16))
        st_scr[idx] = s_t * jnp.exp(b_end) + _dot_tn(v, ke)
        o_scr = ofw_scr if d == 0 else obw_scr
        o_scr[rows, hs] = o

    def body(c, carry):
        for h in range(heads_per_step):
            chain(c, h, 0)
            chain(n_chunks - 1 - c, h, 1)
        return carry

    lax.fori_loop(0, n_chunks, body, 0)

    fin_rows = 256

    def fin(i, carry):
        rows = pl.ds(pl.multiple_of(i * fin_rows, fin_rows), fin_rows)
        for h in range(heads_per_step):
            hs = slice(h * A_DIM, (h + 1) * A_DIM)
            o = ofw_scr[rows, hs] + obw_scr[rows, hs]
            ms = jnp.mean(o * o, axis=-1, keepdims=True)
            y = o * lax.rsqrt(ms + EPS) * gn_ref[...]
            o_ref[rows, hs] = (y * og_ref[rows, hs].astype(F32)).astype(BF16)
        return carry

    lax.fori_loop(0, seq_len // fin_rows, fin, 0)


def _hgrn(proj, lb_logits4, g_norm, batch, seq_len):
    hps = 2
    wblk = hps * A_DIM
    nblk = A_WIDTH // wblk
    segs_per = SEG // wblk

    def seg_spec(seg):
        return pl.BlockSpec((seq_len, wblk), lambda b, p: (b, seg * segs_per + p))

    est = 2 * 6 * seq_len * wblk * 2 + 2 * seq_len * wblk * 4 + (8 << 20)
    return pl.pallas_call(
        functools.partial(_hgrn_kernel, heads_per_step=hps, seq_len=seq_len),
        grid=(batch, nblk),
        in_specs=[
            seg_spec(SEG_Q), seg_spec(SEG_FFW), seg_spec(SEG_FBW), seg_spec(SEG_V), seg_spec(SEG_OG),
            pl.BlockSpec((4, wblk), lambda b, p: (0, p)),
            pl.BlockSpec((1, A_DIM), lambda b, p: (0, 0)),
        ],
        out_specs=pl.BlockSpec((seq_len, wblk), lambda b, p: (b, p)),
        out_shape=jax.ShapeDtypeStruct((batch * seq_len, A_WIDTH), BF16),
        scratch_shapes=[
            pltpu.VMEM((seq_len, wblk), F32),
            pltpu.VMEM((seq_len, wblk), F32),
            pltpu.VMEM((2 * hps, A_DIM, A_DIM), F32),
        ],
        compiler_params=pltpu.CompilerParams(
            dimension_semantics=("arbitrary", "arbitrary"),
            vmem_limit_bytes=_vmem_limit(est)),
        name="hgrn",
    )(proj, proj, proj, proj, proj, lb_logits4, g_norm.reshape(1, A_DIM))


def _sgu_kernel(u_ref, v_ref, g_ref, ws_ref, bs_ref, o_ref, *, chunks_per_step):
    v = v_ref[...].astype(F32)
    mu = jnp.mean(v, axis=-1, keepdims=True)
    dlt = v - mu
    y = (dlt * lax.rsqrt(jnp.mean(dlt * dlt, axis=-1, keepdims=True) + EPS) * g_ref[...]).astype(BF16)
    for n in range(chunks_per_step):
        rs = slice(n * B_CHUNK, (n + 1) * B_CHUNK)
        for g in range(B_GROUPS):
            cs = slice(g * 128, (g + 1) * 128)
            vm = _dot(ws_ref[g].astype(BF16), y[rs, cs]) + bs_ref[:, cs]
            o_ref[rs, cs] = (u_ref[rs, cs].astype(F32) * vm).astype(BF16)


def _sgu(proj, g_v, w_s, bias_full):
    m = proj.shape[0]
    cps = 4
    tm = cps * B_CHUNK
    return pl.pallas_call(
        functools.partial(_sgu_kernel, chunks_per_step=cps),
        grid=(m // tm,),
        in_specs=[
            pl.BlockSpec((tm, B_WIDTH), lambda i: (i, SEG_ZU)),
            pl.BlockSpec((tm, B_WIDTH), lambda i: (i, SEG_ZV)),
            pl.BlockSpec((1, B_WIDTH), lambda i: (0, 0)),
            pl.BlockSpec((B_GROUPS, B_CHUNK, B_CHUNK), lambda i: (0, 0, 0)),
            pl.BlockSpec((B_CHUNK, B_WIDTH), lambda i: (0, 0)),
        ],
        out_specs=pl.BlockSpec((tm, B_WIDTH), lambda i: (i, 0)),
        out_shape=jax.ShapeDtypeStruct((m, B_WIDTH), BF16),
        compiler_params=pltpu.CompilerParams(
            dimension_semantics=("arbitrary",),
            vmem_limit_bytes=_vmem_limit(32 << 20)),
        name="sgu",
    )(proj, proj, g_v.reshape(1, B_WIDTH), w_s, bias_full)


def _merge_kernel(hg_ref, sg_ref, ga0_ref, ga1_ref, gb0_ref, gb1_ref, x_ref, mod_ref, gpost_ref,
                  gpre_ref, wa_ref, wb_ref, wo_ref, h_ref, a_ref):
    ya = _dot(hg_ref[...], wa_ref[...])
    yb = _dot(sg_ref[...], wb_ref[...])
    ga = jnp.concatenate([ga0_ref[...], ga1_ref[...]], axis=-1).astype(F32)
    gb = jnp.concatenate([gb0_ref[...], gb1_ref[...]], axis=-1).astype(F32)
    merged = ga * ya + gb * yb
    mo = _dot(merged.astype(BF16), wo_ref[...])
    gt1 = mod_ref[0, 2:3, :]
    sh2 = mod_ref[0, 3:4, :]
    sc2 = mod_ref[0, 4:5, :]
    n1 = mo * lax.rsqrt(jnp.mean(mo * mo, axis=-1, keepdims=True) + EPS) * gpost_ref[...]
    h = x_ref[...] + gt1 * n1
    h_ref[...] = h
    n2 = h * lax.rsqrt(jnp.mean(h * h, axis=-1, keepdims=True) + EPS) * gpre_ref[...]
    a_ref[...] = (n2 * (1.0 + sc2) + sh2).astype(BF16)


def _merge(hg, sg, proj, x2, mod3, g_post, g_pre, wa, wb, wo, rows_per_batch):
    m, d = x2.shape
    tm = 256
    bpb = rows_per_batch // tm
    const = lambda shape: pl.BlockSpec(shape, lambda i: (0,) * len(shape),
                                       pipeline_mode=pl.Buffered(1))
    est = (2 * A_WIDTH * d + d * d) * 2 + 2 * tm * (2 * A_WIDTH * 2 + 2 * d * 2 + d * 4 + d * 4 + d * 2) \
        + 5 * tm * d * 4 + (6 << 20)
    return pl.pallas_call(
        _merge_kernel,
        grid=(m // tm,),
        in_specs=[
            pl.BlockSpec((tm, A_WIDTH), lambda i: (i, 0)),
            pl.BlockSpec((tm, B_WIDTH), lambda i: (i, 0)),
            pl.BlockSpec((tm, SEG), lambda i: (i, SEG_GA)),
            pl.BlockSpec((tm, SEG), lambda i: (i, SEG_GA + 1)),
            pl.BlockSpec((tm, SEG), lambda i: (i, SEG_GB)),
            pl.BlockSpec((tm, SEG), lambda i: (i, SEG_GB + 1)),
            pl.BlockSpec((tm, d), lambda i: (i, 0)),
            pl.BlockSpec((1, N_MOD, d), lambda i: (i // bpb, 0, 0)),
            pl.BlockSpec((1, d), lambda i: (0, 0)),
            pl.BlockSpec((1, d), lambda i: (0, 0)),
            const((A_WIDTH, d)),
            const((B_WIDTH, d)),
            const((d, d)),
        ],
        out_specs=[
            pl.BlockSpec((tm, d), lambda i: (i, 0)),
            pl.BlockSpec((tm, d), lambda i: (i, 0)),
        ],
        out_shape=[jax.ShapeDtypeStruct((m, d), F32), jax.ShapeDtypeStruct((m, d), BF16)],
        compiler_params=pltpu.CompilerParams(
            dimension_semantics=("arbitrary",),
            vmem_limit_bytes=_vmem_limit(est)),
        name="merge",
    )(hg, sg, proj, proj, proj, proj, x2, mod3, g_post.reshape(1, d), g_pre.reshape(1, d), wa, wb, wo)


def _ffn_kernel(a_ref, h_ref, mod_ref, g_ref, w1_ref, w2_ref, o_ref):
    k = pl.program_id(1)
    hk = jnp.maximum(_dot(a_ref[...], w1_ref[...]), 0.0)
    part = _dot((hk * hk).astype(BF16), w2_ref[...])

    @pl.when(k == 0)
    def _():
        o_ref[...] = part

    @pl.when(k > 0)
    def _():
        o_ref[...] = o_ref[...] + part

    @pl.when(k == pl.num_programs(1) - 1)
    def _():
        ff = o_ref[...]
        gt2 = mod_ref[0, 5:6, :]
        n = ff * lax.rsqrt(jnp.mean(ff * ff, axis=-1, keepdims=True) + EPS) * g_ref[...]
        o_ref[...] = h_ref[...] + gt2 * n


def _ffn(a2, h1, mod3, g_post, w1, w2, rows_per_batch):
    m, d = h1.shape
    dff = w1.shape[1]
    tm, tk = 512, 1024
    bpb = rows_per_batch // tm
    est = 2 * tm * d * (2 + 4 + 4) + 2 * 2 * d * tk * 2 + tm * tk * 6 + 2 * tm * d * 4 + (6 << 20)
    return pl.pallas_call(
        _ffn_kernel,
        grid=(m // tm, dff // tk),
        in_specs=[
            pl.BlockSpec((tm, d), lambda i, k: (i, 0)),
            pl.BlockSpec((tm, d), lambda i, k: (i, 0)),
            pl.BlockSpec((1, N_MOD, d), lambda i, k: (i // bpb, 0, 0)),
            pl.BlockSpec((1, d), lambda i, k: (0, 0)),
            pl.BlockSpec((d, tk), lambda i, k: (0, k)),
            pl.BlockSpec((tk, d), lambda i, k: (k, 0)),
        ],
        out_specs=pl.BlockSpec((tm, d), lambda i, k: (i, 0)),
        out_shape=jax.ShapeDtypeStruct((m, d), F32),
        compiler_params=pltpu.CompilerParams(
            dimension_semantics=("arbitrary", "arbitrary"),
            vmem_limit_bytes=_vmem_limit(est)),
        name="ffn",
    )(a2, h1, mod3, g_post.reshape(1, d), w1, w2)


def kernel(x, c, w_ada, b_ada, g_pre_mix, g_post_mix, g_pre_ffn, g_post_ffn, w_in, lb_logits,
           g_hgrn_norm, w_a_out, g_sgu_norm, w_spatial, b_spatial, w_b_out, w_o, w_ff1, w_ff2):
    batch, seq_len, d = x.shape
    depth = w_in.shape[0]
    assert depth == 1 and lb_logits.shape == (2, depth + 1, A_WIDTH)
    assert w_in.shape[2] == 11 * SEG and d == 2 * SEG
    assert g_hgrn_norm.shape == (depth, A_DIM) and w_spatial.shape == (depth, B_GROUPS, B_CHUNK, B_CHUNK)
    assert seq_len % 1024 == 0
    m = batch * seq_len

    c8 = jnp.zeros((8, d), F32).at[:batch].set(c)
    mod = _modulation(c8, w_ada[0], b_ada[0])
    mod3 = mod[:batch].reshape(batch, N_MOD, d)

    x2 = x.reshape(m, d)
    proj = _inproj(x2, mod3, g_pre_mix[0], w_in[0].astype(BF16), seq_len)

    hg = _hgrn(proj, lb_logits[:, :, :].reshape(2 * (depth + 1), A_WIDTH), g_hgrn_norm[0], batch, seq_len)

    bias_full = jnp.repeat(b_spatial[0].T, 128, axis=1)
    sg = _sgu(proj, g_sgu_norm[0], w_spatial[0], bias_full)

    h1, a2 = _merge(hg, sg, proj, x2, mod3, g_post_mix[0], g_pre_ffn[0],
                    w_a_out[0].astype(BF16), w_b_out[0].astype(BF16), w_o[0].astype(BF16), seq_len)

    out = _ffn(a2, h1, mod3, g_post_ffn[0], w_ff1[0].astype(BF16), w_ff2[0].astype(BF16), seq_len)
    return out.reshape(batch, seq_len, d)
```

```python
import functools

import jax
import jax.numpy as jnp
from jax import lax
from jax.experimental import pallas as pl
from jax.experimental.pallas import tpu as pltpu

F32 = jnp.float32
BF16 = jnp.bfloat16
EPS = 1e-6

A_HEADS = 8
A_DIM = 128
A_WIDTH = A_HEADS * A_DIM
B_GROUPS = 8
B_CHUNK = 128
B_WIDTH = B_GROUPS * 128
N_MOD = 6
SEG = 1024
SEG_LIN, SEG_OG, SEG_Z, SEG_GATE = 0, 4, 5, 7

HGRN_CHUNK = 64
V7X_VMEM_BYTES = 64 * 1024 * 1024


def _vmem_limit(estimate_bytes):
    return int(min(estimate_bytes, V7X_VMEM_BYTES - 4 * 1024 * 1024))


def _sigmoid(x):
    return 1.0 / (1.0 + jnp.exp(-x))


def _dot(a, b):
    return jnp.dot(a, b, preferred_element_type=F32)


def _dot_nt(a, b):
    return lax.dot_general(a, b, (((1,), (1,)), ((), ())), preferred_element_type=F32)


def _dot_tn(a, b):
    return lax.dot_general(a, b, (((0,), (0,)), ((), ())), preferred_element_type=F32)


def _rms(x):
    return x * lax.rsqrt(jnp.mean(x * x, axis=-1, keepdims=True) + EPS)


def _mod_kernel(c_ref, w_ref, b_ref, o_ref):
    c = c_ref[...]
    s = c * _sigmoid(c)
    o_ref[...] = _dot(s.astype(BF16), w_ref[...].astype(BF16)) + b_ref[...]


def _modulation(c8, w_ada, b_ada):
    d, n = w_ada.shape
    tn = 1024
    return pl.pallas_call(
        _mod_kernel,
        grid=(n // tn,),
        in_specs=[
            pl.BlockSpec((8, d), lambda j: (0, 0)),
            pl.BlockSpec((d, tn), lambda j: (0, j)),
            pl.BlockSpec((1, tn), lambda j: (0, j)),
        ],
        out_specs=pl.BlockSpec((8, tn), lambda j: (0, j)),
        out_shape=jax.ShapeDtypeStruct((8, n), F32),
        compiler_params=pltpu.CompilerParams(
            dimension_semantics=("arbitrary",),
            vmem_limit_bytes=_vmem_limit(2 * d * tn * 4 + d * tn * 2 + (8 << 20))),
        name="mod",
    )(c8, w_ada, b_ada.reshape(1, n))


def _norm_kernel(x_ref, mod_ref, g_ref, o_ref):
    y = _rms(x_ref[...]) * g_ref[...]
    o_ref[...] = (y * (1.0 + mod_ref[0, 1:2, :]) + mod_ref[0, 0:1, :]).astype(BF16)


def _norm_mod(x2, mod3, g, rows_per_batch):
    m, d = x2.shape
    tm = 512
    bpb = rows_per_batch // tm
    return pl.pallas_call(
        _norm_kernel,
        grid=(m // tm,),
        in_specs=[
            pl.BlockSpec((tm, d), lambda i: (i, 0)),
            pl.BlockSpec((1, N_MOD, d), lambda i: (i // bpb, 0, 0)),
            pl.BlockSpec((1, d), lambda i: (0, 0)),
        ],
        out_specs=pl.BlockSpec((tm, d), lambda i: (i, 0)),
        out_shape=jax.ShapeDtypeStruct((m, d), BF16),
        compiler_params=pltpu.CompilerParams(
            dimension_semantics=("arbitrary",),
            vmem_limit_bytes=_vmem_limit(2 * tm * d * 6 + 3 * tm * d * 4 + (4 << 20))),
        name="norm1",
    )(x2, mod3, g.reshape(1, d))


def _act_linear(acc, j):
    return acc * jnp.where(j == 0, A_DIM ** -0.5, 1.0)


def _act_silu(acc, j):
    return acc * _sigmoid(acc)


def _act_gelu(acc, j):
    return 0.5 * acc * (1.0 + lax.erf(acc * (2.0 ** -0.5)))


def _act_sigmoid(acc, j):
    return _sigmoid(acc)


def _act_relu2(acc, j):
    r = jnp.maximum(acc, 0.0)
    return r * r


def _proj_kernel(a_ref, w_ref, o_ref, w_scr, *, act):
    j = pl.program_id(0)

    @pl.when(pl.program_id(1) == 0)
    def _():
        w_scr[...] = w_ref[...].astype(BF16)

    o_ref[...] = act(_dot(a_ref[...], w_scr[...]), j).astype(BF16)


def _proj(a, w, col0, ncols, act, name, tm=1024, tn=SEG):
    m, kdim = a.shape
    est = 2 * tm * kdim * 2 + 2 * kdim * tn * 4 + kdim * tn * 2 + 2 * tm * tn * 2 + 3 * tm * tn * 4 + (4 << 20)
    return pl.pallas_call(
        functools.partial(_proj_kernel, act=act),
        grid=(ncols // tn, m // tm),
        in_specs=[
            pl.BlockSpec((tm, kdim), lambda j, i: (i, 0)),
            pl.BlockSpec((kdim, tn), lambda j, i: (0, col0 // tn + j)),
        ],
        out_specs=pl.BlockSpec((tm, tn), lambda j, i: (i, j)),
        out_shape=jax.ShapeDtypeStruct((m, ncols), BF16),
        scratch_shapes=[pltpu.VMEM((kdim, tn), BF16)],
        compiler_params=pltpu.CompilerParams(
            dimension_semantics=("arbitrary", "arbitrary"),
            vmem_limit_bytes=_vmem_limit(est)),
        name=name,
    )(a, w)


def _chunk_cumsum(x, reverse):
    t = x.shape[0]
    row = lax.broadcasted_iota(jnp.int32, x.shape, 0)
    s = 1
    while s < t:
        if reverse:
            shifted = pltpu.roll(x, t - s, axis=0)
            x = x + jnp.where(row < t - s, shifted, 0.0)
        else:
            shifted = pltpu.roll(x, s, axis=0)
            x = x + jnp.where(row >= s, shifted, 0.0)
        s *= 2
    return x


def _hgrn_kernel(q_ref, ffw_ref, fbw_ref, v_ref, og_ref, lbl_ref, gn_ref, o_ref,
                 ofw_scr, obw_scr, st_scr, *, heads_per_step, seq_len):
    t = HGRN_CHUNK
    n_chunks = seq_len // t
    half = t // 2

    lbl = lbl_ref[...]
    lbs = []
    for d in range(2):
        l0 = lbl[2 * d:2 * d + 1, :]
        l1 = lbl[2 * d + 1:2 * d + 2, :]
        mx = jnp.maximum(l0, l1)
        e0 = jnp.exp(l0 - mx)
        e1 = jnp.exp(l1 - mx)
        lbs.append(e0 / (e0 + e1))

    st_scr[...] = jnp.zeros_like(st_scr)

    row = lax.broadcasted_iota(jnp.int32, (t, t), 0)
    col = lax.broadcasted_iota(jnp.int32, (t, t), 1)
    masks = (col <= row, col >= row)

    def chain(c, h, d):
        rows = pl.ds(pl.multiple_of(c * t, t), t)
        hs = slice(h * A_DIM, (h + 1) * A_DIM)
        f_ref = ffw_ref if d == 0 else fbw_ref
        x = f_ref[rows, hs].astype(F32)
        q = q_ref[rows, hs].astype(F32)
        v = v_ref[rows, hs]
        lb = lbs[d][:, hs]
        r = _sigmoid(x)
        lf = jnp.log(lb + (1.0 - lb) * r)
        k = (1.0 - lb) * (1.0 - r)
        b = _chunk_cumsum(lf, reverse=(d == 1))
        if d == 0:
            b_end = b[t - 1:t, :]
            b_mid = b[half - 1:half, :]
        else:
            b_end = b[0:1, :]
            b_mid = b[half:half + 1, :]
        qd = (q * jnp.exp(b)).astype(BF16)
        ke = (k * jnp.exp(b_end - b)).astype(BF16)
        qm = (q * jnp.exp(b - b_mid)).astype(BF16)
        km = (k * jnp.exp(b_mid - b)).astype(BF16)
        att = jnp.where(masks[d], _dot_nt(qm, km), 0.0).astype(BF16)
        idx = 2 * h + d
        s_t = st_scr[idx]
        o = _dot(att, v) + _dot_nt(qd, s_t.astype(BF16))
        st_scr[idx] = s_t * jnp.exp(b_end) + _dot_tn(v, ke)
        o_scr = ofw_scr if d == 0 else obw_scr
        o_scr[rows, hs] = o

    def body(c, carry):
        for h in range(heads_per_step):
            chain(c, h, 0)
            chain(n_chunks - 1 - c, h, 1)
        return carry

    lax.fori_loop(0, n_chunks, body, 0, unroll=2)

    fin_rows = 256

    def fin(i, carry):
        rows = pl.ds(pl.multiple_of(i * fin_rows, fin_rows), fin_rows)
        for h in range(heads_per_step):
            hs = slice(h * A_DIM, (h + 1) * A_DIM)
            y = _rms(ofw_scr[rows, hs] + obw_scr[rows, hs]) * gn_ref[...]
            o_ref[rows, hs] = (y * og_ref[rows, hs].astype(F32)).astype(BF16)
        return carry

    lax.fori_loop(0, seq_len // fin_rows, fin, 0)


def _hgrn(p_lin, p_og, lb_logits4, g_norm, batch, seq_len):
    hps = 2
    wblk = hps * A_DIM
    nblk = A_WIDTH // wblk

    def seg_spec(seg):
        return pl.BlockSpec((seq_len, wblk), lambda b, p: (b, seg * nblk + p))

    est = 2 * 6 * seq_len * wblk * 2 + 2 * seq_len * wblk * 4 + (8 << 20)
    return pl.pallas_call(
        functools.partial(_hgrn_kernel, heads_per_step=hps, seq_len=seq_len),
        grid=(batch, nblk),
        in_specs=[
            seg_spec(0), seg_spec(1), seg_spec(2), seg_spec(3), seg_spec(0),
            pl.BlockSpec((4, wblk), lambda b, p: (0, p)),
            pl.BlockSpec((1, A_DIM), lambda b, p: (0, 0)),
        ],
        out_specs=pl.BlockSpec((seq_len, wblk), lambda b, p: (b, p)),
        out_shape=jax.ShapeDtypeStruct((batch * seq_len, A_WIDTH), BF16),
        scratch_shapes=[
            pltpu.VMEM((seq_len, wblk), F32),
            pltpu.VMEM((seq_len, wblk), F32),
            pltpu.VMEM((2 * hps, A_DIM, A_DIM), F32),
        ],
        compiler_params=pltpu.CompilerParams(
            dimension_semantics=("arbitrary", "arbitrary"),
            vmem_limit_bytes=_vmem_limit(est)),
        name="hgrn",
    )(p_lin, p_lin, p_lin, p_lin, p_og, lb_logits4, g_norm.reshape(1, A_DIM))


def _sgu_kernel(u_ref, v_ref, g_ref, ws_ref, bs_ref, o_ref, *, chunks_per_step):
    v = v_ref[...].astype(F32)
    mu = jnp.mean(v, axis=-1, keepdims=True)
    dlt = v - mu
    y = (dlt * lax.rsqrt(jnp.mean(dlt * dlt, axis=-1, keepdims=True) + EPS) * g_ref[...]).astype(BF16)
    for n in range(chunks_per_step):
        rs = slice(n * B_CHUNK, (n + 1) * B_CHUNK)
        for g in range(B_GROUPS):
            cs = slice(g * 128, (g + 1) * 128)
            vm = _dot(ws_ref[g].astype(BF16), y[rs, cs]) + bs_ref[:, cs]
            o_ref[rs, cs] = (u_ref[rs, cs].astype(F32) * vm).astype(BF16)


def _sgu(p_z, g_v, w_s, bias_full):
    m = p_z.shape[0]
    cps = 4
    tm = cps * B_CHUNK
    return pl.pallas_call(
        functools.partial(_sgu_kernel, chunks_per_step=cps),
        grid=(m // tm,),
        in_specs=[
            pl.BlockSpec((tm, B_WIDTH), lambda i: (i, 0)),
            pl.BlockSpec((tm, B_WIDTH), lambda i: (i, 1)),
            pl.BlockSpec((1, B_WIDTH), lambda i: (0, 0)),
            pl.BlockSpec((B_GROUPS, B_CHUNK, B_CHUNK), lambda i: (0, 0, 0)),
            pl.BlockSpec((B_CHUNK, B_WIDTH), lambda i: (0, 0)),
        ],
        out_specs=pl.BlockSpec((tm, B_WIDTH), lambda i: (i, 0)),
        out_shape=jax.ShapeDtypeStruct((m, B_WIDTH), BF16),
        compiler_params=pltpu.CompilerParams(
            dimension_semantics=("arbitrary",),
            vmem_limit_bytes=_vmem_limit(32 << 20)),
        name="sgu",
    )(p_z, p_z, g_v.reshape(1, B_WIDTH), w_s, bias_full)


def _merge_kernel(hg_ref, sg_ref, ga_ref, gb_ref, wa_ref, wb_ref, o_ref, wa_scr, wb_scr):
    @pl.when(pl.program_id(1) == 0)
    def _():
        wa_scr[...] = wa_ref[...].astype(BF16)
        wb_scr[...] = wb_ref[...].astype(BF16)

    ya = _dot(hg_ref[...], wa_scr[...])
    yb = _dot(sg_ref[...], wb_scr[...])
    o_ref[...] = (ga_ref[...].astype(F32) * ya + gb_ref[...].astype(F32) * yb).astype(BF16)


def _merge(hg, sg, p_gate, wa, wb):
    m = hg.shape[0]
    d = wa.shape[1]
    tm, tn = 512, 1024
    nj = d // tn
    est = 2 * 2 * tm * A_WIDTH * 2 + 2 * 2 * tm * tn * 2 + 2 * 2 * A_WIDTH * tn * 4 + 2 * A_WIDTH * tn * 2 \
        + 2 * tm * tn * 2 + 4 * tm * tn * 4 + (4 << 20)
    return pl.pallas_call(
        _merge_kernel,
        grid=(nj, m // tm),
        in_specs=[
            pl.BlockSpec((tm, A_WIDTH), lambda j, i: (i, 0)),
            pl.BlockSpec((tm, B_WIDTH), lambda j, i: (i, 0)),
            pl.BlockSpec((tm, tn), lambda j, i: (i, j)),
            pl.BlockSpec((tm, tn), lambda j, i: (i, nj + j)),
            pl.BlockSpec((A_WIDTH, tn), lambda j, i: (0, j)),
            pl.BlockSpec((B_WIDTH, tn), lambda j, i: (0, j)),
        ],
        out_specs=pl.BlockSpec((tm, tn), lambda j, i: (i, j)),
        out_shape=jax.ShapeDtypeStruct((m, d), BF16),
        scratch_shapes=[pltpu.VMEM((A_WIDTH, tn), BF16), pltpu.VMEM((B_WIDTH, tn), BF16)],
        compiler_params=pltpu.CompilerParams(
            dimension_semantics=("arbitrary", "arbitrary"),
            vmem_limit_bytes=_vmem_limit(est)),
        name="merge",
    )(hg, sg, p_gate, p_gate, wa, wb)


def _resproj_kernel(*refs, gate_row, emit_next):
    if emit_next:
        lhs_ref, w_ref, res_ref, mod_ref, gpost_ref, gnext_ref, h_ref, a_ref = refs
    else:
        lhs_ref, w_ref, res_ref, mod_ref, gpost_ref, h_ref = refs
    k = pl.program_id(1)

    @pl.when(k == 0)
    def _():
        h_ref[...] = jnp.zeros_like(h_ref)

    h_ref[...] += _dot(lhs_ref[...], w_ref[...].astype(BF16))

    @pl.when(k == pl.num_programs(1) - 1)
    def _():
        gate = mod_ref[0, gate_row:gate_row + 1, :]
        h = res_ref[...] + gate * (_rms(h_ref[...]) * gpost_ref[...])
        h_ref[...] = h
        if emit_next:
            sh = mod_ref[0, 3:4, :]
            sc = mod_ref[0, 4:5, :]
            a_ref[...] = (_rms(h) * gnext_ref[...] * (1.0 + sc) + sh).astype(BF16)


def _resproj(lhs, w, res, mod3, g_post, g_next, gate_row, rows_per_batch, tm, tk, name):
    m, kdim = lhs.shape
    d = w.shape[1]
    emit_next = g_next is not None
    bpb = rows_per_batch // tm
    row_spec = pl.BlockSpec((tm, d), lambda i, k: (i, 0))
    vec_spec = pl.BlockSpec((1, d), lambda i, k: (0, 0))
    in_specs = [
        pl.BlockSpec((tm, tk), lambda i, k: (i, k)),
        pl.BlockSpec((tk, d), lambda i, k: (k, 0)),
        row_spec,
        pl.BlockSpec((1, N_MOD, d), lambda i, k: (i // bpb, 0, 0)),
        vec_spec,
    ]
    args = [lhs, w, res, mod3, g_post.reshape(1, d)]
    out_specs = [row_spec]
    out_shape = [jax.ShapeDtypeStruct((m, d), F32)]
    if emit_next:
        in_specs.append(vec_spec)
        args.append(g_next.reshape(1, d))
        out_specs.append(row_spec)
        out_shape.append(jax.ShapeDtypeStruct((m, d), BF16))
    est = 2 * tm * tk * 2 + 2 * tk * d * 4 + tk * d * 2 + 4 * tm * d * 4 + (2 * tm * d * 2 if emit_next else 0) \
        + 2 * tm * d * 4 + (4 << 20)
    return pl.pallas_call(
        functools.partial(_resproj_kernel, gate_row=gate_row, emit_next=emit_next),
        grid=(m // tm, kdim // tk),
        in_specs=in_specs,
        out_specs=out_specs,
        out_shape=out_shape,
        compiler_params=pltpu.CompilerParams(
            dimension_semantics=("arbitrary", "arbitrary"),
            vmem_limit_bytes=_vmem_limit(est)),
        name=name,
    )(*args)


def kernel(x, c, w_ada, b_ada, g_pre_mix, g_post_mix, g_pre_ffn, g_post_ffn, w_in, lb_logits,
           g_hgrn_norm, w_a_out, g_sgu_norm, w_spatial, b_spatial, w_b_out, w_o, w_ff1, w_ff2):
    batch, seq_len, d = x.shape
    depth = w_in.shape[0]
    assert depth == 1 and lb_logits.shape == (2, depth + 1, A_WIDTH)
    assert w_in.shape[2] == 11 * SEG and d == 2 * SEG
    assert g_hgrn_norm.shape == (depth, A_DIM) and w_spatial.shape == (depth, B_GROUPS, B_CHUNK, B_CHUNK)
    assert seq_len % 1024 == 0
    m = batch * seq_len

    c8 = jnp.zeros((8, d), F32).at[:batch].set(c)
    mod = _modulation(c8, w_ada[0], b_ada[0])
    mod3 = mod[:batch].reshape(batch, N_MOD, d)

    x2 = x.reshape(m, d)
    a1 = _norm_mod(x2, mod3, g_pre_mix[0], seq_len)
    w_in0 = w_in[0]
    p_lin = _proj(a1, w_in0, SEG_LIN * SEG, 4 * SEG, _act_linear, "inproj_lin")
    p_og = _proj(a1, w_in0, SEG_OG * SEG, SEG, _act_silu, "inproj_silu")
    p_z = _proj(a1, w_in0, SEG_Z * SEG, 2 * SEG, _act_gelu, "inproj_gelu")
    p_gate = _proj(a1, w_in0, SEG_GATE * SEG, 4 * SEG, _act_sigmoid, "inproj_sig")

    hg = _hgrn(p_lin, p_og, lb_logits.reshape(2 * (depth + 1), A_WIDTH), g_hgrn_norm[0], batch, seq_len)

    bias_full = jnp.repeat(b_spatial[0].T, 128, axis=1)
    sg = _sgu(p_z, g_sgu_norm[0], w_spatial[0], bias_full)

    merged = _merge(hg, sg, p_gate, w_a_out[0], w_b_out[0])
    h1, a2 = _resproj(merged, w_o[0], x2, mod3, g_post_mix[0], g_pre_ffn[0], 2, seq_len,
                      tm=512, tk=512, name="wo")
    hid = _proj(a2, w_ff1[0], 0, w_ff1.shape[2], _act_relu2, "ff1")
    (out,) = _resproj(hid, w_ff2[0], h1, mod3, g_post_ffn[0], None, 5, seq_len,
                      tm=1024, tk=512, name="ff2")
    return out.reshape(batch, seq_len, d)
```

```python
import functools

import jax
import jax.numpy as jnp
from jax import lax
from jax.experimental import pallas as pl
from jax.experimental.pallas import tpu as pltpu

F32 = jnp.float32
BF16 = jnp.bfloat16
EPS = 1e-6

A_HEADS = 8
A_DIM = 128
A_WIDTH = A_HEADS * A_DIM
B_GROUPS = 8
B_CHUNK = 128
B_WIDTH = B_GROUPS * 128
N_MOD = 6
SEG = 1024
SEG_LIN, SEG_OG, SEG_Z, SEG_GATE = 0, 4, 5, 7

HGRN_CHUNK = 64
V7X_VMEM_BYTES = 64 * 1024 * 1024


def _vmem_limit(estimate_bytes):
    return int(min(estimate_bytes, V7X_VMEM_BYTES - 4 * 1024 * 1024))


def _sigmoid(x):
    return 1.0 / (1.0 + jnp.exp(-x))


def _dot(a, b):
    return jnp.dot(a, b, preferred_element_type=F32)


def _dot_nt(a, b):
    return lax.dot_general(a, b, (((1,), (1,)), ((), ())), preferred_element_type=F32)


def _dot_tn(a, b):
    return lax.dot_general(a, b, (((0,), (0,)), ((), ())), preferred_element_type=F32)


def _rms(x):
    return x * lax.rsqrt(jnp.mean(x * x, axis=-1, keepdims=True) + EPS)


def _mod_kernel(c_ref, w_ref, b_ref, o_ref):
    c = c_ref[...]
    s = c * _sigmoid(c)
    o_ref[...] = _dot(s.astype(BF16), w_ref[...].astype(BF16)) + b_ref[...]


def _modulation(c8, w_ada, b_ada):
    d, n = w_ada.shape
    tn = 1024
    return pl.pallas_call(
        _mod_kernel,
        grid=(n // tn,),
        in_specs=[
            pl.BlockSpec((8, d), lambda j: (0, 0)),
            pl.BlockSpec((d, tn), lambda j: (0, j)),
            pl.BlockSpec((1, tn), lambda j: (0, j)),
        ],
        out_specs=pl.BlockSpec((8, tn), lambda j: (0, j)),
        out_shape=jax.ShapeDtypeStruct((8, n), F32),
        compiler_params=pltpu.CompilerParams(
            dimension_semantics=("arbitrary",),
            vmem_limit_bytes=_vmem_limit(2 * d * tn * 4 + d * tn * 2 + (8 << 20))),
        name="mod",
    )(c8, w_ada, b_ada.reshape(1, n))


def _norm_kernel(x_ref, mod_ref, g_ref, o_ref):
    y = _rms(x_ref[...]) * g_ref[...]
    o_ref[...] = (y * (1.0 + mod_ref[0, 1:2, :]) + mod_ref[0, 0:1, :]).astype(BF16)


def _norm_mod(x2, mod3, g, rows_per_batch):
    m, d = x2.shape
    tm = 512
    bpb = rows_per_batch // tm
    return pl.pallas_call(
        _norm_kernel,
        grid=(m // tm,),
        in_specs=[
            pl.BlockSpec((tm, d), lambda i: (i, 0)),
            pl.BlockSpec((1, N_MOD, d), lambda i: (i // bpb, 0, 0)),
            pl.BlockSpec((1, d), lambda i: (0, 0)),
        ],
        out_specs=pl.BlockSpec((tm, d), lambda i: (i, 0)),
        out_shape=jax.ShapeDtypeStruct((m, d), BF16),
        compiler_params=pltpu.CompilerParams(
            dimension_semantics=("arbitrary",),
            vmem_limit_bytes=_vmem_limit(2 * tm * d * 6 + 3 * tm * d * 4 + (4 << 20))),
        name="norm1",
    )(x2, mod3, g.reshape(1, d))


def _act_linear(acc, j):
    return acc * jnp.where(j == 0, A_DIM ** -0.5, 1.0)


def _act_silu(acc, j):
    return acc * _sigmoid(acc)


def _act_gelu(acc, j):
    return 0.5 * acc * (1.0 + lax.erf(acc * (2.0 ** -0.5)))


def _act_sigmoid(acc, j):
    return _sigmoid(acc)


def _act_relu2(acc, j):
    r = jnp.maximum(acc, 0.0)
    return r * r


def _proj_kernel(a_ref, w_ref, o_ref, w_scr, *, act):
    j = pl.program_id(0)

    @pl.when(pl.program_id(1) == 0)
    def _():
        w_scr[...] = w_ref[...].astype(BF16)

    o_ref[...] = act(_dot(a_ref[...], w_scr[...]), j).astype(BF16)


def _proj(a, w, col0, ncols, act, name, tm=1024, tn=SEG):
    m, kdim = a.shape
    est = 2 * tm * kdim * 2 + 2 * kdim * tn * 4 + kdim * tn * 2 + 2 * tm * tn * 2 + 3 * tm * tn * 4 + (4 << 20)
    return pl.pallas_call(
        functools.partial(_proj_kernel, act=act),
        grid=(ncols // tn, m // tm),
        in_specs=[
            pl.BlockSpec((tm, kdim), lambda j, i: (i, 0)),
            pl.BlockSpec((kdim, tn), lambda j, i: (0, col0 // tn + j)),
        ],
        out_specs=pl.BlockSpec((tm, tn), lambda j, i: (i, j)),
        out_shape=jax.ShapeDtypeStruct((m, ncols), BF16),
        scratch_shapes=[pltpu.VMEM((kdim, tn), BF16)],
        compiler_params=pltpu.CompilerParams(
            dimension_semantics=("arbitrary", "arbitrary"),
            vmem_limit_bytes=_vmem_limit(est)),
        name=name,
    )(a, w)


def _chunk_cumsum(x, reverse):
    t = x.shape[0]
    row = lax.broadcasted_iota(jnp.int32, x.shape, 0)
    s = 1
    while s < t:
        if reverse:
            shifted = pltpu.roll(x, t - s, axis=0)
            x = x + jnp.where(row < t - s, shifted, 0.0)
        else:
            shifted = pltpu.roll(x, s, axis=0)
            x = x + jnp.where(row >= s, shifted, 0.0)
        s *= 2
    return x


def _hgrn_kernel(q_ref, ffw_ref, fbw_ref, v_ref, og_ref, lbl_ref, gn_ref, o_ref,
                 ofw_scr, obw_scr, st_scr, *, heads_per_step, seq_len):
    t = HGRN_CHUNK
    n_chunks = seq_len // t
    half = t // 2

    lbl = lbl_ref[...]
    lbs = []
    for d in range(2):
        l0 = lbl[2 * d:2 * d + 1, :]
        l1 = lbl[2 * d + 1:2 * d + 2, :]
        mx = jnp.maximum(l0, l1)
        e0 = jnp.exp(l0 - mx)
        e1 = jnp.exp(l1 - mx)
        lbs.append(e0 / (e0 + e1))

    st_scr[...] = jnp.zeros_like(st_scr)

    row = lax.broadcasted_iota(jnp.int32, (t, t), 0)
    col = lax.broadcasted_iota(jnp.int32, (t, t), 1)
    masks = (col <= row, col >= row)

    def chain(c, h, d):
        rows = pl.ds(pl.multiple_of(c * t, t), t)
        hs = slice(h * A_DIM, (h + 1) * A_DIM)
        f_ref = ffw_ref if d == 0 else fbw_ref
        x = f_ref[rows, hs].astype(F32)
        q = q_ref[rows, hs].astype(F32)
        v = v_ref[rows, hs]
        lb = lbs[d][:, hs]
        r = _sigmoid(x)
        lf = jnp.log(lb + (1.0 - lb) * r)
        k = (1.0 - lb) * (1.0 - r)
        b = _chunk_cumsum(lf, reverse=(d == 1))
        if d == 0:
            b_end = b[t - 1:t, :]
            b_mid = b[half - 1:half, :]
        else:
            b_end = b[0:1, :]
            b_mid = b[half:half + 1, :]
        qd = (q * jnp.exp(b)).astype(BF16)
        ke = (k * jnp.exp(b_end - b)).astype(BF16)
        qm = (q * jnp.exp(b - b_mid)).astype(BF16)
        km = (k * jnp.exp(b_mid - b)).astype(BF16)
        att = jnp.where(masks[d], _dot_nt(qm, km), 0.0).astype(BF16)
        idx = 2 * h + d
        s_t = st_scr[idx]
        o = _dot(att, v) + _dot_nt(qd, s_t.astype(BF16))
        st_scr[idx] = s_t * jnp.exp(b_end) + _dot_tn(v, ke)
        o_scr = ofw_scr if d == 0 else obw_scr
        o_scr[rows, hs] = o

    def body(c, carry):
        for h in range(heads_per_step):
            chain(c, h, 0)
            chain(n_chunks - 1 - c, h, 1)
        return carry

    lax.fori_loop(0, n_chunks, body, 0, unroll=2)

    fin_rows = 256

    def fin(i, carry):
        rows = pl.ds(pl.multiple_of(i * fin_rows, fin_rows), fin_rows)
        for h in range(heads_per_step):
            hs = slice(h * A_DIM, (h + 1) * A_DIM)
            y = _rms(ofw_scr[rows, hs] + obw_scr[rows, hs]) * gn_ref[...]
            o_ref[rows, hs] = (y * og_ref[rows, hs].astype(F32)).astype(BF16)
        return carry

    lax.fori_loop(0, seq_len // fin_rows, fin, 0)


def _hgrn(p_lin, p_og, lb_logits4, g_norm, batch, seq_len):
    hps = 2
    wblk = hps * A_DIM
    nblk = A_WIDTH // wblk

    def seg_spec(seg):
        return pl.BlockSpec((seq_len, wblk), lambda b, p: (b, seg * nblk + p))

    est = 2 * 6 * seq_len * wblk * 2 + 2 * seq_len * wblk * 4 + (8 << 20)
    return pl.pallas_call(
        functools.partial(_hgrn_kernel, heads_per_step=hps, seq_len=seq_len),
        grid=(batch, nblk),
        in_specs=[
            seg_spec(0), seg_spec(1), seg_spec(2), seg_spec(3), seg_spec(0),
            pl.BlockSpec((4, wblk), lambda b, p: (0, p)),
            pl.BlockSpec((1, A_DIM), lambda b, p: (0, 0)),
        ],
        out_specs=pl.BlockSpec((seq_len, wblk), lambda b, p: (b, p)),
        out_shape=jax.ShapeDtypeStruct((batch * seq_len, A_WIDTH), BF16),
        scratch_shapes=[
            pltpu.VMEM((seq_len, wblk), F32),
            pltpu.VMEM((seq_len, wblk), F32),
            pltpu.VMEM((2 * hps, A_DIM, A_DIM), F32),
        ],
        compiler_params=pltpu.CompilerParams(
            dimension_semantics=("arbitrary", "arbitrary"),
            vmem_limit_bytes=_vmem_limit(est)),
        name="hgrn",
    )(p_lin, p_lin, p_lin, p_lin, p_og, lb_logits4, g_norm.reshape(1, A_DIM))


def _sgu_kernel(u_ref, v_ref, g_ref, ws_ref, bs_ref, o_ref, *, chunks_per_step):
    v = v_ref[...].astype(F32)
    mu = jnp.mean(v, axis=-1, keepdims=True)
    dlt = v - mu
    y = (dlt * lax.rsqrt(jnp.mean(dlt * dlt, axis=-1, keepdims=True) + EPS) * g_ref[...]).astype(BF16)
    for n in range(chunks_per_step):
        rs = slice(n * B_CHUNK, (n + 1) * B_CHUNK)
        for g in range(B_GROUPS):
            cs = slice(g * 128, (g + 1) * 128)
            vm = _dot(ws_ref[g].astype(BF16), y[rs, cs]) + bs_ref[:, cs]
            o_ref[rs, cs] = (u_ref[rs, cs].astype(F32) * vm).astype(BF16)


def _sgu(p_z, g_v, w_s, bias_full):
    m = p_z.shape[0]
    cps = 4
    tm = cps * B_CHUNK
    return pl.pallas_call(
        functools.partial(_sgu_kernel, chunks_per_step=cps),
        grid=(m // tm,),
        in_specs=[
            pl.BlockSpec((tm, B_WIDTH), lambda i: (i, 0)),
            pl.BlockSpec((tm, B_WIDTH), lambda i: (i, 1)),
            pl.BlockSpec((1, B_WIDTH), lambda i: (0, 0)),
            pl.BlockSpec((B_GROUPS, B_CHUNK, B_CHUNK), lambda i: (0, 0, 0)),
            pl.BlockSpec((B_CHUNK, B_WIDTH), lambda i: (0, 0)),
        ],
        out_specs=pl.BlockSpec((tm, B_WIDTH), lambda i: (i, 0)),
        out_shape=jax.ShapeDtypeStruct((m, B_WIDTH), BF16),
        compiler_params=pltpu.CompilerParams(
            dimension_semantics=("arbitrary",),
            vmem_limit_bytes=_vmem_limit(32 << 20)),
        name="sgu",
    )(p_z, p_z, g_v.reshape(1, B_WIDTH), w_s, bias_full)


def _mix_kernel(hg_ref, sg_ref, ga_ref, gb_ref, x_ref, mod_ref, gpost_ref, gnext_ref,
                wa_ref, wb_ref, wo_ref, h_ref, a_ref, *, sub):
    gt1 = mod_ref[0, 2:3, :]
    sh2 = mod_ref[0, 3:4, :]
    sc2 = mod_ref[0, 4:5, :]
    for s in range(hg_ref.shape[0] // sub):
        rs = slice(s * sub, (s + 1) * sub)
        ya = _dot(hg_ref[rs, :], wa_ref[...])
        yb = _dot(sg_ref[rs, :], wb_ref[...])
        merged = ga_ref[rs, :].astype(F32) * ya + gb_ref[rs, :].astype(F32) * yb
        mo = _dot(merged.astype(BF16), wo_ref[...])
        h = x_ref[rs, :] + gt1 * (_rms(mo) * gpost_ref[...])
        h_ref[rs, :] = h
        a_ref[rs, :] = (_rms(h) * gnext_ref[...] * (1.0 + sc2) + sh2).astype(BF16)


def _mix(hg, sg, p_gate, x2, mod3, g_post, g_next, wa, wb, wo, rows_per_batch):
    m, d = x2.shape
    tm, sub = 512, 256
    bpb = rows_per_batch // tm
    const = lambda shape: pl.BlockSpec(shape, lambda i: (0, 0), pipeline_mode=pl.Buffered(1))
    row = lambda width, col: pl.BlockSpec((tm, width), lambda i: (i, col))
    vec_spec = pl.BlockSpec((1, d), lambda i: (0, 0))
    est = (2 * A_WIDTH * d + d * d) * 2 + 2 * tm * (2 * A_WIDTH * 2 + 2 * d * 2 + d * 4 + d * 4 + d * 2) \
        + 6 * tm * d * 4 + (4 << 20)
    return pl.pallas_call(
        functools.partial(_mix_kernel, sub=sub),
        grid=(m // tm,),
        in_specs=[
            row(A_WIDTH, 0), row(B_WIDTH, 0), row(d, 0), row(d, 1), row(d, 0),
            pl.BlockSpec((1, N_MOD, d), lambda i: (i // bpb, 0, 0)),
            vec_spec, vec_spec,
            const((A_WIDTH, d)), const((B_WIDTH, d)), const((d, d)),
        ],
        out_specs=[row(d, 0), row(d, 0)],
        out_shape=[jax.ShapeDtypeStruct((m, d), F32), jax.ShapeDtypeStruct((m, d), BF16)],
        compiler_params=pltpu.CompilerParams(
            dimension_semantics=("arbitrary",),
            vmem_limit_bytes=_vmem_limit(est)),
        name="mix",
    )(hg, sg, p_gate, p_gate, x2, mod3, g_post.reshape(1, d), g_next.reshape(1, d), wa, wb, wo)


def _resproj_kernel(lhs_ref, w_ref, res_ref, mod_ref, gpost_ref, h_ref, *, gate_row):
    k = pl.program_id(1)

    @pl.when(k == 0)
    def _():
        h_ref[...] = jnp.zeros_like(h_ref)

    h_ref[...] += _dot(lhs_ref[...], w_ref[...].astype(BF16))

    @pl.when(k == pl.num_programs(1) - 1)
    def _():
        gate = mod_ref[0, gate_row:gate_row + 1, :]
        h_ref[...] = res_ref[...] + gate * (_rms(h_ref[...]) * gpost_ref[...])


def _resproj(lhs, w, res, mod3, g_post, gate_row, rows_per_batch, tm, tk, name):
    m, kdim = lhs.shape
    d = w.shape[1]
    bpb = rows_per_batch // tm
    row_spec = pl.BlockSpec((tm, d), lambda i, k: (i, 0))
    est = 2 * tm * tk * 2 + 2 * tk * d * 4 + tk * d * 2 + 4 * tm * d * 4 + 2 * tm * d * 4 + (4 << 20)
    return pl.pallas_call(
        functools.partial(_resproj_kernel, gate_row=gate_row),
        grid=(m // tm, kdim // tk),
        in_specs=[
            pl.BlockSpec((tm, tk), lambda i, k: (i, k)),
            pl.BlockSpec((tk, d), lambda i, k: (k, 0)),
            row_spec,
            pl.BlockSpec((1, N_MOD, d), lambda i, k: (i // bpb, 0, 0)),
            pl.BlockSpec((1, d), lambda i, k: (0, 0)),
        ],
        out_specs=row_spec,
        out_shape=jax.ShapeDtypeStruct((m, d), F32),
        compiler_params=pltpu.CompilerParams(
            dimension_semantics=("arbitrary", "arbitrary"),
            vmem_limit_bytes=_vmem_limit(est)),
        name=name,
    )(lhs, w, res, mod3, g_post.reshape(1, d))


def kernel(x, c, w_ada, b_ada, g_pre_mix, g_post_mix, g_pre_ffn, g_post_ffn, w_in, lb_logits,
           g_hgrn_norm, w_a_out, g_sgu_norm, w_spatial, b_spatial, w_b_out, w_o, w_ff1, w_ff2):
    batch, seq_len, d = x.shape
    depth = w_in.shape[0]
    assert depth == 1 and lb_logits.shape == (2, depth + 1, A_WIDTH)
    assert w_in.shape[2] == 11 * SEG and d == 2 * SEG
    assert g_hgrn_norm.shape == (depth, A_DIM) and w_spatial.shape == (depth, B_GROUPS, B_CHUNK, B_CHUNK)
    assert seq_len % 1024 == 0
    m = batch * seq_len

    c8 = jnp.zeros((8, d), F32).at[:batch].set(c)
    mod = _modulation(c8, w_ada[0], b_ada[0])
    mod3 = mod[:batch].reshape(batch, N_MOD, d)

    x2 = x.reshape(m, d)
    a1 = _norm_mod(x2, mod3, g_pre_mix[0], seq_len)
    w_in0 = w_in[0]
    p_lin = _proj(a1, w_in0, SEG_LIN * SEG, 4 * SEG, _act_linear, "inproj_lin")
    p_og = _proj(a1, w_in0, SEG_OG * SEG, SEG, _act_silu, "inproj_silu")
    p_z = _proj(a1, w_in0, SEG_Z * SEG, 2 * SEG, _act_gelu, "inproj_gelu")
    p_gate = _proj(a1, w_in0, SEG_GATE * SEG, 4 * SEG, _act_sigmoid, "inproj_sig")

    hg = _hgrn(p_lin, p_og, lb_logits.reshape(2 * (depth + 1), A_WIDTH), g_hgrn_norm[0], batch, seq_len)

    bias_full = jnp.repeat(b_spatial[0].T, 128, axis=1)
    sg = _sgu(p_z, g_sgu_norm[0], w_spatial[0], bias_full)

    h1, a2 = _mix(hg, sg, p_gate, x2, mod3, g_post_mix[0], g_pre_ffn[0],
                  w_a_out[0].astype(BF16), w_b_out[0].astype(BF16), w_o[0].astype(BF16), seq_len)
    hid = _proj(a2, w_ff1[0], 0, w_ff1.shape[2], _act_relu2, "ff1")
    out = _resproj(hid, w_ff2[0], h1, mod3, g_post_ffn[0], 5, seq_len, tm=1024, tk=512, name="ff2")
    return out.reshape(batch, seq_len, d)
```

```python
import functools

import jax
import jax.numpy as jnp
from jax import lax
from jax.experimental import pallas as pl
from jax.experimental.pallas import tpu as pltpu

F32 = jnp.float32
BF16 = jnp.bfloat16
EPS = 1e-6
LOG2E = 1.4426950408889634

A_HEADS = 8
A_DIM = 128
A_WIDTH = A_HEADS * A_DIM
B_GROUPS = 8
B_CHUNK = 128
B_WIDTH = B_GROUPS * 128
N_MOD = 6
SEG = 1024
SEG_Q, SEG_F, SEG_V, SEG_OG, SEG_Z, SEG_GATE = 0, 1, 3, 4, 5, 7

HGRN_CHUNK = 64
HGRN_BLOCK = 16
PROJ_SUB = 256
V7X_VMEM_BYTES = 64 * 1024 * 1024


def _vmem_limit(estimate_bytes):
    return int(min(estimate_bytes, V7X_VMEM_BYTES - 4 * 1024 * 1024))


def _sigmoid(x):
    return 1.0 / (1.0 + jnp.exp(-x))


def _dot(a, b):
    return jnp.dot(a, b, preferred_element_type=F32)


def _dot_nt(a, b):
    return lax.dot_general(a, b, (((1,), (1,)), ((), ())), preferred_element_type=F32)


def _dot_tn(a, b):
    return lax.dot_general(a, b, (((0,), (0,)), ((), ())), preferred_element_type=F32)


def _rms(x):
    return x * lax.rsqrt(jnp.mean(x * x, axis=-1, keepdims=True) + EPS)


def _mod_kernel(c_ref, w_ref, b_ref, o_ref):
    c = c_ref[...]
    s = c * _sigmoid(c)
    o_ref[...] = _dot(s.astype(BF16), w_ref[...].astype(BF16)) + b_ref[...]


def _modulation(c8, w_ada, b_ada):
    d, n = w_ada.shape
    tn = 1024
    return pl.pallas_call(
        _mod_kernel,
        grid=(n // tn,),
        in_specs=[
            pl.BlockSpec((8, d), lambda j: (0, 0)),
            pl.BlockSpec((d, tn), lambda j: (0, j)),
            pl.BlockSpec((1, tn), lambda j: (0, j)),
        ],
        out_specs=pl.BlockSpec((8, tn), lambda j: (0, j)),
        out_shape=jax.ShapeDtypeStruct((8, n), F32),
        compiler_params=pltpu.CompilerParams(
            dimension_semantics=("arbitrary",),
            vmem_limit_bytes=_vmem_limit(2 * d * tn * 4 + d * tn * 2 + (8 << 20))),
        name="mod",
    )(c8, w_ada, b_ada.reshape(1, n))


def _norm_kernel(x_ref, mod_ref, g_ref, o_ref):
    y = _rms(x_ref[...]) * g_ref[...]
    o_ref[...] = (y * (1.0 + mod_ref[0, 1:2, :]) + mod_ref[0, 0:1, :]).astype(BF16)


def _norm_mod(x2, mod3, g, rows_per_batch):
    m, d = x2.shape
    tm = 512
    bpb = rows_per_batch // tm
    return pl.pallas_call(
        _norm_kernel,
        grid=(m // tm,),
        in_specs=[
            pl.BlockSpec((tm, d), lambda i: (i, 0)),
            pl.BlockSpec((1, N_MOD, d), lambda i: (i // bpb, 0, 0)),
            pl.BlockSpec((1, d), lambda i: (0, 0)),
        ],
        out_specs=pl.BlockSpec((tm, d), lambda i: (i, 0)),
        out_shape=jax.ShapeDtypeStruct((m, d), BF16),
        compiler_params=pltpu.CompilerParams(
            dimension_semantics=("arbitrary",),
            vmem_limit_bytes=_vmem_limit(2 * tm * d * 6 + 3 * tm * d * 4 + (4 << 20))),
        name="norm1",
    )(x2, mod3, g.reshape(1, d))


def _act_linear(acc, j):
    return acc * jnp.where(j == 0, A_DIM ** -0.5, 1.0)


def _act_silu(acc, j):
    return acc * _sigmoid(acc)


def _act_gelu(acc, j):
    return 0.5 * acc * (1.0 + lax.erf(acc * (2.0 ** -0.5)))


def _act_sigmoid(acc, j):
    return _sigmoid(acc)


def _act_relu2(acc, j):
    r = jnp.maximum(acc, 0.0)
    return r * r


def _proj_kernel(a_ref, w_ref, o_ref, w_scr, *, act):
    j = pl.program_id(0)

    @pl.when(pl.program_id(1) == 0)
    def _():
        w_scr[...] = w_ref[...].astype(BF16)

    for s in range(a_ref.shape[0] // PROJ_SUB):
        rs = slice(s * PROJ_SUB, (s + 1) * PROJ_SUB)
        o_ref[rs, :] = act(_dot(a_ref[rs, :], w_scr[...]), j).astype(BF16)


def _proj(a, w, seg0, nseg, act, name, seg_stride=1, tm=1024):
    m, kdim = a.shape
    tn = SEG
    est = 2 * tm * kdim * 2 + 2 * kdim * tn * 4 + kdim * tn * 2 + 2 * tm * tn * 2 + 3 * tm * tn * 4 + (4 << 20)
    return pl.pallas_call(
        functools.partial(_proj_kernel, act=act),
        grid=(nseg, m // tm),
        in_specs=[
            pl.BlockSpec((tm, kdim), lambda j, i: (i, 0)),
            pl.BlockSpec((kdim, tn), lambda j, i: (0, seg0 + j * seg_stride)),
        ],
        out_specs=pl.BlockSpec((tm, tn), lambda j, i: (i, j)),
        out_shape=jax.ShapeDtypeStruct((m, nseg * tn), BF16),
        scratch_shapes=[pltpu.VMEM((kdim, tn), BF16)],
        compiler_params=pltpu.CompilerParams(
            dimension_semantics=("arbitrary", "arbitrary"),
            vmem_limit_bytes=_vmem_limit(est)),
        name=name,
    )(a, w)


def _hgrn_kernel(q_ref, ffw_ref, fbw_ref, v_ref, og_ref, lbl_ref, gn_ref,
                 o_ref, ofw_scr, obw_scr, st_scr, *, heads_per_step, seq_len):
    t = HGRN_CHUNK
    n_chunks = seq_len // t
    half = t // 2
    f_refs = (ffw_ref, fbw_ref)

    lbl = lbl_ref[...]
    c0s, c1s = [], []
    for d in range(2):
        l0 = lbl[2 * d:2 * d + 1, :]
        l1 = lbl[2 * d + 1:2 * d + 2, :]
        mx = jnp.maximum(l0, l1)
        e0 = jnp.exp(l0 - mx)
        lb = e0 / (e0 + jnp.exp(l1 - mx))
        c0s.append(0.5 * (1.0 + lb))
        c1s.append(0.5 * (1.0 - lb))

    st_scr[...] = jnp.zeros_like(st_scr)

    row = lax.broadcasted_iota(jnp.int32, (t, t), 0)
    col = lax.broadcasted_iota(jnp.int32, (t, t), 1)
    masks = (col <= row, col >= row)
    row2 = lax.broadcasted_iota(jnp.int32, (t, 2 * t), 0)
    col2 = lax.broadcasted_iota(jnp.int32, (t, 2 * t), 1) & (t - 1)
    tris = (jnp.where(col2 <= row2, 1.0, 0.0).astype(BF16), jnp.where(col2 >= row2, 1.0, 0.0).astype(BF16))

    def stage0(c, h, d):
        rows = pl.ds(pl.multiple_of(c * t, t), t)
        hs = slice(h * A_DIM, (h + 1) * A_DIM)
        ct = c1s[d][:, hs] * jnp.tanh(0.5 * f_refs[d][rows, hs].astype(F32))
        k = c1s[d][:, hs] - ct
        lf = jnp.log(c0s[d][:, hs] + ct)
        hi = lf.astype(BF16)
        lo = (lf - hi.astype(F32)).astype(BF16)
        return h, d, rows, hs, k, _dot(tris[d], jnp.concatenate([hi, lo], axis=0))

    def stage1(h, d, rows, hs, k, b):
        b = b * LOG2E
        q = q_ref[rows, hs].astype(F32)
        if d == 0:
            b_end = b[t - 1:t, :]
            b_mid = b[half - 1:half, :]
        else:
            b_end = b[0:1, :]
            b_mid = b[half:half + 1, :]
        qd = (q * jnp.exp2(b)).astype(BF16)
        ke = (k * jnp.exp2(b_end - b)).astype(BF16)
        qm = (q * jnp.exp2(b - b_mid)).astype(BF16)
        km = (k * jnp.exp2(b_mid - b)).astype(BF16)
        return h, d, rows, hs, _dot_nt(qm, km), qd, ke, jnp.exp2(b_end)

    def stage2(h, d, rows, hs, att, qd, ke, decay):
        v = v_ref[rows, hs]
        att = jnp.where(masks[d], att, 0.0).astype(BF16)
        idx = 2 * h + d
        s_t = st_scr[idx]
        o = _dot(att, v) + _dot_nt(qd, s_t.astype(BF16))
        st_scr[idx] = s_t * decay + _dot_tn(v, ke)
        o_scr = ofw_scr if d == 0 else obw_scr
        o_scr[rows, hs] = o

    units = [(h, d) for h in range(heads_per_step) for d in range(2)]

    def block(cb, carry):
        after0, after1 = {}, {}
        for step in range(HGRN_BLOCK + 2):
            if step < HGRN_BLOCK:
                c = cb * HGRN_BLOCK + step
                after0[step] = [stage0(c if d == 0 else n_chunks - 1 - c, h, d) for h, d in units]
            if 1 <= step <= HGRN_BLOCK:
                after1[step - 1] = [stage1(*vals) for vals in after0.pop(step - 1)]
            if step >= 2:
                for vals in after1.pop(step - 2):
                    stage2(*vals)
        return carry

    lax.fori_loop(0, n_chunks // HGRN_BLOCK, block, 0)

    fin_rows = 256

    def fin(i, carry):
        rows = pl.ds(pl.multiple_of(i * fin_rows, fin_rows), fin_rows)
        for h in range(heads_per_step):
            hs = slice(h * A_DIM, (h + 1) * A_DIM)
            y = _rms(ofw_scr[rows, hs] + obw_scr[rows, hs]) * gn_ref[...]
            o_ref[rows, hs] = (y * og_ref[rows, hs].astype(F32)).astype(BF16)
        return carry

    lax.fori_loop(0, seq_len // fin_rows, fin, 0)


def _hgrn(p_lin, p_og, lb_logits4, g_norm, batch, seq_len):
    hps = 2
    wblk = hps * A_DIM
    nblk = A_WIDTH // wblk

    def seg_spec(seg):
        return pl.BlockSpec((seq_len, wblk), lambda b, p: (b, seg * nblk + p))

    est = 2 * 6 * seq_len * wblk * 2 + 2 * seq_len * wblk * 4 + (6 << 20)
    return pl.pallas_call(
        functools.partial(_hgrn_kernel, heads_per_step=hps, seq_len=seq_len),
        grid=(batch, nblk),
        in_specs=[
            seg_spec(0), seg_spec(1), seg_spec(2), seg_spec(3),
            seg_spec(0),
            pl.BlockSpec((4, wblk), lambda b, p: (0, p)),
            pl.BlockSpec((1, A_DIM), lambda b, p: (0, 0)),
        ],
        out_specs=pl.BlockSpec((seq_len, wblk), lambda b, p: (b, p)),
        out_shape=jax.ShapeDtypeStruct((batch * seq_len, A_WIDTH), BF16),
        scratch_shapes=[
            pltpu.VMEM((seq_len, wblk), F32),
            pltpu.VMEM((seq_len, wblk), F32),
            pltpu.VMEM((2 * hps, A_DIM, A_DIM), F32),
        ],
        compiler_params=pltpu.CompilerParams(
            dimension_semantics=("arbitrary", "arbitrary"),
            vmem_limit_bytes=_vmem_limit(est)),
        name="hgrn",
    )(p_lin, p_lin, p_lin, p_lin, p_og, lb_logits4, g_norm.reshape(1, A_DIM))


def _sgu_kernel(u_ref, v_ref, g_ref, ws_ref, bs_ref, o_ref, *, chunks_per_step):
    v = v_ref[...].astype(F32)
    mu = jnp.mean(v, axis=-1, keepdims=True)
    dlt = v - mu
    y = (dlt * lax.rsqrt(jnp.mean(dlt * dlt, axis=-1, keepdims=True) + EPS) * g_ref[...]).astype(BF16)
    for n in range(chunks_per_step):
        rs = slice(n * B_CHUNK, (n + 1) * B_CHUNK)
        for g in range(B_GROUPS):
            cs = slice(g * 128, (g + 1) * 128)
            vm = _dot(ws_ref[g].astype(BF16), y[rs, cs]) + bs_ref[:, cs]
            o_ref[rs, cs] = (u_ref[rs, cs].astype(F32) * vm).astype(BF16)


def _sgu(p_z, g_v, w_s, bias_full):
    m = p_z.shape[0]
    cps = 4
    tm = cps * B_CHUNK
    return pl.pallas_call(
        functools.partial(_sgu_kernel, chunks_per_step=cps),
        grid=(m // tm,),
        in_specs=[
            pl.BlockSpec((tm, B_WIDTH), lambda i: (i, 0)),
            pl.BlockSpec((tm, B_WIDTH), lambda i: (i, 1)),
            pl.BlockSpec((1, B_WIDTH), lambda i: (0, 0)),
            pl.BlockSpec((B_GROUPS, B_CHUNK, B_CHUNK), lambda i: (0, 0, 0)),
            pl.BlockSpec((B_CHUNK, B_WIDTH), lambda i: (0, 0)),
        ],
        out_specs=pl.BlockSpec((tm, B_WIDTH), lambda i: (i, 0)),
        out_shape=jax.ShapeDtypeStruct((m, B_WIDTH), BF16),
        compiler_params=pltpu.CompilerParams(
            dimension_semantics=("arbitrary",),
            vmem_limit_bytes=_vmem_limit(32 << 20)),
        name="sgu",
    )(p_z, p_z, g_v.reshape(1, B_WIDTH), w_s, bias_full)


def _mix_kernel(hg_ref, sg_ref, ga_ref, gb_ref, x_ref, mod_ref, gpost_ref, gnext_ref,
                wa_ref, wb_ref, wo_ref, h_ref, a_ref, *, sub):
    gt1 = mod_ref[0, 2:3, :]
    sh2 = mod_ref[0, 3:4, :]
    sc2 = mod_ref[0, 4:5, :]
    for s in range(hg_ref.shape[0] // sub):
        rs = slice(s * sub, (s + 1) * sub)
        ya = _dot(hg_ref[rs, :], wa_ref[...])
        yb = _dot(sg_ref[rs, :], wb_ref[...])
        merged = ga_ref[rs, :].astype(F32) * ya + gb_ref[rs, :].astype(F32) * yb
        mo = _dot(merged.astype(BF16), wo_ref[...])
        h = x_ref[rs, :] + gt1 * (_rms(mo) * gpost_ref[...])
        h_ref[rs, :] = h
        a_ref[rs, :] = (_rms(h) * gnext_ref[...] * (1.0 + sc2) + sh2).astype(BF16)


def _mix(hg, sg, p_gate, x2, mod3, g_post, g_next, wa, wb, wo, rows_per_batch):
    m, d = x2.shape
    tm, sub = 512, 256
    bpb = rows_per_batch // tm
    const = lambda shape: pl.BlockSpec(shape, lambda i: (0, 0), pipeline_mode=pl.Buffered(1))
    row = lambda width, col: pl.BlockSpec((tm, width), lambda i: (i, col))
    vec_spec = pl.BlockSpec((1, d), lambda i: (0, 0))
    est = (2 * A_WIDTH * d + d * d) * 2 + 2 * tm * (2 * A_WIDTH * 2 + 2 * d * 2 + d * 4 + d * 4 + d * 2) \
        + 6 * tm * d * 4 + (4 << 20)
    return pl.pallas_call(
        functools.partial(_mix_kernel, sub=sub),
        grid=(m // tm,),
        in_specs=[
            row(A_WIDTH, 0), row(B_WIDTH, 0), row(d, 0), row(d, 1), row(d, 0),
            pl.BlockSpec((1, N_MOD, d), lambda i: (i // bpb, 0, 0)),
            vec_spec, vec_spec,
            const((A_WIDTH, d)), const((B_WIDTH, d)), const((d, d)),
        ],
        out_specs=[row(d, 0), row(d, 0)],
        out_shape=[jax.ShapeDtypeStruct((m, d), F32), jax.ShapeDtypeStruct((m, d), BF16)],
        compiler_params=pltpu.CompilerParams(
            dimension_semantics=("arbitrary",),
            vmem_limit_bytes=_vmem_limit(est)),
        name="mix",
    )(hg, sg, p_gate, p_gate, x2, mod3, g_post.reshape(1, d), g_next.reshape(1, d), wa, wb, wo)


def _resproj_kernel(lhs_ref, w_ref, res_ref, mod_ref, gpost_ref, h_ref, *, gate_row):
    k = pl.program_id(1)

    @pl.when(k == 0)
    def _():
        h_ref[...] = jnp.zeros_like(h_ref)

    h_ref[...] += _dot(lhs_ref[...], w_ref[...].astype(BF16))

    @pl.when(k == pl.num_programs(1) - 1)
    def _():
        gate = mod_ref[0, gate_row:gate_row + 1, :]
        h_ref[...] = res_ref[...] + gate * (_rms(h_ref[...]) * gpost_ref[...])


def _resproj(lhs, w, res, mod3, g_post, gate_row, rows_per_batch, tm, tk, name):
    m, kdim = lhs.shape
    d = w.shape[1]
    bpb = rows_per_batch // tm
    row_spec = pl.BlockSpec((tm, d), lambda i, k: (i, 0))
    est = 2 * tm * tk * 2 + 2 * tk * d * 4 + tk * d * 2 + 4 * tm * d * 4 + 2 * tm * d * 4 + (4 << 20)
    return pl.pallas_call(
        functools.partial(_resproj_kernel, gate_row=gate_row),
        grid=(m // tm, kdim // tk),
        in_specs=[
            pl.BlockSpec((tm, tk), lambda i, k: (i, k)),
            pl.BlockSpec((tk, d), lambda i, k: (k, 0)),
            row_spec,
            pl.BlockSpec((1, N_MOD, d), lambda i, k: (i // bpb, 0, 0)),
            pl.BlockSpec((1, d), lambda i, k: (0, 0)),
        ],
        out_specs=row_spec,
        out_shape=jax.ShapeDtypeStruct((m, d), F32),
        compiler_params=pltpu.CompilerParams(
            dimension_semantics=("arbitrary", "arbitrary"),
            vmem_limit_bytes=_vmem_limit(est)),
        name=name,
    )(lhs, w, res, mod3, g_post.reshape(1, d))


def kernel(x, c, w_ada, b_ada, g_pre_mix, g_post_mix, g_pre_ffn, g_post_ffn, w_in, lb_logits,
           g_hgrn_norm, w_a_out, g_sgu_norm, w_spatial, b_spatial, w_b_out, w_o, w_ff1, w_ff2):
    batch, seq_len, d = x.shape
    depth = w_in.shape[0]
    assert depth == 1 and lb_logits.shape == (2, depth + 1, A_WIDTH)
    assert w_in.shape[2] == 11 * SEG and d == 2 * SEG
    assert g_hgrn_norm.shape == (depth, A_DIM) and w_spatial.shape == (depth, B_GROUPS, B_CHUNK, B_CHUNK)
    assert seq_len % 1024 == 0
    m = batch * seq_len

    c8 = jnp.zeros((8, d), F32).at[:batch].set(c)
    mod = _modulation(c8, w_ada[0], b_ada[0])
    mod3 = mod[:batch].reshape(batch, N_MOD, d)

    x2 = x.reshape(m, d)
    a1 = _norm_mod(x2, mod3, g_pre_mix[0], seq_len)
    w_in0 = w_in[0]
    p_lin = _proj(a1, w_in0, SEG_Q, 4, _act_linear, "inproj_lin")
    p_og = _proj(a1, w_in0, SEG_OG, 1, _act_silu, "inproj_silu")
    p_z = _proj(a1, w_in0, SEG_Z, 2, _act_gelu, "inproj_gelu")
    p_gate = _proj(a1, w_in0, SEG_GATE, 4, _act_sigmoid, "inproj_sig")

    hg = _hgrn(p_lin, p_og, lb_logits.reshape(2 * (depth + 1), A_WIDTH), g_hgrn_norm[0], batch, seq_len)

    bias_full = jnp.repeat(b_spatial[0].T, 128, axis=1)
    sg = _sgu(p_z, g_sgu_norm[0], w_spatial[0], bias_full)

    h1, a2 = _mix(hg, sg, p_gate, x2, mod3, g_post_mix[0], g_pre_ffn[0],
                  w_a_out[0].astype(BF16), w_b_out[0].astype(BF16), w_o[0].astype(BF16), seq_len)
    hid = _proj(a2, w_ff1[0], 0, w_ff1.shape[2] // SEG, _act_relu2, "ff1")
    out = _resproj(hid, w_ff2[0], h1, mod3, g_post_ffn[0], 5, seq_len, tm=1024, tk=512, name="ff2")
    return out.reshape(batch, seq_len, d)
```

```python
import functools

import jax
import jax.numpy as jnp
from jax import lax
from jax.experimental import pallas as pl
from jax.experimental.pallas import tpu as pltpu

F32 = jnp.float32
BF16 = jnp.bfloat16
EPS = 1e-6
LOG2E = 1.4426950408889634

A_HEADS = 8
A_DIM = 128
A_WIDTH = A_HEADS * A_DIM
B_GROUPS = 8
B_CHUNK = 128
B_WIDTH = B_GROUPS * 128
N_MOD = 6
SEG = 1024
SEG_Q, SEG_F, SEG_V, SEG_OG, SEG_Z, SEG_GATE = 0, 1, 3, 4, 5, 7

HGRN_CHUNK = 64
HGRN_BLOCK = 16
PROJ_SUB = 128
V7X_VMEM_BYTES = 64 * 1024 * 1024


def _vmem_limit(estimate_bytes):
    return int(min(estimate_bytes, V7X_VMEM_BYTES - 4 * 1024 * 1024))


def _sigmoid(x):
    return 1.0 / (1.0 + jnp.exp(-x))


def _dot(a, b):
    return jnp.dot(a, b, preferred_element_type=F32)


def _dot_nt(a, b):
    return lax.dot_general(a, b, (((1,), (1,)), ((), ())), preferred_element_type=F32)


def _dot_tn(a, b):
    return lax.dot_general(a, b, (((0,), (0,)), ((), ())), preferred_element_type=F32)


def _rms(x):
    return x * lax.rsqrt(jnp.mean(x * x, axis=-1, keepdims=True) + EPS)


def _mod_kernel(c_ref, w_ref, b_ref, o_ref):
    c = c_ref[...]
    s = c * _sigmoid(c)
    o_ref[...] = _dot(s.astype(BF16), w_ref[...].astype(BF16)) + b_ref[...]


def _modulation(c8, w_ada, b_ada):
    d, n = w_ada.shape
    tn = 1024
    return pl.pallas_call(
        _mod_kernel,
        grid=(n // tn,),
        in_specs=[
            pl.BlockSpec((8, d), lambda j: (0, 0)),
            pl.BlockSpec((d, tn), lambda j: (0, j)),
            pl.BlockSpec((1, tn), lambda j: (0, j)),
        ],
        out_specs=pl.BlockSpec((8, tn), lambda j: (0, j)),
        out_shape=jax.ShapeDtypeStruct((8, n), F32),
        compiler_params=pltpu.CompilerParams(
            dimension_semantics=("arbitrary",),
            vmem_limit_bytes=_vmem_limit(2 * d * tn * 4 + d * tn * 2 + (8 << 20))),
        name="mod",
    )(c8, w_ada, b_ada.reshape(1, n))


def _norm_kernel(x_ref, mod_ref, g_ref, o_ref):
    y = _rms(x_ref[...]) * g_ref[...]
    o_ref[...] = (y * (1.0 + mod_ref[0, 1:2, :]) + mod_ref[0, 0:1, :]).astype(BF16)


def _norm_mod(x2, mod3, g, rows_per_batch):
    m, d = x2.shape
    tm = 512
    bpb = rows_per_batch // tm
    return pl.pallas_call(
        _norm_kernel,
        grid=(m // tm,),
        in_specs=[
            pl.BlockSpec((tm, d), lambda i: (i, 0)),
            pl.BlockSpec((1, N_MOD, d), lambda i: (i // bpb, 0, 0)),
            pl.BlockSpec((1, d), lambda i: (0, 0)),
        ],
        out_specs=pl.BlockSpec((tm, d), lambda i: (i, 0)),
        out_shape=jax.ShapeDtypeStruct((m, d), BF16),
        compiler_params=pltpu.CompilerParams(
            dimension_semantics=("arbitrary",),
            vmem_limit_bytes=_vmem_limit(2 * tm * d * 6 + 3 * tm * d * 4 + (4 << 20))),
        name="norm1",
    )(x2, mod3, g.reshape(1, d))


def _act_linear(acc, j):
    return acc * jnp.where(j == 0, A_DIM ** -0.5, 1.0)


def _act_silu(acc, j):
    return acc * _sigmoid(acc)


def _act_gelu(acc, j):
    return 0.5 * acc * (1.0 + lax.erf(acc * (2.0 ** -0.5)))


def _act_sigmoid(acc, j):
    return _sigmoid(acc)


def _act_relu2(acc, j):
    r = jnp.maximum(acc, 0.0)
    return r * r


def _proj_kernel(a_ref, w_ref, o_ref, w_scr, *, act):
    j = pl.program_id(0)

    @pl.when(pl.program_id(1) == 0)
    def _():
        w_scr[...] = w_ref[...].astype(BF16)

    for s in range(a_ref.shape[0] // PROJ_SUB):
        rs = slice(s * PROJ_SUB, (s + 1) * PROJ_SUB)
        o_ref[rs, :] = act(_dot(a_ref[rs, :], w_scr[...]), j).astype(BF16)


def _proj(a, w, seg0, nseg, act, name, seg_stride=1, tm=2048):
    m, kdim = a.shape
    tn = SEG
    est = 2 * tm * kdim * 2 + 2 * kdim * tn * 4 + kdim * tn * 2 + 2 * tm * tn * 2 + 8 * PROJ_SUB * tn * 4 + (6 << 20)
    return pl.pallas_call(
        functools.partial(_proj_kernel, act=act),
        grid=(nseg, m // tm),
        in_specs=[
            pl.BlockSpec((tm, kdim), lambda j, i: (i, 0)),
            pl.BlockSpec((kdim, tn), lambda j, i: (0, seg0 + j * seg_stride)),
        ],
        out_specs=pl.BlockSpec((tm, tn), lambda j, i: (i, j)),
        out_shape=jax.ShapeDtypeStruct((m, nseg * tn), BF16),
        scratch_shapes=[pltpu.VMEM((kdim, tn), BF16)],
        compiler_params=pltpu.CompilerParams(
            dimension_semantics=("arbitrary", "arbitrary"),
            vmem_limit_bytes=_vmem_limit(est)),
        name=name,
    )(a, w)


def _hgrn_kernel(q_ref, ffw_ref, fbw_ref, v_ref, og_ref, lbl_ref, gn_ref,
                 o_ref, ofw_scr, obw_scr, st_scr, *, heads_per_step, seq_len):
    t = HGRN_CHUNK
    n_chunks = seq_len // t
    half = t // 2
    f_refs = (ffw_ref, fbw_ref)

    lbl = lbl_ref[...]
    c0s, c1s = [], []
    for d in range(2):
        l0 = lbl[2 * d:2 * d + 1, :]
        l1 = lbl[2 * d + 1:2 * d + 2, :]
        mx = jnp.maximum(l0, l1)
        e0 = jnp.exp(l0 - mx)
        lb = e0 / (e0 + jnp.exp(l1 - mx))
        c0s.append(0.5 * (1.0 + lb))
        c1s.append(0.5 * (1.0 - lb))

    st_scr[...] = jnp.zeros_like(st_scr)

    row = lax.broadcasted_iota(jnp.int32, (t, t), 0)
    col = lax.broadcasted_iota(jnp.int32, (t, t), 1)
    masks = (col <= row, col >= row)
    row2 = lax.broadcasted_iota(jnp.int32, (t, 2 * t), 0)
    col2 = lax.broadcasted_iota(jnp.int32, (t, 2 * t), 1) & (t - 1)
    tris = (jnp.where(col2 <= row2, 1.0, 0.0).astype(BF16), jnp.where(col2 >= row2, 1.0, 0.0).astype(BF16))

    def stage0(c, h, d):
        rows = pl.ds(pl.multiple_of(c * t, t), t)
        hs = slice(h * A_DIM, (h + 1) * A_DIM)
        ct = c1s[d][:, hs] * jnp.tanh(0.5 * f_refs[d][rows, hs].astype(F32))
        k = c1s[d][:, hs] - ct
        lf = jnp.log(c0s[d][:, hs] + ct)
        hi = lf.astype(BF16)
        lo = (lf - hi.astype(F32)).astype(BF16)
        return h, d, rows, hs, k, _dot(tris[d], jnp.concatenate([hi, lo], axis=0))

    def stage1(h, d, rows, hs, k, b):
        b = b * LOG2E
        q = q_ref[rows, hs].astype(F32)
        if d == 0:
            b_end = b[t - 1:t, :]
            b_mid = b[half - 1:half, :]
        else:
            b_end = b[0:1, :]
            b_mid = b[half:half + 1, :]
        qd = (q * jnp.exp2(b)).astype(BF16)
        ke = (k * jnp.exp2(b_end - b)).astype(BF16)
        qm = (q * jnp.exp2(b - b_mid)).astype(BF16)
        km = (k * jnp.exp2(b_mid - b)).astype(BF16)
        return h, d, rows, hs, _dot_nt(qm, km), qd, ke, jnp.exp2(b_end)

    def stage2(h, d, rows, hs, att, qd, ke, decay):
        v = v_ref[rows, hs]
        att = jnp.where(masks[d], att, 0.0).astype(BF16)
        idx = 2 * h + d
        s_t = st_scr[idx]
        o = _dot(att, v) + _dot_nt(qd, s_t.astype(BF16))
        st_scr[idx] = s_t * decay + _dot_tn(v, ke)
        o_scr = ofw_scr if d == 0 else obw_scr
        o_scr[rows, hs] = o

    units = [(h, d) for h in range(heads_per_step) for d in range(2)]

    def block(cb, carry):
        after0, after1 = {}, {}
        for step in range(HGRN_BLOCK + 2):
            if step < HGRN_BLOCK:
                c = cb * HGRN_BLOCK + step
                after0[step] = [stage0(c if d == 0 else n_chunks - 1 - c, h, d) for h, d in units]
            if 1 <= step <= HGRN_BLOCK:
                after1[step - 1] = [stage1(*vals) for vals in after0.pop(step - 1)]
            if step >= 2:
                for vals in after1.pop(step - 2):
                    stage2(*vals)
        return carry

    lax.fori_loop(0, n_chunks // HGRN_BLOCK, block, 0)

    fin_rows = 256

    def fin(i, carry):
        rows = pl.ds(pl.multiple_of(i * fin_rows, fin_rows), fin_rows)
        for h in range(heads_per_step):
            hs = slice(h * A_DIM, (h + 1) * A_DIM)
            y = _rms(ofw_scr[rows, hs] + obw_scr[rows, hs]) * gn_ref[...]
            o_ref[rows, hs] = (y * og_ref[rows, hs].astype(F32)).astype(BF16)
        return carry

    lax.fori_loop(0, seq_len // fin_rows, fin, 0)


def _hgrn(p_lin, p_og, lb_logits4, g_norm, batch, seq_len):
    hps = 2
    wblk = hps * A_DIM
    nblk = A_WIDTH // wblk

    def seg_spec(seg):
        return pl.BlockSpec((seq_len, wblk), lambda b, p: (b, seg * nblk + p))

    est = 2 * 6 * seq_len * wblk * 2 + 2 * seq_len * wblk * 4 + (6 << 20)
    return pl.pallas_call(
        functools.partial(_hgrn_kernel, heads_per_step=hps, seq_len=seq_len),
        grid=(batch, nblk),
        in_specs=[
            seg_spec(0), seg_spec(1), seg_spec(2), seg_spec(3),
            seg_spec(0),
            pl.BlockSpec((4, wblk), lambda b, p: (0, p)),
            pl.BlockSpec((1, A_DIM), lambda b, p: (0, 0)),
        ],
        out_specs=pl.BlockSpec((seq_len, wblk), lambda b, p: (b, p)),
        out_shape=jax.ShapeDtypeStruct((batch * seq_len, A_WIDTH), BF16),
        scratch_shapes=[
            pltpu.VMEM((seq_len, wblk), F32),
            pltpu.VMEM((seq_len, wblk), F32),
            pltpu.VMEM((2 * hps, A_DIM, A_DIM), F32),
        ],
        compiler_params=pltpu.CompilerParams(
            dimension_semantics=("arbitrary", "arbitrary"),
            vmem_limit_bytes=_vmem_limit(est)),
        name="hgrn",
    )(p_lin, p_lin, p_lin, p_lin, p_og, lb_logits4, g_norm.reshape(1, A_DIM))


def _sgu_kernel(u_ref, v_ref, g_ref, ws_ref, bs_ref, o_ref, *, chunks_per_step):
    v = v_ref[...].astype(F32)
    mu = jnp.mean(v, axis=-1, keepdims=True)
    dlt = v - mu
    y = (dlt * lax.rsqrt(jnp.mean(dlt * dlt, axis=-1, keepdims=True) + EPS) * g_ref[...]).astype(BF16)
    for n in range(chunks_per_step):
        rs = slice(n * B_CHUNK, (n + 1) * B_CHUNK)
        for g in range(B_GROUPS):
            cs = slice(g * 128, (g + 1) * 128)
            vm = _dot(ws_ref[g].astype(BF16), y[rs, cs]) + bs_ref[:, cs]
            o_ref[rs, cs] = (u_ref[rs, cs].astype(F32) * vm).astype(BF16)


def _sgu(p_z, g_v, w_s, bias_full):
    m = p_z.shape[0]
    cps = 4
    tm = cps * B_CHUNK
    return pl.pallas_call(
        functools.partial(_sgu_kernel, chunks_per_step=cps),
        grid=(m // tm,),
        in_specs=[
            pl.BlockSpec((tm, B_WIDTH), lambda i: (i, 0)),
            pl.BlockSpec((tm, B_WIDTH), lambda i: (i, 1)),
            pl.BlockSpec((1, B_WIDTH), lambda i: (0, 0)),
            pl.BlockSpec((B_GROUPS, B_CHUNK, B_CHUNK), lambda i: (0, 0, 0)),
            pl.BlockSpec((B_CHUNK, B_WIDTH), lambda i: (0, 0)),
        ],
        out_specs=pl.BlockSpec((tm, B_WIDTH), lambda i: (i, 0)),
        out_shape=jax.ShapeDtypeStruct((m, B_WIDTH), BF16),
        compiler_params=pltpu.CompilerParams(
            dimension_semantics=("arbitrary",),
            vmem_limit_bytes=_vmem_limit(32 << 20)),
        name="sgu",
    )(p_z, p_z, g_v.reshape(1, B_WIDTH), w_s, bias_full)


def _mix_kernel(hg_ref, sg_ref, ga_ref, gb_ref, x_ref, mod_ref, gpost_ref, gnext_ref,
                wa_ref, wb_ref, wo_ref, h_ref, a_ref, *, sub):
    gt1 = mod_ref[0, 2:3, :]
    sh2 = mod_ref[0, 3:4, :]
    sc2 = mod_ref[0, 4:5, :]
    for s in range(hg_ref.shape[0] // sub):
        rs = slice(s * sub, (s + 1) * sub)
        ya = _dot(hg_ref[rs, :], wa_ref[...])
        yb = _dot(sg_ref[rs, :], wb_ref[...])
        merged = ga_ref[rs, :].astype(F32) * ya + gb_ref[rs, :].astype(F32) * yb
        mo = _dot(merged.astype(BF16), wo_ref[...])
        h = x_ref[rs, :] + gt1 * (_rms(mo) * gpost_ref[...])
        h_ref[rs, :] = h
        a_ref[rs, :] = (_rms(h) * gnext_ref[...] * (1.0 + sc2) + sh2).astype(BF16)


def _mix(hg, sg, p_gate, x2, mod3, g_post, g_next, wa, wb, wo, rows_per_batch):
    m, d = x2.shape
    tm, sub = 512, 256
    bpb = rows_per_batch // tm
    const = lambda shape: pl.BlockSpec(shape, lambda i: (0, 0), pipeline_mode=pl.Buffered(1))
    row = lambda width, col: pl.BlockSpec((tm, width), lambda i: (i, col))
    vec_spec = pl.BlockSpec((1, d), lambda i: (0, 0))
    est = (2 * A_WIDTH * d + d * d) * 2 + 2 * tm * (2 * A_WIDTH * 2 + 2 * d * 2 + d * 4 + d * 4 + d * 2) \
        + 6 * tm * d * 4 + (4 << 20)
    return pl.pallas_call(
        functools.partial(_mix_kernel, sub=sub),
        grid=(m // tm,),
        in_specs=[
            row(A_WIDTH, 0), row(B_WIDTH, 0), row(d, 0), row(d, 1), row(d, 0),
            pl.BlockSpec((1, N_MOD, d), lambda i: (i // bpb, 0, 0)),
            vec_spec, vec_spec,
            const((A_WIDTH, d)), const((B_WIDTH, d)), const((d, d)),
        ],
        out_specs=[row(d, 0), row(d, 0)],
        out_shape=[jax.ShapeDtypeStruct((m, d), F32), jax.ShapeDtypeStruct((m, d), BF16)],
        compiler_params=pltpu.CompilerParams(
            dimension_semantics=("arbitrary",),
            vmem_limit_bytes=_vmem_limit(est)),
        name="mix",
    )(hg, sg, p_gate, p_gate, x2, mod3, g_post.reshape(1, d), g_next.reshape(1, d), wa, wb, wo)


def _resproj_kernel(lhs_ref, w_ref, res_ref, mod_ref, gpost_ref, h_ref, *, gate_row, sub):
    k = pl.program_id(1)
    last = pl.num_programs(1) - 1

    @pl.when(k == 0)
    def _():
        h_ref[...] = _dot(lhs_ref[...], w_ref[...].astype(BF16))

    @pl.when((k > 0) & (k < last))
    def _():
        h_ref[...] += _dot(lhs_ref[...], w_ref[...].astype(BF16))

    @pl.when(k == last)
    def _():
        w = w_ref[...].astype(BF16)
        gate = mod_ref[0, gate_row:gate_row + 1, :]
        for s in range(h_ref.shape[0] // sub):
            rs = slice(s * sub, (s + 1) * sub)
            acc = h_ref[rs, :] + _dot(lhs_ref[rs, :], w)
            h_ref[rs, :] = res_ref[rs, :] + gate * (_rms(acc) * gpost_ref[...])


def _resproj(lhs, w, res, mod3, g_post, gate_row, rows_per_batch, tm, tk, name):
    m, kdim = lhs.shape
    d = w.shape[1]
    bpb = rows_per_batch // tm
    row_spec = pl.BlockSpec((tm, d), lambda i, k: (i, 0))
    est = 2 * tm * tk * 2 + 2 * tk * d * 4 + tk * d * 2 + 4 * tm * d * 4 + 2 * tm * d * 4 + (4 << 20)
    return pl.pallas_call(
        functools.partial(_resproj_kernel, gate_row=gate_row, sub=256),
        grid=(m // tm, kdim // tk),
        in_specs=[
            pl.BlockSpec((tm, tk), lambda i, k: (i, k)),
            pl.BlockSpec((tk, d), lambda i, k: (k, 0)),
            row_spec,
            pl.BlockSpec((1, N_MOD, d), lambda i, k: (i // bpb, 0, 0)),
            pl.BlockSpec((1, d), lambda i, k: (0, 0)),
        ],
        out_specs=row_spec,
        out_shape=jax.ShapeDtypeStruct((m, d), F32),
        compiler_params=pltpu.CompilerParams(
            dimension_semantics=("arbitrary", "arbitrary"),
            vmem_limit_bytes=_vmem_limit(est)),
        name=name,
    )(lhs, w, res, mod3, g_post.reshape(1, d))


def kernel(x, c, w_ada, b_ada, g_pre_mix, g_post_mix, g_pre_ffn, g_post_ffn, w_in, lb_logits,
           g_hgrn_norm, w_a_out, g_sgu_norm, w_spatial, b_spatial, w_b_out, w_o, w_ff1, w_ff2):
    batch, seq_len, d = x.shape
    depth = w_in.shape[0]
    assert depth == 1 and lb_logits.shape == (2, depth + 1, A_WIDTH)
    assert w_in.shape[2] == 11 * SEG and d == 2 * SEG
    assert g_hgrn_norm.shape == (depth, A_DIM) and w_spatial.shape == (depth, B_GROUPS, B_CHUNK, B_CHUNK)
    assert seq_len % 1024 == 0
    m = batch * seq_len

    c8 = jnp.zeros((8, d), F32).at[:batch].set(c)
    mod = _modulation(c8, w_ada[0], b_ada[0])
    mod3 = mod[:batch].reshape(batch, N_MOD, d)

    x2 = x.reshape(m, d)
    a1 = _norm_mod(x2, mod3, g_pre_mix[0], seq_len)
    w_in0 = w_in[0]
    p_lin = _proj(a1, w_in0, SEG_Q, 4, _act_linear, "inproj_lin")
    p_og = _proj(a1, w_in0, SEG_OG, 1, _act_silu, "inproj_silu")
    p_z = _proj(a1, w_in0, SEG_Z, 2, _act_gelu, "inproj_gelu")
    p_gate = _proj(a1, w_in0, SEG_GATE, 4, _act_sigmoid, "inproj_sig")

    hg = _hgrn(p_lin, p_og, lb_logits.reshape(2 * (depth + 1), A_WIDTH), g_hgrn_norm[0], batch, seq_len)

    bias_full = jnp.repeat(b_spatial[0].T, 128, axis=1)
    sg = _sgu(p_z, g_sgu_norm[0], w_spatial[0], bias_full)

    h1, a2 = _mix(hg, sg, p_gate, x2, mod3, g_post_mix[0], g_pre_ffn[0],
                  w_a_out[0].astype(BF16), w_b_out[0].astype(BF16), w_o[0].astype(BF16), seq_len)
    hid = _proj(a2, w_ff1[0], 0, w_ff1.shape[2] // SEG, _act_relu2, "ff1")
    out = _resproj(hid, w_ff2[0], h1, mod3, g_post_ffn[0], 5, seq_len, tm=1024, tk=512, name="ff2")
    return out.reshape(batch, seq_len, d)
```

```python
import functools

import jax
import jax.numpy as jnp
from jax import lax
from jax.experimental import pallas as pl
from jax.experimental.pallas import tpu as pltpu

F32 = jnp.float32
BF16 = jnp.bfloat16
EPS = 1e-6
LOG2E = 1.4426950408889634

A_HEADS = 8
A_DIM = 128
A_WIDTH = A_HEADS * A_DIM
B_GROUPS = 8
B_CHUNK = 128
B_WIDTH = B_GROUPS * 128
N_MOD = 6
SEG = 1024
SEG_Q, SEG_F, SEG_V, SEG_OG, SEG_Z, SEG_GATE = 0, 1, 3, 4, 5, 7

HGRN_CHUNK = 64
HGRN_BLOCK = 16
PROJ_SUB = 128
V7X_VMEM_BYTES = 64 * 1024 * 1024


def _vmem_limit(estimate_bytes):
    return int(min(estimate_bytes, V7X_VMEM_BYTES - 4 * 1024 * 1024))


def _sigmoid(x):
    return 1.0 / (1.0 + jnp.exp(-x))


def _dot(a, b):
    return jnp.dot(a, b, preferred_element_type=F32)


def _dot_nt(a, b):
    return lax.dot_general(a, b, (((1,), (1,)), ((), ())), preferred_element_type=F32)


def _dot_tn(a, b):
    return lax.dot_general(a, b, (((0,), (0,)), ((), ())), preferred_element_type=F32)


def _rms(x):
    return x * lax.rsqrt(jnp.mean(x * x, axis=-1, keepdims=True) + EPS)


def _mod_kernel(c_ref, w_ref, b_ref, o_ref):
    c = c_ref[...]
    s = c * _sigmoid(c)
    o_ref[...] = _dot(s.astype(BF16), w_ref[...].astype(BF16)) + b_ref[...]


def _modulation(c8, w_ada, b_ada):
    d, n = w_ada.shape
    tn = 1024
    return pl.pallas_call(
        _mod_kernel,
        grid=(n // tn,),
        in_specs=[
            pl.BlockSpec((8, d), lambda j: (0, 0)),
            pl.BlockSpec((d, tn), lambda j: (0, j)),
            pl.BlockSpec((1, tn), lambda j: (0, j)),
        ],
        out_specs=pl.BlockSpec((8, tn), lambda j: (0, j)),
        out_shape=jax.ShapeDtypeStruct((8, n), F32),
        compiler_params=pltpu.CompilerParams(
            dimension_semantics=("arbitrary",),
            vmem_limit_bytes=_vmem_limit(2 * d * tn * 4 + d * tn * 2 + (8 << 20))),
        name="mod",
    )(c8, w_ada, b_ada.reshape(1, n))


def _normproj_kernel(x_ref, mod_ref, g_ref, w_ref, a_ref, o_ref, w_scr):
    @pl.when(pl.program_id(0) == 0)
    def _():
        w_scr[...] = w_ref[...].astype(BF16)

    sh = mod_ref[0, 0:1, :]
    sc = mod_ref[0, 1:2, :]
    for s in range(x_ref.shape[0] // PROJ_SUB):
        rs = slice(s * PROJ_SUB, (s + 1) * PROJ_SUB)
        a = (_rms(x_ref[rs, :]) * g_ref[...] * (1.0 + sc) + sh).astype(BF16)
        a_ref[rs, :] = a
        o_ref[rs, :] = _act_silu(_dot(a, w_scr[...]), 0).astype(BF16)


def _normproj(x2, mod3, g, w, seg, rows_per_batch):
    m, d = x2.shape
    tm, tn = 512, SEG
    bpb = rows_per_batch // tm
    est = 2 * tm * d * 6 + 2 * d * tn * 4 + d * tn * 2 + 2 * tm * tn * 2 + 8 * PROJ_SUB * d * 4 + (6 << 20)
    return pl.pallas_call(
        _normproj_kernel,
        grid=(m // tm,),
        in_specs=[
            pl.BlockSpec((tm, d), lambda i: (i, 0)),
            pl.BlockSpec((1, N_MOD, d), lambda i: (i // bpb, 0, 0)),
            pl.BlockSpec((1, d), lambda i: (0, 0)),
            pl.BlockSpec((d, tn), lambda i: (0, seg)),
        ],
        out_specs=[pl.BlockSpec((tm, d), lambda i: (i, 0)), pl.BlockSpec((tm, tn), lambda i: (i, 0))],
        out_shape=[jax.ShapeDtypeStruct((m, d), BF16), jax.ShapeDtypeStruct((m, tn), BF16)],
        scratch_shapes=[pltpu.VMEM((d, tn), BF16)],
        compiler_params=pltpu.CompilerParams(
            dimension_semantics=("arbitrary",),
            vmem_limit_bytes=_vmem_limit(est)),
        name="norm_silu",
    )(x2, mod3, g.reshape(1, d), w)


def _act_linear(acc, j):
    return acc * jnp.where(j == 0, A_DIM ** -0.5, 1.0)


def _act_silu(acc, j):
    return acc * _sigmoid(acc)


def _act_gelu(acc, j):
    return 0.5 * acc * (1.0 + lax.erf(acc * (2.0 ** -0.5)))


def _act_sigmoid(acc, j):
    return _sigmoid(acc)


def _act_relu2(acc, j):
    r = jnp.maximum(acc, 0.0)
    return r * r


def _proj_kernel(a_ref, w_ref, o_ref, w_scr, *, act):
    j = pl.program_id(0)

    @pl.when(pl.program_id(1) == 0)
    def _():
        w_scr[...] = w_ref[...].astype(BF16)

    for s in range(a_ref.shape[0] // PROJ_SUB):
        rs = slice(s * PROJ_SUB, (s + 1) * PROJ_SUB)
        o_ref[rs, :] = act(_dot(a_ref[rs, :], w_scr[...]), j).astype(BF16)


def _proj(a, w, seg0, nseg, act, name, seg_stride=1, tm=2048):
    m, kdim = a.shape
    tn = SEG
    est = 2 * tm * kdim * 2 + 2 * kdim * tn * 4 + kdim * tn * 2 + 2 * tm * tn * 2 + 8 * PROJ_SUB * tn * 4 + (6 << 20)
    return pl.pallas_call(
        functools.partial(_proj_kernel, act=act),
        grid=(nseg, m // tm),
        in_specs=[
            pl.BlockSpec((tm, kdim), lambda j, i: (i, 0)),
            pl.BlockSpec((kdim, tn), lambda j, i: (0, seg0 + j * seg_stride)),
        ],
        out_specs=pl.BlockSpec((tm, tn), lambda j, i: (i, j)),
        out_shape=jax.ShapeDtypeStruct((m, nseg * tn), BF16),
        scratch_shapes=[pltpu.VMEM((kdim, tn), BF16)],
        compiler_params=pltpu.CompilerParams(
            dimension_semantics=("arbitrary", "arbitrary"),
            vmem_limit_bytes=_vmem_limit(est)),
        name=name,
    )(a, w)


def _hgrn_kernel(q_ref, ffw_ref, fbw_ref, v_ref, og_ref, lbl_ref, gn_ref,
                 o_ref, ofw_scr, obw_scr, st_scr, *, heads_per_step, seq_len):
    t = HGRN_CHUNK
    n_chunks = seq_len // t
    half = t // 2
    f_refs = (ffw_ref, fbw_ref)

    lbl = lbl_ref[...]
    c0s, c1s = [], []
    for d in range(2):
        l0 = lbl[2 * d:2 * d + 1, :]
        l1 = lbl[2 * d + 1:2 * d + 2, :]
        mx = jnp.maximum(l0, l1)
        e0 = jnp.exp(l0 - mx)
        lb = e0 / (e0 + jnp.exp(l1 - mx))
        c0s.append(0.5 * (1.0 + lb))
        c1s.append(0.5 * (1.0 - lb))

    st_scr[...] = jnp.zeros_like(st_scr)

    row = lax.broadcasted_iota(jnp.int32, (t, t), 0)
    col = lax.broadcasted_iota(jnp.int32, (t, t), 1)
    masks = (col <= row, col >= row)
    row2 = lax.broadcasted_iota(jnp.int32, (t, 2 * t), 0)
    col2 = lax.broadcasted_iota(jnp.int32, (t, 2 * t), 1) & (t - 1)
    tris = (jnp.where(col2 <= row2, 1.0, 0.0).astype(BF16), jnp.where(col2 >= row2, 1.0, 0.0).astype(BF16))

    def stage0(c, h, d):
        rows = pl.ds(pl.multiple_of(c * t, t), t)
        hs = slice(h * A_DIM, (h + 1) * A_DIM)
        ct = c1s[d][:, hs] * jnp.tanh(0.5 * f_refs[d][rows, hs].astype(F32))
        k = c1s[d][:, hs] - ct
        lf = jnp.log(c0s[d][:, hs] + ct)
        hi = lf.astype(BF16)
        lo = (lf - hi.astype(F32)).astype(BF16)
        return h, d, rows, hs, k, _dot(tris[d], jnp.concatenate([hi, lo], axis=0))

    def stage1(h, d, rows, hs, k, b):
        b = b * LOG2E
        q = q_ref[rows, hs].astype(F32)
        if d == 0:
            b_end = b[t - 1:t, :]
            b_mid = b[half - 1:half, :]
        else:
            b_end = b[0:1, :]
            b_mid = b[half:half + 1, :]
        qd = (q * jnp.exp2(b)).astype(BF16)
        ke = (k * jnp.exp2(b_end - b)).astype(BF16)
        qm = (q * jnp.exp2(b - b_mid)).astype(BF16)
        km = (k * jnp.exp2(b_mid - b)).astype(BF16)
        return h, d, rows, hs, _dot_nt(qm, km), qd, ke, jnp.exp2(b_end)

    def stage2(h, d, rows, hs, att, qd, ke, decay):
        v = v_ref[rows, hs]
        att = jnp.where(masks[d], att, 0.0).astype(BF16)
        idx = 2 * h + d
        s_t = st_scr[idx]
        o = _dot(att, v) + _dot_nt(qd, s_t.astype(BF16))
        st_scr[idx] = s_t * decay + _dot_tn(v, ke)
        o_scr = ofw_scr if d == 0 else obw_scr
        o_scr[rows, hs] = o

    units = [(h, d) for h in range(heads_per_step) for d in range(2)]

    def block(cb, carry):
        after0, after1 = {}, {}
        for step in range(HGRN_BLOCK + 2):
            if step < HGRN_BLOCK:
                c = cb * HGRN_BLOCK + step
                after0[step] = [stage0(c if d == 0 else n_chunks - 1 - c, h, d) for h, d in units]
            if 1 <= step <= HGRN_BLOCK:
                after1[step - 1] = [stage1(*vals) for vals in after0.pop(step - 1)]
            if step >= 2:
                for vals in after1.pop(step - 2):
                    stage2(*vals)
        return carry

    lax.fori_loop(0, n_chunks // HGRN_BLOCK, block, 0)

    fin_rows = 256

    def fin(i, carry):
        rows = pl.ds(pl.multiple_of(i * fin_rows, fin_rows), fin_rows)
        for h in range(heads_per_step):
            hs = slice(h * A_DIM, (h + 1) * A_DIM)
            y = _rms(ofw_scr[rows, hs] + obw_scr[rows, hs]) * gn_ref[...]
            o_ref[rows, hs] = (y * og_ref[rows, hs].astype(F32)).astype(BF16)
        return carry

    lax.fori_loop(0, seq_len // fin_rows, fin, 0)


def _hgrn(p_lin, p_og, lb_logits4, g_norm, batch, seq_len):
    hps = 2
    wblk = hps * A_DIM
    nblk = A_WIDTH // wblk

    def seg_spec(seg):
        return pl.BlockSpec((seq_len, wblk), lambda b, p: (b, seg * nblk + p))

    est = 2 * 6 * seq_len * wblk * 2 + 2 * seq_len * wblk * 4 + (6 << 20)
    return pl.pallas_call(
        functools.partial(_hgrn_kernel, heads_per_step=hps, seq_len=seq_len),
        grid=(batch, nblk),
        in_specs=[
            seg_spec(0), seg_spec(1), seg_spec(2), seg_spec(3),
            seg_spec(0),
            pl.BlockSpec((4, wblk), lambda b, p: (0, p)),
            pl.BlockSpec((1, A_DIM), lambda b, p: (0, 0)),
        ],
        out_specs=pl.BlockSpec((seq_len, wblk), lambda b, p: (b, p)),
        out_shape=jax.ShapeDtypeStruct((batch * seq_len, A_WIDTH), BF16),
        scratch_shapes=[
            pltpu.VMEM((seq_len, wblk), F32),
            pltpu.VMEM((seq_len, wblk), F32),
            pltpu.VMEM((2 * hps, A_DIM, A_DIM), F32),
        ],
        compiler_params=pltpu.CompilerParams(
            dimension_semantics=("arbitrary", "arbitrary"),
            vmem_limit_bytes=_vmem_limit(est)),
        name="hgrn",
    )(p_lin, p_lin, p_lin, p_lin, p_og, lb_logits4, g_norm.reshape(1, A_DIM))


def _sgu_kernel(u_ref, v_ref, g_ref, ws_ref, bs_ref, o_ref, *, chunks_per_step):
    v = v_ref[...].astype(F32)
    mu = jnp.mean(v, axis=-1, keepdims=True)
    dlt = v - mu
    y = (dlt * lax.rsqrt(jnp.mean(dlt * dlt, axis=-1, keepdims=True) + EPS) * g_ref[...]).astype(BF16)
    for n in range(chunks_per_step):
        rs = slice(n * B_CHUNK, (n + 1) * B_CHUNK)
        for g in range(B_GROUPS):
            cs = slice(g * 128, (g + 1) * 128)
            vm = _dot(ws_ref[g].astype(BF16), y[rs, cs]) + bs_ref[:, cs]
            o_ref[rs, cs] = (u_ref[rs, cs].astype(F32) * vm).astype(BF16)


def _sgu(p_z, g_v, w_s, bias_full):
    m = p_z.shape[0]
    cps = 4
    tm = cps * B_CHUNK
    return pl.pallas_call(
        functools.partial(_sgu_kernel, chunks_per_step=cps),
        grid=(m // tm,),
        in_specs=[
            pl.BlockSpec((tm, B_WIDTH), lambda i: (i, 0)),
            pl.BlockSpec((tm, B_WIDTH), lambda i: (i, 1)),
            pl.BlockSpec((1, B_WIDTH), lambda i: (0, 0)),
            pl.BlockSpec((B_GROUPS, B_CHUNK, B_CHUNK), lambda i: (0, 0, 0)),
            pl.BlockSpec((B_CHUNK, B_WIDTH), lambda i: (0, 0)),
        ],
        out_specs=pl.BlockSpec((tm, B_WIDTH), lambda i: (i, 0)),
        out_shape=jax.ShapeDtypeStruct((m, B_WIDTH), BF16),
        compiler_params=pltpu.CompilerParams(
            dimension_semantics=("arbitrary",),
            vmem_limit_bytes=_vmem_limit(32 << 20)),
        name="sgu",
    )(p_z, p_z, g_v.reshape(1, B_WIDTH), w_s, bias_full)


def _mix_kernel(hg_ref, sg_ref, ga_ref, gb_ref, x_ref, mod_ref, gpost_ref, gnext_ref,
                wa_ref, wb_ref, wo_ref, h_ref, a_ref, *, sub):
    gt1 = mod_ref[0, 2:3, :]
    sh2 = mod_ref[0, 3:4, :]
    sc2 = mod_ref[0, 4:5, :]
    def branches(rs):
        return rs, _dot(hg_ref[rs, :], wa_ref[...]), _dot(sg_ref[rs, :], wb_ref[...])

    def project(rs, ya, yb):
        merged = ga_ref[rs, :].astype(F32) * ya + gb_ref[rs, :].astype(F32) * yb
        return rs, _dot(merged.astype(BF16), wo_ref[...])

    def finish(rs, mo):
        h = x_ref[rs, :] + gt1 * (_rms(mo) * gpost_ref[...])
        h_ref[rs, :] = h
        a_ref[rs, :] = (_rms(h) * gnext_ref[...] * (1.0 + sc2) + sh2).astype(BF16)

    n_sub = hg_ref.shape[0] // sub
    after0, after1 = {}, {}
    for step in range(n_sub + 2):
        if step < n_sub:
            after0[step] = branches(slice(step * sub, (step + 1) * sub))
        if 1 <= step <= n_sub:
            after1[step - 1] = project(*after0.pop(step - 1))
        if step >= 2:
            finish(*after1.pop(step - 2))


def _mix(hg, sg, p_gate, x2, mod3, g_post, g_next, wa, wb, wo, rows_per_batch):
    m, d = x2.shape
    tm, sub = 512, 256
    bpb = rows_per_batch // tm
    const = lambda shape: pl.BlockSpec(shape, lambda i: (0, 0), pipeline_mode=pl.Buffered(1))
    row = lambda width, col: pl.BlockSpec((tm, width), lambda i: (i, col))
    vec_spec = pl.BlockSpec((1, d), lambda i: (0, 0))
    est = (2 * A_WIDTH * d + d * d) * 2 + 2 * tm * (2 * A_WIDTH * 2 + 2 * d * 2 + d * 4 + d * 4 + d * 2) \
        + 6 * tm * d * 4 + (4 << 20)
    return pl.pallas_call(
        functools.partial(_mix_kernel, sub=sub),
        grid=(m // tm,),
        in_specs=[
            row(A_WIDTH, 0), row(B_WIDTH, 0), row(d, 0), row(d, 1), row(d, 0),
            pl.BlockSpec((1, N_MOD, d), lambda i: (i // bpb, 0, 0)),
            vec_spec, vec_spec,
            const((A_WIDTH, d)), const((B_WIDTH, d)), const((d, d)),
        ],
        out_specs=[row(d, 0), row(d, 0)],
        out_shape=[jax.ShapeDtypeStruct((m, d), F32), jax.ShapeDtypeStruct((m, d), BF16)],
        compiler_params=pltpu.CompilerParams(
            dimension_semantics=("arbitrary",),
            vmem_limit_bytes=_vmem_limit(est)),
        name="mix",
    )(hg, sg, p_gate, p_gate, x2, mod3, g_post.reshape(1, d), g_next.reshape(1, d), wa, wb, wo)


def _resproj_kernel(lhs_ref, w_ref, res_ref, mod_ref, gpost_ref, h_ref, *, gate_row, sub):
    k = pl.program_id(1)
    last = pl.num_programs(1) - 1

    @pl.when(k == 0)
    def _():
        h_ref[...] = _dot(lhs_ref[...], w_ref[...].astype(BF16))

    @pl.when((k > 0) & (k < last))
    def _():
        h_ref[...] += _dot(lhs_ref[...], w_ref[...].astype(BF16))

    @pl.when(k == last)
    def _():
        w = w_ref[...].astype(BF16)
        gate = mod_ref[0, gate_row:gate_row + 1, :]
        for s in range(h_ref.shape[0] // sub):
            rs = slice(s * sub, (s + 1) * sub)
            acc = h_ref[rs, :] + _dot(lhs_ref[rs, :], w)
            h_ref[rs, :] = res_ref[rs, :] + gate * (_rms(acc) * gpost_ref[...])


def _resproj(lhs, w, res, mod3, g_post, gate_row, rows_per_batch, tm, tk, name):
    m, kdim = lhs.shape
    d = w.shape[1]
    bpb = rows_per_batch // tm
    row_spec = pl.BlockSpec((tm, d), lambda i, k: (i, 0))
    est = 2 * tm * tk * 2 + 2 * tk * d * 4 + tk * d * 2 + 4 * tm * d * 4 + 2 * tm * d * 4 + (4 << 20)
    return pl.pallas_call(
        functools.partial(_resproj_kernel, gate_row=gate_row, sub=256),
        grid=(m // tm, kdim // tk),
        in_specs=[
            pl.BlockSpec((tm, tk), lambda i, k: (i, k)),
            pl.BlockSpec((tk, d), lambda i, k: (k, 0)),
            row_spec,
            pl.BlockSpec((1, N_MOD, d), lambda i, k: (i // bpb, 0, 0)),
            pl.BlockSpec((1, d), lambda i, k: (0, 0)),
        ],
        out_specs=row_spec,
        out_shape=jax.ShapeDtypeStruct((m, d), F32),
        compiler_params=pltpu.CompilerParams(
            dimension_semantics=("arbitrary", "arbitrary"),
            vmem_limit_bytes=_vmem_limit(est)),
        name=name,
    )(lhs, w, res, mod3, g_post.reshape(1, d))


def kernel(x, c, w_ada, b_ada, g_pre_mix, g_post_mix, g_pre_ffn, g_post_ffn, w_in, lb_logits,
           g_hgrn_norm, w_a_out, g_sgu_norm, w_spatial, b_spatial, w_b_out, w_o, w_ff1, w_ff2):
    batch, seq_len, d = x.shape
    depth = w_in.shape[0]
    assert depth == 1 and lb_logits.shape == (2, depth + 1, A_WIDTH)
    assert w_in.shape[2] == 11 * SEG and d == 2 * SEG
    assert g_hgrn_norm.shape == (depth, A_DIM) and w_spatial.shape == (depth, B_GROUPS, B_CHUNK, B_CHUNK)
    assert seq_len % 1024 == 0
    m = batch * seq_len

    c8 = jnp.zeros((8, d), F32).at[:batch].set(c)
    mod = _modulation(c8, w_ada[0], b_ada[0])
    mod3 = mod[:batch].reshape(batch, N_MOD, d)

    x2 = x.reshape(m, d)
    w_in0 = w_in[0]
    a1, p_og = _normproj(x2, mod3, g_pre_mix[0], w_in0, SEG_OG, seq_len)
    p_lin = _proj(a1, w_in0, SEG_Q, 4, _act_linear, "inproj_lin")
    p_z = _proj(a1, w_in0, SEG_Z, 2, _act_gelu, "inproj_gelu")
    p_gate = _proj(a1, w_in0, SEG_GATE, 4, _act_sigmoid, "inproj_sig")

    hg = _hgrn(p_lin, p_og, lb_logits.reshape(2 * (depth + 1), A_WIDTH), g_hgrn_norm[0], batch, seq_len)

    bias_full = jnp.repeat(b_spatial[0].T, 128, axis=1)
    sg = _sgu(p_z, g_sgu_norm[0], w_spatial[0], bias_full)

    h1, a2 = _mix(hg, sg, p_gate, x2, mod3, g_post_mix[0], g_pre_ffn[0],
                  w_a_out[0].astype(BF16), w_b_out[0].astype(BF16), w_o[0].astype(BF16), seq_len)
    hid = _proj(a2, w_ff1[0], 0, w_ff1.shape[2] // SEG, _act_relu2, "ff1")
    out = _resproj(hid, w_ff2[0], h1, mod3, g_post_ffn[0], 5, seq_len, tm=1024, tk=512, name="ff2")
    return out.reshape(batch, seq_len, d)
```

```python
import functools

import jax
import jax.numpy as jnp
from jax import lax
from jax.experimental import pallas as pl
from jax.experimental.pallas import tpu as pltpu

F32 = jnp.float32
BF16 = jnp.bfloat16
EPS = 1e-6
LOG2E = 1.4426950408889634

A_HEADS = 8
A_DIM = 128
A_WIDTH = A_HEADS * A_DIM
B_GROUPS = 8
B_CHUNK = 128
B_WIDTH = B_GROUPS * 128
N_MOD = 6
SEG = 1024
SEG_Q, SEG_F, SEG_V, SEG_OG, SEG_Z, SEG_GATE = 0, 1, 3, 4, 5, 7

HGRN_CHUNK = 64
HGRN_BLOCK = 16
PROJ_SUB = 128
V7X_VMEM_BYTES = 64 * 1024 * 1024


def _vmem_limit(estimate_bytes):
    return int(min(estimate_bytes, V7X_VMEM_BYTES - 4 * 1024 * 1024))


def _sigmoid(x):
    return 1.0 / (1.0 + jnp.exp(-x))


def _dot(a, b):
    return jnp.dot(a, b, preferred_element_type=F32)


def _dot_nt(a, b):
    return lax.dot_general(a, b, (((1,), (1,)), ((), ())), preferred_element_type=F32)


def _dot_tn(a, b):
    return lax.dot_general(a, b, (((0,), (0,)), ((), ())), preferred_element_type=F32)


def _rms(x):
    return x * lax.rsqrt(jnp.mean(x * x, axis=-1, keepdims=True) + EPS)


def _mod_kernel(c_ref, w_ref, b_ref, o_ref):
    c = c_ref[...]
    s = c * _sigmoid(c)
    o_ref[...] = _dot(s.astype(BF16), w_ref[...].astype(BF16)) + b_ref[...]


def _modulation(c8, w_ada, b_ada):
    d, n = w_ada.shape
    tn = 1024
    return pl.pallas_call(
        _mod_kernel,
        grid=(n // tn,),
        in_specs=[
            pl.BlockSpec((8, d), lambda j: (0, 0)),
            pl.BlockSpec((d, tn), lambda j: (0, j)),
            pl.BlockSpec((1, tn), lambda j: (0, j)),
        ],
        out_specs=pl.BlockSpec((8, tn), lambda j: (0, j)),
        out_shape=jax.ShapeDtypeStruct((8, n), F32),
        compiler_params=pltpu.CompilerParams(
            dimension_semantics=("arbitrary",),
            vmem_limit_bytes=_vmem_limit(2 * d * tn * 4 + d * tn * 2 + (8 << 20))),
        name="mod",
    )(c8, w_ada, b_ada.reshape(1, n))


def _normproj_kernel(x_ref, mod_ref, g_ref, w_ref, a_ref, o_ref, w_scr):
    @pl.when(pl.program_id(0) == 0)
    def _():
        w_scr[...] = w_ref[...].astype(BF16)

    sh = mod_ref[0, 0:1, :]
    sc = mod_ref[0, 1:2, :]
    for s in range(x_ref.shape[0] // PROJ_SUB):
        rs = slice(s * PROJ_SUB, (s + 1) * PROJ_SUB)
        a = (_rms(x_ref[rs, :]) * g_ref[...] * (1.0 + sc) + sh).astype(BF16)
        a_ref[rs, :] = a
        o_ref[rs, :] = _act_silu(_dot(a, w_scr[...]), 0).astype(BF16)


def _normproj(x2, mod3, g, w, seg, rows_per_batch):
    m, d = x2.shape
    tm, tn = 512, SEG
    bpb = rows_per_batch // tm
    est = 2 * tm * d * 6 + 2 * d * tn * 4 + d * tn * 2 + 2 * tm * tn * 2 + 8 * PROJ_SUB * d * 4 + (6 << 20)
    return pl.pallas_call(
        _normproj_kernel,
        grid=(m // tm,),
        in_specs=[
            pl.BlockSpec((tm, d), lambda i: (i, 0)),
            pl.BlockSpec((1, N_MOD, d), lambda i: (i // bpb, 0, 0)),
            pl.BlockSpec((1, d), lambda i: (0, 0)),
            pl.BlockSpec((d, tn), lambda i: (0, seg)),
        ],
        out_specs=[pl.BlockSpec((tm, d), lambda i: (i, 0)), pl.BlockSpec((tm, tn), lambda i: (i, 0))],
        out_shape=[jax.ShapeDtypeStruct((m, d), BF16), jax.ShapeDtypeStruct((m, tn), BF16)],
        scratch_shapes=[pltpu.VMEM((d, tn), BF16)],
        compiler_params=pltpu.CompilerParams(
            dimension_semantics=("arbitrary",),
            vmem_limit_bytes=_vmem_limit(est)),
        name="norm_silu",
    )(x2, mod3, g.reshape(1, d), w)


def _act_linear(acc, j):
    return acc * jnp.where(j == 0, A_DIM ** -0.5, 1.0)


def _act_silu(acc, j):
    return acc * _sigmoid(acc)


def _act_gelu(acc, j):
    return 0.5 * acc * (1.0 + lax.erf(acc * (2.0 ** -0.5)))


def _act_sigmoid(acc, j):
    return _sigmoid(acc)


def _act_relu2(acc, j):
    r = jnp.maximum(acc, 0.0)
    return r * r


def _proj_kernel(a_ref, w_ref, o_ref, w_scr, *, act):
    j = pl.program_id(0)

    @pl.when(pl.program_id(1) == 0)
    def _():
        w_scr[...] = w_ref[...].astype(BF16)

    for s in range(a_ref.shape[0] // PROJ_SUB):
        rs = slice(s * PROJ_SUB, (s + 1) * PROJ_SUB)
        o_ref[rs, :] = act(_dot(a_ref[rs, :], w_scr[...]), j).astype(BF16)


def _proj(a, w, seg0, nseg, act, name, seg_stride=1, tm=2048):
    m, kdim = a.shape
    tn = SEG
    est = 2 * tm * kdim * 2 + 2 * kdim * tn * 4 + kdim * tn * 2 + 2 * tm * tn * 2 + 8 * PROJ_SUB * tn * 4 + (6 << 20)
    return pl.pallas_call(
        functools.partial(_proj_kernel, act=act),
        grid=(nseg, m // tm),
        in_specs=[
            pl.BlockSpec((tm, kdim), lambda j, i: (i, 0)),
            pl.BlockSpec((kdim, tn), lambda j, i: (0, seg0 + j * seg_stride)),
        ],
        out_specs=pl.BlockSpec((tm, tn), lambda j, i: (i, j)),
        out_shape=jax.ShapeDtypeStruct((m, nseg * tn), BF16),
        scratch_shapes=[pltpu.VMEM((kdim, tn), BF16)],
        compiler_params=pltpu.CompilerParams(
            dimension_semantics=("arbitrary", "arbitrary"),
            vmem_limit_bytes=_vmem_limit(est)),
        name=name,
    )(a, w)


def _hgrn_kernel(q_ref, ffw_ref, fbw_ref, v_ref, og_ref, lbl_ref, gn_ref,
                 o_ref, ofw_scr, obw_scr, st_scr, *, heads_per_step, seq_len):
    t = HGRN_CHUNK
    n_chunks = seq_len // t
    half = t // 2
    f_refs = (ffw_ref, fbw_ref)

    lbl = lbl_ref[...]
    c0s, c1s = [], []
    for d in range(2):
        l0 = lbl[2 * d:2 * d + 1, :]
        l1 = lbl[2 * d + 1:2 * d + 2, :]
        mx = jnp.maximum(l0, l1)
        e0 = jnp.exp(l0 - mx)
        lb = e0 / (e0 + jnp.exp(l1 - mx))
        c0s.append(0.5 * (1.0 + lb))
        c1s.append(0.5 * (1.0 - lb))

    st_scr[...] = jnp.zeros_like(st_scr)

    row = lax.broadcasted_iota(jnp.int32, (t, t), 0)
    col = lax.broadcasted_iota(jnp.int32, (t, t), 1)
    masks = (col <= row, col >= row)
    row2 = lax.broadcasted_iota(jnp.int32, (t, 2 * t), 0)
    col2 = lax.broadcasted_iota(jnp.int32, (t, 2 * t), 1) & (t - 1)
    tris = (jnp.where(col2 <= row2, 1.0, 0.0).astype(BF16), jnp.where(col2 >= row2, 1.0, 0.0).astype(BF16))

    def stage0(c, h, d):
        rows = pl.ds(pl.multiple_of(c * t, t), t)
        hs = slice(h * A_DIM, (h + 1) * A_DIM)
        ct = c1s[d][:, hs] * jnp.tanh(0.5 * f_refs[d][rows, hs].astype(F32))
        k = c1s[d][:, hs] - ct
        lf = jnp.log(c0s[d][:, hs] + ct)
        hi = lf.astype(BF16)
        lo = (lf - hi.astype(F32)).astype(BF16)
        return h, d, rows, hs, k, _dot(tris[d], jnp.concatenate([hi, lo], axis=0))

    def stage1(h, d, rows, hs, k, b):
        b = b * LOG2E
        q = q_ref[rows, hs].astype(F32)
        if d == 0:
            b_end = b[t - 1:t, :]
            b_mid = b[half - 1:half, :]
        else:
            b_end = b[0:1, :]
            b_mid = b[half:half + 1, :]
        qd = (q * jnp.exp2(b)).astype(BF16)
        ke = (k * jnp.exp2(b_end - b)).astype(BF16)
        qm = (q * jnp.exp2(b - b_mid)).astype(BF16)
        km = (k * jnp.exp2(b_mid - b)).astype(BF16)
        return h, d, rows, hs, _dot_nt(qm, km), qd, ke, jnp.exp2(b_end)

    def stage2(h, d, rows, hs, att, qd, ke, decay):
        v = v_ref[rows, hs]
        att = jnp.where(masks[d], att, 0.0).astype(BF16)
        idx = 2 * h + d
        s_t = st_scr[idx]
        o = _dot(att, v) + _dot_nt(qd, s_t.astype(BF16))
        st_scr[idx] = s_t * decay + _dot_tn(v, ke)
        o_scr = ofw_scr if d == 0 else obw_scr
        o_scr[rows, hs] = o

    units = [(h, d) for h in range(heads_per_step) for d in range(2)]

    def block(cb, carry):
        after0, after1 = {}, {}
        for step in range(HGRN_BLOCK + 2):
            if step < HGRN_BLOCK:
                c = cb * HGRN_BLOCK + step
                after0[step] = [stage0(c if d == 0 else n_chunks - 1 - c, h, d) for h, d in units]
            if 1 <= step <= HGRN_BLOCK:
                after1[step - 1] = [stage1(*vals) for vals in after0.pop(step - 1)]
            if step >= 2:
                for vals in after1.pop(step - 2):
                    stage2(*vals)
        return carry

    lax.fori_loop(0, n_chunks // HGRN_BLOCK, block, 0)

    fin_rows = 256

    def fin(i, carry):
        rows = pl.ds(pl.multiple_of(i * fin_rows, fin_rows), fin_rows)
        for h in range(heads_per_step):
            hs = slice(h * A_DIM, (h + 1) * A_DIM)
            y = _rms(ofw_scr[rows, hs] + obw_scr[rows, hs]) * gn_ref[...]
            o_ref[rows, hs] = (y * og_ref[rows, hs].astype(F32)).astype(BF16)
        return carry

    lax.fori_loop(0, seq_len // fin_rows, fin, 0)


def _hgrn(p_lin, p_og, lb_logits4, g_norm, batch, seq_len):
    hps = 2
    wblk = hps * A_DIM
    nblk = A_WIDTH // wblk

    def seg_spec(seg):
        return pl.BlockSpec((seq_len, wblk), lambda b, p: (b, seg * nblk + p))

    est = 2 * 6 * seq_len * wblk * 2 + 2 * seq_len * wblk * 4 + (6 << 20)
    return pl.pallas_call(
        functools.partial(_hgrn_kernel, heads_per_step=hps, seq_len=seq_len),
        grid=(batch, nblk),
        in_specs=[
            seg_spec(0), seg_spec(1), seg_spec(2), seg_spec(3),
            seg_spec(0),
            pl.BlockSpec((4, wblk), lambda b, p: (0, p)),
            pl.BlockSpec((1, A_DIM), lambda b, p: (0, 0)),
        ],
        out_specs=pl.BlockSpec((seq_len, wblk), lambda b, p: (b, p)),
        out_shape=jax.ShapeDtypeStruct((batch * seq_len, A_WIDTH), BF16),
        scratch_shapes=[
            pltpu.VMEM((seq_len, wblk), F32),
            pltpu.VMEM((seq_len, wblk), F32),
            pltpu.VMEM((2 * hps, A_DIM, A_DIM), F32),
        ],
        compiler_params=pltpu.CompilerParams(
            dimension_semantics=("arbitrary", "arbitrary"),
            vmem_limit_bytes=_vmem_limit(est)),
        name="hgrn",
    )(p_lin, p_lin, p_lin, p_lin, p_og, lb_logits4, g_norm.reshape(1, A_DIM))


def _sgu_kernel(u_ref, v_ref, g_ref, ws_ref, bs_ref, o_ref, *, chunks_per_step):
    v = v_ref[...].astype(F32)
    mu = jnp.mean(v, axis=-1, keepdims=True)
    dlt = v - mu
    y = (dlt * lax.rsqrt(jnp.mean(dlt * dlt, axis=-1, keepdims=True) + EPS) * g_ref[...]).astype(BF16)
    for n in range(chunks_per_step):
        rs = slice(n * B_CHUNK, (n + 1) * B_CHUNK)
        for g in range(B_GROUPS):
            cs = slice(g * 128, (g + 1) * 128)
            vm = _dot(ws_ref[g].astype(BF16), y[rs, cs]) + bs_ref[:, cs]
            o_ref[rs, cs] = (u_ref[rs, cs].astype(F32) * vm).astype(BF16)


def _sgu(p_z, g_v, w_s, bias_full):
    m = p_z.shape[0]
    cps = 4
    tm = cps * B_CHUNK
    return pl.pallas_call(
        functools.partial(_sgu_kernel, chunks_per_step=cps),
        grid=(m // tm,),
        in_specs=[
            pl.BlockSpec((tm, B_WIDTH), lambda i: (i, 0)),
            pl.BlockSpec((tm, B_WIDTH), lambda i: (i, 1)),
            pl.BlockSpec((1, B_WIDTH), lambda i: (0, 0)),
            pl.BlockSpec((B_GROUPS, B_CHUNK, B_CHUNK), lambda i: (0, 0, 0)),
            pl.BlockSpec((B_CHUNK, B_WIDTH), lambda i: (0, 0)),
        ],
        out_specs=pl.BlockSpec((tm, B_WIDTH), lambda i: (i, 0)),
        out_shape=jax.ShapeDtypeStruct((m, B_WIDTH), BF16),
        compiler_params=pltpu.CompilerParams(
            dimension_semantics=("arbitrary",),
            vmem_limit_bytes=_vmem_limit(32 << 20)),
        name="sgu",
    )(p_z, p_z, g_v.reshape(1, B_WIDTH), w_s, bias_full)


def _mix_kernel(hg_ref, sg_ref, ga_ref, gb_ref, x_ref, mod_ref, gpost_ref, gnext_ref,
                wa_ref, wb_ref, wo_ref, h_ref, a_ref, *, sub):
    gt1 = mod_ref[0, 2:3, :]
    sh2 = mod_ref[0, 3:4, :]
    sc2 = mod_ref[0, 4:5, :]
    def branches(rs):
        return rs, _dot(hg_ref[rs, :], wa_ref[...]), _dot(sg_ref[rs, :], wb_ref[...])

    def project(rs, ya, yb):
        merged = ga_ref[rs, :].astype(F32) * ya + gb_ref[rs, :].astype(F32) * yb
        return rs, _dot(merged.astype(BF16), wo_ref[...])

    def finish(rs, mo):
        h = x_ref[rs, :] + gt1 * (_rms(mo) * gpost_ref[...])
        h_ref[rs, :] = h
        a_ref[rs, :] = (_rms(h) * gnext_ref[...] * (1.0 + sc2) + sh2).astype(BF16)

    n_sub = hg_ref.shape[0] // sub
    after0, after1 = {}, {}
    for step in range(n_sub + 2):
        if step < n_sub:
            after0[step] = branches(slice(step * sub, (step + 1) * sub))
        if 1 <= step <= n_sub:
            after1[step - 1] = project(*after0.pop(step - 1))
        if step >= 2:
            finish(*after1.pop(step - 2))


def _mix(hg, sg, p_gate, x2, mod3, g_post, g_next, wa, wb, wo, rows_per_batch):
    m, d = x2.shape
    tm, sub = 512, 256
    bpb = rows_per_batch // tm
    const = lambda shape: pl.BlockSpec(shape, lambda i: (0, 0), pipeline_mode=pl.Buffered(1))
    row = lambda width, col: pl.BlockSpec((tm, width), lambda i: (i, col))
    vec_spec = pl.BlockSpec((1, d), lambda i: (0, 0))
    est = (2 * A_WIDTH * d + d * d) * 2 + 2 * tm * (2 * A_WIDTH * 2 + 2 * d * 2 + d * 4 + d * 4 + d * 2) \
        + 6 * tm * d * 4 + (4 << 20)
    return pl.pallas_call(
        functools.partial(_mix_kernel, sub=sub),
        grid=(m // tm,),
        in_specs=[
            row(A_WIDTH, 0), row(B_WIDTH, 0), row(d, 0), row(d, 1), row(d, 0),
            pl.BlockSpec((1, N_MOD, d), lambda i: (i // bpb, 0, 0)),
            vec_spec, vec_spec,
            const((A_WIDTH, d)), const((B_WIDTH, d)), const((d, d)),
        ],
        out_specs=[row(d, 0), row(d, 0)],
        out_shape=[jax.ShapeDtypeStruct((m, d), F32), jax.ShapeDtypeStruct((m, d), BF16)],
        compiler_params=pltpu.CompilerParams(
            dimension_semantics=("arbitrary",),
            vmem_limit_bytes=_vmem_limit(est)),
        name="mix",
    )(hg, sg, p_gate, p_gate, x2, mod3, g_post.reshape(1, d), g_next.reshape(1, d), wa, wb, wo)


def _resproj_kernel(lhs_ref, w_ref, res_ref, mod_ref, gpost_ref, h_ref, res_scr, *, gate_row, sub):
    k = pl.program_id(1)
    last = pl.num_programs(1) - 1

    @pl.when(k == 0)
    def _():
        res_scr[0] = res_ref[...]
        h_ref[...] = _dot(lhs_ref[...], w_ref[...].astype(BF16))

    @pl.when((k > 0) & (k < last))
    def _():
        res_scr[k] = res_ref[...]
        h_ref[...] += _dot(lhs_ref[...], w_ref[...].astype(BF16))

    @pl.when(k == last)
    def _():
        res_scr[res_scr.shape[0] - 1] = res_ref[...]
        w = w_ref[...].astype(BF16)
        gate = mod_ref[0, gate_row:gate_row + 1, :]
        for s in range(h_ref.shape[0] // sub):
            rs = slice(s * sub, (s + 1) * sub)
            acc = h_ref[rs, :] + _dot(lhs_ref[rs, :], w)
            res = jnp.concatenate([res_scr[kk, rs, :] for kk in range(res_scr.shape[0])], axis=-1)
            h_ref[rs, :] = res + gate * (_rms(acc) * gpost_ref[...])


def _resproj(lhs, w, res, mod3, g_post, gate_row, rows_per_batch, tm, tk, name):
    m, kdim = lhs.shape
    d = w.shape[1]
    bpb = rows_per_batch // tm
    nk = kdim // tk
    piece = d // nk
    assert piece % 128 == 0
    est = 2 * tm * tk * 2 + 2 * tk * d * 4 + tk * d * 2 + 3 * tm * d * 4 + 2 * tm * piece * 4 + 2 * tm * d * 4 \
        + (4 << 20)
    return pl.pallas_call(
        functools.partial(_resproj_kernel, gate_row=gate_row, sub=256),
        grid=(m // tm, nk),
        in_specs=[
            pl.BlockSpec((tm, tk), lambda i, k: (i, k)),
            pl.BlockSpec((tk, d), lambda i, k: (k, 0)),
            pl.BlockSpec((tm, piece), lambda i, k: (i, k)),
            pl.BlockSpec((1, N_MOD, d), lambda i, k: (i // bpb, 0, 0)),
            pl.BlockSpec((1, d), lambda i, k: (0, 0)),
        ],
        out_specs=pl.BlockSpec((tm, d), lambda i, k: (i, 0)),
        out_shape=jax.ShapeDtypeStruct((m, d), F32),
        scratch_shapes=[pltpu.VMEM((nk, tm, piece), F32)],
        compiler_params=pltpu.CompilerParams(
            dimension_semantics=("arbitrary", "arbitrary"),
            vmem_limit_bytes=_vmem_limit(est)),
        name=name,
    )(lhs, w, res, mod3, g_post.reshape(1, d))


def kernel(x, c, w_ada, b_ada, g_pre_mix, g_post_mix, g_pre_ffn, g_post_ffn, w_in, lb_logits,
           g_hgrn_norm, w_a_out, g_sgu_norm, w_spatial, b_spatial, w_b_out, w_o, w_ff1, w_ff2):
    batch, seq_len, d = x.shape
    depth = w_in.shape[0]
    assert depth == 1 and lb_logits.shape == (2, depth + 1, A_WIDTH)
    assert w_in.shape[2] == 11 * SEG and d == 2 * SEG
    assert g_hgrn_norm.shape == (depth, A_DIM) and w_spatial.shape == (depth, B_GROUPS, B_CHUNK, B_CHUNK)
    assert seq_len % 1024 == 0
    m = batch * seq_len

    c8 = jnp.zeros((8, d), F32).at[:batch].set(c)
    mod = _modulation(c8, w_ada[0], b_ada[0])
    mod3 = mod[:batch].reshape(batch, N_MOD, d)

    x2 = x.reshape(m, d)
    w_in0 = w_in[0]
    a1, p_og = _normproj(x2, mod3, g_pre_mix[0], w_in0, SEG_OG, seq_len)
    p_lin = _proj(a1, w_in0, SEG_Q, 4, _act_linear, "inproj_lin")
    p_z = _proj(a1, w_in0, SEG_Z, 2, _act_gelu, "inproj_gelu")
    p_gate = _proj(a1, w_in0, SEG_GATE, 4, _act_sigmoid, "inproj_sig")

    hg = _hgrn(p_lin, p_og, lb_logits.reshape(2 * (depth + 1), A_WIDTH), g_hgrn_norm[0], batch, seq_len)

    bias_full = jnp.repeat(b_spatial[0].T, 128, axis=1)
    sg = _sgu(p_z, g_sgu_norm[0], w_spatial[0], bias_full)

    h1, a2 = _mix(hg, sg, p_gate, x2, mod3, g_post_mix[0], g_pre_ffn[0],
                  w_a_out[0].astype(BF16), w_b_out[0].astype(BF16), w_o[0].astype(BF16), seq_len)
    hid = _proj(a2, w_ff1[0], 0, w_ff1.shape[2] // SEG, _act_relu2, "ff1")
    out = _resproj(hid, w_ff2[0], h1, mod3, g_post_ffn[0], 5, seq_len, tm=1024, tk=512, name="ff2")
    return out.reshape(batch, seq_len, d)
```

```python
import functools

import jax
import jax.numpy as jnp
from jax import lax
from jax.experimental import pallas as pl
from jax.experimental.pallas import tpu as pltpu

F32 = jnp.float32
BF16 = jnp.bfloat16
EPS = 1e-6
LOG2E = 1.4426950408889634

A_HEADS = 8
A_DIM = 128
A_WIDTH = A_HEADS * A_DIM
B_GROUPS = 8
B_CHUNK = 128
B_WIDTH = B_GROUPS * 128
N_MOD = 6
SEG = 1024
SEG_Q, SEG_F, SEG_V, SEG_OG, SEG_Z, SEG_GATE = 0, 1, 3, 4, 5, 7

HGRN_CHUNK = 64
PROJ_SUB = 128
V7X_VMEM_BYTES = 64 * 1024 * 1024


def _vmem_limit(estimate_bytes):
    return int(min(estimate_bytes, V7X_VMEM_BYTES - 4 * 1024 * 1024))


def _sigmoid(x):
    return 1.0 / (1.0 + jnp.exp(-x))


def _dot(a, b):
    return jnp.dot(a, b, preferred_element_type=F32)


def _dot_nt(a, b):
    return lax.dot_general(a, b, (((1,), (1,)), ((), ())), preferred_element_type=F32)


def _dot_tn(a, b):
    return lax.dot_general(a, b, (((0,), (0,)), ((), ())), preferred_element_type=F32)


def _rms(x):
    return x * lax.rsqrt(jnp.mean(x * x, axis=-1, keepdims=True) + EPS)


def _mod_kernel(c_ref, w_ref, b_ref, o_ref):
    c = c_ref[...]
    s = c * _sigmoid(c)
    o_ref[...] = _dot(s.astype(BF16), w_ref[...].astype(BF16)) + b_ref[...]


def _modulation(c8, w_ada, b_ada):
    d, n = w_ada.shape
    tn = 1024
    return pl.pallas_call(
        _mod_kernel,
        grid=(n // tn,),
        in_specs=[
            pl.BlockSpec((8, d), lambda j: (0, 0)),
            pl.BlockSpec((d, tn), lambda j: (0, j)),
            pl.BlockSpec((1, tn), lambda j: (0, j)),
        ],
        out_specs=pl.BlockSpec((8, tn), lambda j: (0, j)),
        out_shape=jax.ShapeDtypeStruct((8, n), F32),
        compiler_params=pltpu.CompilerParams(
            dimension_semantics=("arbitrary",),
            vmem_limit_bytes=_vmem_limit(2 * d * tn * 4 + d * tn * 2 + (8 << 20))),
        name="mod",
    )(c8, w_ada, b_ada.reshape(1, n))


def _normproj_kernel(x_ref, mod_ref, g_ref, w_ref, a_ref, o_ref, w_scr):
    @pl.when(pl.program_id(0) == 0)
    def _():
        w_scr[...] = w_ref[...].astype(BF16)

    sh = mod_ref[0, 0:1, :]
    sc = mod_ref[0, 1:2, :]
    for s in range(x_ref.shape[0] // PROJ_SUB):
        rs = slice(s * PROJ_SUB, (s + 1) * PROJ_SUB)
        a = (_rms(x_ref[rs, :]) * g_ref[...] * (1.0 + sc) + sh).astype(BF16)
        a_ref[rs, :] = a
        o_ref[rs, :] = _act_silu(_dot(a, w_scr[...]), 0).astype(BF16)


def _normproj(x2, mod3, g, w, seg, rows_per_batch):
    m, d = x2.shape
    tm, tn = 1024, SEG
    bpb = rows_per_batch // tm
    est = 2 * tm * d * 6 + 2 * d * tn * 4 + d * tn * 2 + 2 * tm * tn * 2 + 8 * PROJ_SUB * d * 4 + (6 << 20)
    return pl.pallas_call(
        _normproj_kernel,
        grid=(m // tm,),
        in_specs=[
            pl.BlockSpec((tm, d), lambda i: (i, 0)),
            pl.BlockSpec((1, N_MOD, d), lambda i: (i // bpb, 0, 0)),
            pl.BlockSpec((1, d), lambda i: (0, 0)),
            pl.BlockSpec((d, tn), lambda i: (0, seg)),
        ],
        out_specs=[pl.BlockSpec((tm, d), lambda i: (i, 0)), pl.BlockSpec((tm, tn), lambda i: (i, 0))],
        out_shape=[jax.ShapeDtypeStruct((m, d), BF16), jax.ShapeDtypeStruct((m, tn), BF16)],
        scratch_shapes=[pltpu.VMEM((d, tn), BF16)],
        compiler_params=pltpu.CompilerParams(
            dimension_semantics=("arbitrary",),
            vmem_limit_bytes=_vmem_limit(est)),
        name="norm_silu",
    )(x2, mod3, g.reshape(1, d), w)


def _act_linear(acc, j):
    return acc * jnp.where(j == 0, A_DIM ** -0.5, 1.0)


def _act_silu(acc, j):
    return acc * _sigmoid(acc)


def _act_gelu(acc, j):
    return 0.5 * acc * (1.0 + lax.erf(acc * (2.0 ** -0.5)))


def _act_sigmoid(acc, j):
    return _sigmoid(acc)


def _act_relu2(acc, j):
    r = jnp.maximum(acc, 0.0)
    return r * r


def _proj_kernel(a_ref, w_ref, o_ref, w_scr, *, act):
    j = pl.program_id(0)

    @pl.when(pl.program_id(1) == 0)
    def _():
        w_scr[...] = w_ref[...].astype(BF16)

    for s in range(a_ref.shape[0] // PROJ_SUB):
        rs = slice(s * PROJ_SUB, (s + 1) * PROJ_SUB)
        o_ref[rs, :] = act(_dot(a_ref[rs, :], w_scr[...]), j).astype(BF16)


def _proj(a, w, seg0, nseg, act, name, seg_stride=1, tm=2048):
    m, kdim = a.shape
    tn = SEG
    est = 2 * tm * kdim * 2 + 2 * kdim * tn * 4 + kdim * tn * 2 + 2 * tm * tn * 2 + 8 * PROJ_SUB * tn * 4 + (6 << 20)
    return pl.pallas_call(
        functools.partial(_proj_kernel, act=act),
        grid=(nseg, m // tm),
        in_specs=[
            pl.BlockSpec((tm, kdim), lambda j, i: (i, 0)),
            pl.BlockSpec((kdim, tn), lambda j, i: (0, seg0 + j * seg_stride)),
        ],
        out_specs=pl.BlockSpec((tm, tn), lambda j, i: (i, j)),
        out_shape=jax.ShapeDtypeStruct((m, nseg * tn), BF16),
        scratch_shapes=[pltpu.VMEM((kdim, tn), BF16)],
        compiler_params=pltpu.CompilerParams(
            dimension_semantics=("arbitrary", "arbitrary"),
            vmem_limit_bytes=_vmem_limit(est)),
        name=name,
    )(a, w)


def _hgrn_kernel(q_ref, ffw_ref, fbw_ref, v_ref, og_ref, lbl_ref, gn_ref,
                 o_ref, ofw_scr, obw_scr, st_scr, *, heads_per_step, seq_len):
    t = HGRN_CHUNK
    n_chunks = seq_len // t
    half = t // 2
    f_refs = (ffw_ref, fbw_ref)

    lbl = lbl_ref[...]
    c0s, c1s = [], []
    for d in range(2):
        l0 = lbl[2 * d:2 * d + 1, :]
        l1 = lbl[2 * d + 1:2 * d + 2, :]
        mx = jnp.maximum(l0, l1)
        e0 = jnp.exp(l0 - mx)
        lb = e0 / (e0 + jnp.exp(l1 - mx))
        c0s.append(0.5 * (1.0 + lb))
        c1s.append(0.5 * (1.0 - lb))

    st_scr[...] = jnp.zeros_like(st_scr)

    row = lax.broadcasted_iota(jnp.int32, (t, t), 0)
    col = lax.broadcasted_iota(jnp.int32, (t, t), 1)
    masks = (col <= row, col >= row)
    row2 = lax.broadcasted_iota(jnp.int32, (t, 2 * t), 0)
    col2 = lax.broadcasted_iota(jnp.int32, (t, 2 * t), 1) & (t - 1)
    tris = (jnp.where(col2 <= row2, 1.0, 0.0).astype(BF16), jnp.where(col2 >= row2, 1.0, 0.0).astype(BF16))

    def stage0(c, h, d):
        rows = pl.ds(pl.multiple_of(c * t, t), t)
        hs = slice(h * A_DIM, (h + 1) * A_DIM)
        ct = c1s[d][:, hs] * jnp.tanh(0.5 * f_refs[d][rows, hs].astype(F32))
        k = c1s[d][:, hs] - ct
        lf = jnp.log(c0s[d][:, hs] + ct)
        hi = lf.astype(BF16)
        lo = (lf - hi.astype(F32)).astype(BF16)
        return h, d, rows, hs, k, _dot(tris[d], jnp.concatenate([hi, lo], axis=0))

    def stage1(h, d, rows, hs, k, b):
        b = b * LOG2E
        q = q_ref[rows, hs].astype(F32)
        if d == 0:
            b_end = b[t - 1:t, :]
            b_mid = b[half - 1:half, :]
        else:
            b_end = b[0:1, :]
            b_mid = b[half:half + 1, :]
        qd = (q * jnp.exp2(b)).astype(BF16)
        ke = (k * jnp.exp2(b_end - b)).astype(BF16)
        qm = (q * jnp.exp2(b - b_mid)).astype(BF16)
        km = (k * jnp.exp2(b_mid - b)).astype(BF16)
        return h, d, rows, hs, _dot_nt(qm, km), qd, ke, jnp.exp2(b_end)

    def stage2(finalize, h, d, rows, hs, att, qd, ke, decay):
        v = v_ref[rows, hs]
        att = jnp.where(masks[d], att, 0.0).astype(BF16)
        idx = 2 * h + d
        s_t = st_scr[idx]
        o = _dot(att, v) + _dot_nt(qd, s_t.astype(BF16))
        st_scr[idx] = s_t * decay + _dot_tn(v, ke)
        if finalize:
            other = obw_scr if d == 0 else ofw_scr
            y = _rms(o + other[rows, hs]) * gn_ref[...]
            o_ref[rows, hs] = (y * og_ref[rows, hs].astype(F32)).astype(BF16)
        else:
            o_scr = ofw_scr if d == 0 else obw_scr
            o_scr[rows, hs] = o

    units = [(h, d) for h in range(heads_per_step) for d in range(2)]

    def half_sweep(c0, finalize):
        after0, after1 = {}, {}
        n_half = n_chunks // 2
        for step in range(n_half + 2):
            if step < n_half:
                c = c0 + step
                after0[step] = [stage0(c if d == 0 else n_chunks - 1 - c, h, d) for h, d in units]
            if 1 <= step <= n_half:
                after1[step - 1] = [stage1(*vals) for vals in after0.pop(step - 1)]
            if step >= 2:
                for vals in after1.pop(step - 2):
                    stage2(finalize, *vals)

    half_sweep(0, False)
    half_sweep(n_chunks // 2, True)


def _hgrn(p_lin, p_og, lb_logits4, g_norm, batch, seq_len):
    hps = 2
    wblk = hps * A_DIM
    nblk = A_WIDTH // wblk

    def seg_spec(seg):
        return pl.BlockSpec((seq_len, wblk), lambda b, p: (b, seg * nblk + p))

    est = 2 * 6 * seq_len * wblk * 2 + 2 * seq_len * wblk * 4 + (6 << 20)
    return pl.pallas_call(
        functools.partial(_hgrn_kernel, heads_per_step=hps, seq_len=seq_len),
        grid=(batch, nblk),
        in_specs=[
            seg_spec(0), seg_spec(1), seg_spec(2), seg_spec(3),
            seg_spec(0),
            pl.BlockSpec((4, wblk), lambda b, p: (0, p)),
            pl.BlockSpec((1, A_DIM), lambda b, p: (0, 0)),
        ],
        out_specs=pl.BlockSpec((seq_len, wblk), lambda b, p: (b, p)),
        out_shape=jax.ShapeDtypeStruct((batch * seq_len, A_WIDTH), BF16),
        scratch_shapes=[
            pltpu.VMEM((seq_len, wblk), F32),
            pltpu.VMEM((seq_len, wblk), F32),
            pltpu.VMEM((2 * hps, A_DIM, A_DIM), F32),
        ],
        compiler_params=pltpu.CompilerParams(
            dimension_semantics=("arbitrary", "arbitrary"),
            vmem_limit_bytes=_vmem_limit(est)),
        name="hgrn",
    )(p_lin, p_lin, p_lin, p_lin, p_og, lb_logits4, g_norm.reshape(1, A_DIM))


def _sgu_kernel(u_ref, v_ref, g_ref, ws_ref, bs_ref, o_ref, *, chunks_per_step):
    v = v_ref[...].astype(F32)
    mu = jnp.mean(v, axis=-1, keepdims=True)
    dlt = v - mu
    y = (dlt * lax.rsqrt(jnp.mean(dlt * dlt, axis=-1, keepdims=True) + EPS) * g_ref[...]).astype(BF16)
    for n in range(chunks_per_step):
        rs = slice(n * B_CHUNK, (n + 1) * B_CHUNK)
        for g in range(B_GROUPS):
            cs = slice(g * 128, (g + 1) * 128)
            vm = _dot(ws_ref[g].astype(BF16), y[rs, cs]) + bs_ref[:, cs]
            o_ref[rs, cs] = (u_ref[rs, cs].astype(F32) * vm).astype(BF16)


def _sgu(p_z, g_v, w_s, bias_full):
    m = p_z.shape[0]
    cps = 4
    tm = cps * B_CHUNK
    return pl.pallas_call(
        functools.partial(_sgu_kernel, chunks_per_step=cps),
        grid=(m // tm,),
        in_specs=[
            pl.BlockSpec((tm, B_WIDTH), lambda i: (i, 0)),
            pl.BlockSpec((tm, B_WIDTH), lambda i: (i, 1)),
            pl.BlockSpec((1, B_WIDTH), lambda i: (0, 0)),
            pl.BlockSpec((B_GROUPS, B_CHUNK, B_CHUNK), lambda i: (0, 0, 0)),
            pl.BlockSpec((B_CHUNK, B_WIDTH), lambda i: (0, 0)),
        ],
        out_specs=pl.BlockSpec((tm, B_WIDTH), lambda i: (i, 0)),
        out_shape=jax.ShapeDtypeStruct((m, B_WIDTH), BF16),
        compiler_params=pltpu.CompilerParams(
            dimension_semantics=("arbitrary",),
            vmem_limit_bytes=_vmem_limit(32 << 20)),
        name="sgu",
    )(p_z, p_z, g_v.reshape(1, B_WIDTH), w_s, bias_full)


def _mix_kernel(hg_ref, sg_ref, ga_ref, gb_ref, x_ref, mod_ref, gpost_ref, gnext_ref,
                wa_ref, wb_ref, wo_ref, h_ref, a_ref, *, sub):
    gt1 = mod_ref[0, 2:3, :]
    sh2 = mod_ref[0, 3:4, :]
    sc2 = mod_ref[0, 4:5, :]
    def branches(rs):
        return rs, _dot(hg_ref[rs, :], wa_ref[...]), _dot(sg_ref[rs, :], wb_ref[...])

    def project(rs, ya, yb):
        merged = ga_ref[rs, :].astype(F32) * ya + gb_ref[rs, :].astype(F32) * yb
        return rs, _dot(merged.astype(BF16), wo_ref[...])

    def finish(rs, mo):
        h = x_ref[rs, :] + gt1 * (_rms(mo) * gpost_ref[...])
        h_ref[rs, :] = h
        a_ref[rs, :] = (_rms(h) * gnext_ref[...] * (1.0 + sc2) + sh2).astype(BF16)

    bounds = [0]
    for size in sub:
        bounds.append(bounds[-1] + size)
    assert bounds[-1] == hg_ref.shape[0]
    n_sub = len(sub)
    after0, after1 = {}, {}
    for step in range(n_sub + 2):
        if step < n_sub:
            after0[step] = branches(slice(bounds[step], bounds[step + 1]))
        if 1 <= step <= n_sub:
            after1[step - 1] = project(*after0.pop(step - 1))
        if step >= 2:
            finish(*after1.pop(step - 2))


def _mix(hg, sg, p_gate, x2, mod3, g_post, g_next, wa, wb, wo, rows_per_batch):
    m, d = x2.shape
    tm, sub = 512, (384, 128)
    bpb = rows_per_batch // tm
    const = lambda shape: pl.BlockSpec(shape, lambda i: (0, 0), pipeline_mode=pl.Buffered(1))
    row = lambda width, col: pl.BlockSpec((tm, width), lambda i: (i, col))
    vec_spec = pl.BlockSpec((1, d), lambda i: (0, 0))
    est = (2 * A_WIDTH * d + d * d) * 2 + 2 * tm * (2 * A_WIDTH * 2 + 2 * d * 2 + d * 4 + d * 4 + d * 2) \
        + 6 * tm * d * 4 + (4 << 20)
    return pl.pallas_call(
        functools.partial(_mix_kernel, sub=sub),
        grid=(m // tm,),
        in_specs=[
            row(A_WIDTH, 0), row(B_WIDTH, 0), row(d, 0), row(d, 1), row(d, 0),
            pl.BlockSpec((1, N_MOD, d), lambda i: (i // bpb, 0, 0)),
            vec_spec, vec_spec,
            const((A_WIDTH, d)), const((B_WIDTH, d)), const((d, d)),
        ],
        out_specs=[row(d, 0), row(d, 0)],
        out_shape=[jax.ShapeDtypeStruct((m, d), F32), jax.ShapeDtypeStruct((m, d), BF16)],
        compiler_params=pltpu.CompilerParams(
            dimension_semantics=("arbitrary",),
            vmem_limit_bytes=_vmem_limit(est)),
        name="mix",
    )(hg, sg, p_gate, p_gate, x2, mod3, g_post.reshape(1, d), g_next.reshape(1, d), wa, wb, wo)


def _resproj_kernel(lhs_ref, w_ref, res_ref, mod_ref, gpost_ref, h_ref, *, gate_row, sub):
    k = pl.program_id(1)
    last = pl.num_programs(1) - 1

    @pl.when(k == 0)
    def _():
        h_ref[...] = _dot(lhs_ref[...], w_ref[...].astype(BF16))

    @pl.when((k > 0) & (k < last))
    def _():
        h_ref[...] += _dot(lhs_ref[...], w_ref[...].astype(BF16))

    @pl.when(k == last)
    def _():
        w = w_ref[...].astype(BF16)
        gate = mod_ref[0, gate_row:gate_row + 1, :]
        for s in range(h_ref.shape[0] // sub):
            rs = slice(s * sub, (s + 1) * sub)
            acc = h_ref[rs, :] + _dot(lhs_ref[rs, :], w)
            h_ref[rs, :] = res_ref[rs, :] + gate * (_rms(acc) * gpost_ref[...])


def _resproj(lhs, w, res, mod3, g_post, gate_row, rows_per_batch, tm, tk, name):
    m, kdim = lhs.shape
    d = w.shape[1]
    bpb = rows_per_batch // tm
    row_spec = pl.BlockSpec((tm, d), lambda i, k: (i, 0))
    est = 2 * tm * tk * 2 + 2 * tk * d * 4 + tk * d * 2 + 4 * tm * d * 4 + 2 * tm * d * 4 + (4 << 20)
    return pl.pallas_call(
        functools.partial(_resproj_kernel, gate_row=gate_row, sub=256),
        grid=(m // tm, kdim // tk),
        in_specs=[
            pl.BlockSpec((tm, tk), lambda i, k: (i, k)),
            pl.BlockSpec((tk, d), lambda i, k: (k, 0)),
            row_spec,
            pl.BlockSpec((1, N_MOD, d), lambda i, k: (i // bpb, 0, 0)),
            pl.BlockSpec((1, d), lambda i, k: (0, 0)),
        ],
        out_specs=row_spec,
        out_shape=jax.ShapeDtypeStruct((m, d), F32),
        compiler_params=pltpu.CompilerParams(
            dimension_semantics=("arbitrary", "arbitrary"),
            vmem_limit_bytes=_vmem_limit(est)),
        name=name,
    )(lhs, w, res, mod3, g_post.reshape(1, d))


def kernel(x, c, w_ada, b_ada, g_pre_mix, g_post_mix, g_pre_ffn, g_post_ffn, w_in, lb_logits,
           g_hgrn_norm, w_a_out, g_sgu_norm, w_spatial, b_spatial, w_b_out, w_o, w_ff1, w_ff2):
    batch, seq_len, d = x.shape
    depth = w_in.shape[0]
    assert depth == 1 and lb_logits.shape == (2, depth + 1, A_WIDTH)
    assert w_in.shape[2] == 11 * SEG and d == 2 * SEG
    assert g_hgrn_norm.shape == (depth, A_DIM) and w_spatial.shape == (depth, B_GROUPS, B_CHUNK, B_CHUNK)
    assert seq_len % 1024 == 0
    m = batch * seq_len

    c8 = jnp.zeros((8, d), F32).at[:batch].set(c)
    mod = _modulation(c8, w_ada[0], b_ada[0])
    mod3 = mod[:batch].reshape(batch, N_MOD, d)

    x2 = x.reshape(m, d)
    w_in0 = w_in[0]
    a1, p_og = _normproj(x2, mod3, g_pre_mix[0], w_in0, SEG_OG, seq_len)
    p_lin = _proj(a1, w_in0, SEG_Q, 4, _act_linear, "inproj_lin")
    p_z = _proj(a1, w_in0, SEG_Z, 2, _act_gelu, "inproj_gelu")
    p_gate = _proj(a1, w_in0, SEG_GATE, 4, _act_sigmoid, "inproj_sig")

    hg = _hgrn(p_lin, p_og, lb_logits.reshape(2 * (depth + 1), A_WIDTH), g_hgrn_norm[0], batch, seq_len)

    bias_full = jnp.repeat(b_spatial[0].T, 128, axis=1)
    sg = _sgu(p_z, g_sgu_norm[0], w_spatial[0], bias_full)

    h1, a2 = _mix(hg, sg, p_gate, x2, mod3, g_post_mix[0], g_pre_ffn[0],
                  w_a_out[0].astype(BF16), w_b_out[0].astype(BF16), w_o[0].astype(BF16), seq_len)
    hid = _proj(a2, w_ff1[0], 0, w_ff1.shape[2] // SEG, _act_relu2, "ff1")
    out = _resproj(hid, w_ff2[0], h1, mod3, g_post_ffn[0], 5, seq_len, tm=1024, tk=512, name="ff2")
    return out.reshape(batch, seq_len, d)
```

```python
import functools

import jax
import jax.numpy as jnp
from jax import lax
from jax.experimental import pallas as pl
from jax.experimental.pallas import tpu as pltpu

F32 = jnp.float32
BF16 = jnp.bfloat16
EPS = 1e-6
LOG2E = 1.4426950408889634

A_HEADS = 8
A_DIM = 128
A_WIDTH = A_HEADS * A_DIM
B_GROUPS = 8
B_CHUNK = 128
B_WIDTH = B_GROUPS * 128
N_MOD = 6
SEG = 1024
SEG_Q, SEG_F, SEG_V, SEG_OG, SEG_Z, SEG_GATE = 0, 1, 3, 4, 5, 7

HGRN_CHUNK = 64
PROJ_SUB = 128
V7X_VMEM_BYTES = 64 * 1024 * 1024


def _vmem_limit(estimate_bytes):
    return int(min(estimate_bytes, V7X_VMEM_BYTES - 4 * 1024 * 1024))


def _sigmoid(x):
    return 1.0 / (1.0 + jnp.exp(-x))


def _dot(a, b):
    return jnp.dot(a, b, preferred_element_type=F32)


def _dot_nt(a, b):
    return lax.dot_general(a, b, (((1,), (1,)), ((), ())), preferred_element_type=F32)


def _dot_tn(a, b):
    return lax.dot_general(a, b, (((0,), (0,)), ((), ())), preferred_element_type=F32)


def _rms(x):
    return x * lax.rsqrt(jnp.mean(x * x, axis=-1, keepdims=True) + EPS)


def _mod_kernel(c_ref, w_ref, b_ref, o_ref):
    c = c_ref[...]
    s = c * _sigmoid(c)
    o_ref[...] = _dot(s.astype(BF16), w_ref[...].astype(BF16)) + b_ref[...]


def _modulation(c8, w_ada, b_ada):
    d, n = w_ada.shape
    tn = 1024
    return pl.pallas_call(
        _mod_kernel,
        grid=(n // tn,),
        in_specs=[
            pl.BlockSpec((8, d), lambda j: (0, 0)),
            pl.BlockSpec((d, tn), lambda j: (0, j)),
            pl.BlockSpec((1, tn), lambda j: (0, j)),
        ],
        out_specs=pl.BlockSpec((8, tn), lambda j: (0, j)),
        out_shape=jax.ShapeDtypeStruct((8, n), F32),
        compiler_params=pltpu.CompilerParams(
            dimension_semantics=("arbitrary",),
            vmem_limit_bytes=_vmem_limit(2 * d * tn * 4 + d * tn * 2 + (8 << 20))),
        name="mod",
    )(c8, w_ada, b_ada.reshape(1, n))


def _normproj_kernel(x_ref, mod_ref, g_ref, w_ref, a_ref, o_ref, w_scr):
    @pl.when(pl.program_id(0) == 0)
    def _():
        w_scr[...] = w_ref[...].astype(BF16)

    sh = mod_ref[0, 0:1, :]
    sc = mod_ref[0, 1:2, :]
    for s in range(x_ref.shape[0] // PROJ_SUB):
        rs = slice(s * PROJ_SUB, (s + 1) * PROJ_SUB)
        a = (_rms(x_ref[rs, :]) * g_ref[...] * (1.0 + sc) + sh).astype(BF16)
        a_ref[rs, :] = a
        o_ref[rs, :] = _act_silu(_dot(a, w_scr[...]), 0).astype(BF16)


def _normproj(x2, mod3, g, w, seg, rows_per_batch):
    m, d = x2.shape
    tm, tn = 1024, SEG
    bpb = rows_per_batch // tm
    est = 2 * tm * d * 6 + 2 * d * tn * 4 + d * tn * 2 + 2 * tm * tn * 2 + 8 * PROJ_SUB * d * 4 + (6 << 20)
    return pl.pallas_call(
        _normproj_kernel,
        grid=(m // tm,),
        in_specs=[
            pl.BlockSpec((tm, d), lambda i: (i, 0)),
            pl.BlockSpec((1, N_MOD, d), lambda i: (i // bpb, 0, 0)),
            pl.BlockSpec((1, d), lambda i: (0, 0)),
            pl.BlockSpec((d, tn), lambda i: (0, seg)),
        ],
        out_specs=[pl.BlockSpec((tm, d), lambda i: (i, 0)), pl.BlockSpec((tm, tn), lambda i: (i, 0))],
        out_shape=[jax.ShapeDtypeStruct((m, d), BF16), jax.ShapeDtypeStruct((m, tn), BF16)],
        scratch_shapes=[pltpu.VMEM((d, tn), BF16)],
        compiler_params=pltpu.CompilerParams(
            dimension_semantics=("arbitrary",),
            vmem_limit_bytes=_vmem_limit(est)),
        name="norm_silu",
    )(x2, mod3, g.reshape(1, d), w)


def _act_linear(acc, j):
    return acc * jnp.where(j == 0, A_DIM ** -0.5, 1.0)


def _act_silu(acc, j):
    return acc * _sigmoid(acc)


def _act_gelu(acc, j):
    return 0.5 * acc * (1.0 + lax.erf(acc * (2.0 ** -0.5)))


def _act_sigmoid(acc, j):
    return _sigmoid(acc)


def _act_relu2(acc, j):
    r = jnp.maximum(acc, 0.0)
    return r * r


def _proj_kernel(a_ref, w_ref, *refs, act, has_side):
    if has_side:
        side_ref, o_ref, side_out_ref, w_scr = refs
        side_out_ref[...] = side_ref[...].astype(BF16)
    else:
        o_ref, w_scr = refs
    j = pl.program_id(0)

    @pl.when(pl.program_id(1) == 0)
    def _():
        w_scr[...] = w_ref[...].astype(BF16)

    for s in range(a_ref.shape[0] // PROJ_SUB):
        rs = slice(s * PROJ_SUB, (s + 1) * PROJ_SUB)
        o_ref[rs, :] = act(_dot(a_ref[rs, :], w_scr[...]), j).astype(BF16)


def _proj(a, w, seg0, nseg, act, name, seg_stride=1, tm=2048, side=None):
    m, kdim = a.shape
    tn = SEG
    ni = m // tm
    in_specs = [
        pl.BlockSpec((tm, kdim), lambda j, i: (i, 0)),
        pl.BlockSpec((kdim, tn), lambda j, i: (0, seg0 + j * seg_stride)),
    ]
    out_specs = [pl.BlockSpec((tm, tn), lambda j, i: (i, j))]
    out_shape = [jax.ShapeDtypeStruct((m, nseg * tn), BF16)]
    args = [a, w]
    est = 2 * tm * kdim * 2 + 2 * kdim * tn * 4 + kdim * tn * 2 + 2 * tm * tn * 2 + 8 * PROJ_SUB * tn * 4 + (6 << 20)
    if side is not None:
        rows, cols = side.shape
        blk = rows // (nseg * ni)
        assert blk * nseg * ni == rows and blk % 16 == 0
        side_spec = pl.BlockSpec((blk, cols), lambda j, i: (j * ni + i, 0))
        in_specs.append(side_spec)
        out_specs.append(side_spec)
        out_shape.append(jax.ShapeDtypeStruct(side.shape, BF16))
        args.append(side)
        est += 2 * blk * cols * 6
    outs = pl.pallas_call(
        functools.partial(_proj_kernel, act=act, has_side=side is not None),
        grid=(nseg, ni),
        in_specs=in_specs,
        out_specs=out_specs,
        out_shape=out_shape,
        scratch_shapes=[pltpu.VMEM((kdim, tn), BF16)],
        compiler_params=pltpu.CompilerParams(
            dimension_semantics=("arbitrary", "arbitrary"),
            vmem_limit_bytes=_vmem_limit(est)),
        name=name,
    )(*args)
    return outs if side is not None else outs[0]


def _hgrn_kernel(q_ref, ffw_ref, fbw_ref, v_ref, og_ref, lbl_ref, gn_ref,
                 o_ref, ofw_scr, obw_scr, st_scr, *, heads_per_step, seq_len):
    t = HGRN_CHUNK
    n_chunks = seq_len // t
    half = t // 2
    f_refs = (ffw_ref, fbw_ref)

    lbl = lbl_ref[...]
    c0s, c1s = [], []
    for d in range(2):
        l0 = lbl[2 * d:2 * d + 1, :]
        l1 = lbl[2 * d + 1:2 * d + 2, :]
        mx = jnp.maximum(l0, l1)
        e0 = jnp.exp(l0 - mx)
        lb = e0 / (e0 + jnp.exp(l1 - mx))
        c0s.append(0.5 * (1.0 + lb))
        c1s.append(0.5 * (1.0 - lb))

    st_scr[...] = jnp.zeros_like(st_scr)

    row = lax.broadcasted_iota(jnp.int32, (t, t), 0)
    col = lax.broadcasted_iota(jnp.int32, (t, t), 1)
    masks = (col <= row, col >= row)
    row2 = lax.broadcasted_iota(jnp.int32, (t, 2 * t), 0)
    col2 = lax.broadcasted_iota(jnp.int32, (t, 2 * t), 1) & (t - 1)
    tris = (jnp.where(col2 <= row2, 1.0, 0.0).astype(BF16), jnp.where(col2 >= row2, 1.0, 0.0).astype(BF16))

    def stage0(c, h, d):
        rows = pl.ds(pl.multiple_of(c * t, t), t)
        hs = slice(h * A_DIM, (h + 1) * A_DIM)
        ct = c1s[d][:, hs] * jnp.tanh(0.5 * f_refs[d][rows, hs].astype(F32))
        k = c1s[d][:, hs] - ct
        lf = jnp.log(c0s[d][:, hs] + ct)
        hi = lf.astype(BF16)
        lo = (lf - hi.astype(F32)).astype(BF16)
        return h, d, rows, hs, k, _dot(tris[d], jnp.concatenate([hi, lo], axis=0))

    def stage1(h, d, rows, hs, k, b):
        b = b * LOG2E
        q = q_ref[rows, hs].astype(F32)
        if d == 0:
            b_end = b[t - 1:t, :]
            b_mid = b[half - 1:half, :]
        else:
            b_end = b[0:1, :]
            b_mid = b[half:half + 1, :]
        qd = (q * jnp.exp2(b)).astype(BF16)
        ke = (k * jnp.exp2(b_end - b)).astype(BF16)
        qm = (q * jnp.exp2(b - b_mid)).astype(BF16)
        km = (k * jnp.exp2(b_mid - b)).astype(BF16)
        return h, d, rows, hs, _dot_nt(qm, km), qd, ke, jnp.exp2(b_end)

    def stage2(finalize, h, d, rows, hs, att, qd, ke, decay):
        v = v_ref[rows, hs]
        att = jnp.where(masks[d], att, 0.0).astype(BF16)
        idx = 2 * h + d
        s_t = st_scr[idx]
        o = _dot(att, v) + _dot_nt(qd, s_t.astype(BF16))
        st_scr[idx] = s_t * decay + _dot_tn(v, ke)
        if finalize:
            other = obw_scr if d == 0 else ofw_scr
            y = _rms(o + other[rows, hs]) * gn_ref[...]
            o_ref[rows, hs] = (y * og_ref[rows, hs].astype(F32)).astype(BF16)
        else:
            o_scr = ofw_scr if d == 0 else obw_scr
            o_scr[rows, hs] = o

    units = [(h, d) for h in range(heads_per_step) for d in range(2)]

    def half_sweep(c0, finalize):
        after0, after1 = {}, {}
        n_half = n_chunks // 2
        for step in range(n_half + 2):
            if step < n_half:
                c = c0 + step
                after0[step] = [stage0(c if d == 0 else n_chunks - 1 - c, h, d) for h, d in units]
            if 1 <= step <= n_half:
                after1[step - 1] = [stage1(*vals) for vals in after0.pop(step - 1)]
            if step >= 2:
                for vals in after1.pop(step - 2):
                    stage2(finalize, *vals)

    half_sweep(0, False)
    half_sweep(n_chunks // 2, True)


def _hgrn(p_lin, p_og, lb_logits4, g_norm, batch, seq_len):
    hps = 2
    wblk = hps * A_DIM
    nblk = A_WIDTH // wblk

    def seg_spec(seg):
        return pl.BlockSpec((seq_len, wblk), lambda b, p: (b, seg * nblk + p))

    est = 2 * 6 * seq_len * wblk * 2 + 2 * seq_len * wblk * 4 + (6 << 20)
    return pl.pallas_call(
        functools.partial(_hgrn_kernel, heads_per_step=hps, seq_len=seq_len),
        grid=(batch, nblk),
        in_specs=[
            seg_spec(0), seg_spec(1), seg_spec(2), seg_spec(3),
            seg_spec(0),
            pl.BlockSpec((4, wblk), lambda b, p: (0, p)),
            pl.BlockSpec((1, A_DIM), lambda b, p: (0, 0)),
        ],
        out_specs=pl.BlockSpec((seq_len, wblk), lambda b, p: (b, p)),
        out_shape=jax.ShapeDtypeStruct((batch * seq_len, A_WIDTH), BF16),
        scratch_shapes=[
            pltpu.VMEM((seq_len, wblk), F32),
            pltpu.VMEM((seq_len, wblk), F32),
            pltpu.VMEM((2 * hps, A_DIM, A_DIM), F32),
        ],
        compiler_params=pltpu.CompilerParams(
            dimension_semantics=("arbitrary", "arbitrary"),
            vmem_limit_bytes=_vmem_limit(est)),
        name="hgrn",
    )(p_lin, p_lin, p_lin, p_lin, p_og, lb_logits4, g_norm.reshape(1, A_DIM))


def _sgu_kernel(u_ref, v_ref, g_ref, ws_ref, bs_ref, o_ref, *, chunks_per_step):
    v = v_ref[...].astype(F32)
    mu = jnp.mean(v, axis=-1, keepdims=True)
    dlt = v - mu
    y = (dlt * lax.rsqrt(jnp.mean(dlt * dlt, axis=-1, keepdims=True) + EPS) * g_ref[...]).astype(BF16)
    for n in range(chunks_per_step):
        rs = slice(n * B_CHUNK, (n + 1) * B_CHUNK)
        for g in range(B_GROUPS):
            cs = slice(g * 128, (g + 1) * 128)
            vm = _dot(ws_ref[g].astype(BF16), y[rs, cs]) + bs_ref[:, cs]
            o_ref[rs, cs] = (u_ref[rs, cs].astype(F32) * vm).astype(BF16)


def _sgu(p_z, g_v, w_s, bias_full):
    m = p_z.shape[0]
    cps = 4
    tm = cps * B_CHUNK
    return pl.pallas_call(
        functools.partial(_sgu_kernel, chunks_per_step=cps),
        grid=(m // tm,),
        in_specs=[
            pl.BlockSpec((tm, B_WIDTH), lambda i: (i, 0)),
            pl.BlockSpec((tm, B_WIDTH), lambda i: (i, 1)),
            pl.BlockSpec((1, B_WIDTH), lambda i: (0, 0)),
            pl.BlockSpec((B_GROUPS, B_CHUNK, B_CHUNK), lambda i: (0, 0, 0)),
            pl.BlockSpec((B_CHUNK, B_WIDTH), lambda i: (0, 0)),
        ],
        out_specs=pl.BlockSpec((tm, B_WIDTH), lambda i: (i, 0)),
        out_shape=jax.ShapeDtypeStruct((m, B_WIDTH), BF16),
        compiler_params=pltpu.CompilerParams(
            dimension_semantics=("arbitrary",),
            vmem_limit_bytes=_vmem_limit(32 << 20)),
        name="sgu",
    )(p_z, p_z, g_v.reshape(1, B_WIDTH), w_s, bias_full)


def _mix_kernel(hg_ref, sg_ref, ga_ref, gb_ref, x_ref, mod_ref, gpost_ref, gnext_ref,
                wa_ref, wb_ref, wo_ref, h_ref, a_ref, *, sub):
    gt1 = mod_ref[0, 2:3, :]
    sh2 = mod_ref[0, 3:4, :]
    sc2 = mod_ref[0, 4:5, :]
    def branches(rs):
        return rs, _dot(hg_ref[rs, :], wa_ref[...]), _dot(sg_ref[rs, :], wb_ref[...])

    def project(rs, ya, yb):
        merged = ga_ref[rs, :].astype(F32) * ya + gb_ref[rs, :].astype(F32) * yb
        return rs, _dot(merged.astype(BF16), wo_ref[...])

    def finish(rs, mo):
        h = x_ref[rs, :] + gt1 * (_rms(mo) * gpost_ref[...])
        h_ref[rs, :] = h
        a_ref[rs, :] = (_rms(h) * gnext_ref[...] * (1.0 + sc2) + sh2).astype(BF16)

    bounds = [0]
    for size in sub:
        bounds.append(bounds[-1] + size)
    assert bounds[-1] == hg_ref.shape[0]
    n_sub = len(sub)
    after0, after1 = {}, {}
    for step in range(n_sub + 2):
        if step < n_sub:
            after0[step] = branches(slice(bounds[step], bounds[step + 1]))
        if 1 <= step <= n_sub:
            after1[step - 1] = project(*after0.pop(step - 1))
        if step >= 2:
            finish(*after1.pop(step - 2))


def _mix(hg, sg, p_gate, x2, mod3, g_post, g_next, wa, wb, wo, rows_per_batch):
    m, d = x2.shape
    tm, sub = 512, (384, 128)
    bpb = rows_per_batch // tm
    const = lambda shape: pl.BlockSpec(shape, lambda i: (0, 0), pipeline_mode=pl.Buffered(1))
    row = lambda width, col: pl.BlockSpec((tm, width), lambda i: (i, col))
    vec_spec = pl.BlockSpec((1, d), lambda i: (0, 0))
    est = (2 * A_WIDTH * d + d * d) * 2 + 2 * tm * (2 * A_WIDTH * 2 + 2 * d * 2 + d * 4 + d * 4 + d * 2) \
        + 6 * tm * d * 4 + (4 << 20)
    return pl.pallas_call(
        functools.partial(_mix_kernel, sub=sub),
        grid=(m // tm,),
        in_specs=[
            row(A_WIDTH, 0), row(B_WIDTH, 0), row(d, 0), row(d, 1), row(d, 0),
            pl.BlockSpec((1, N_MOD, d), lambda i: (i // bpb, 0, 0)),
            vec_spec, vec_spec,
            const((A_WIDTH, d)), const((B_WIDTH, d)), const((d, d)),
        ],
        out_specs=[row(d, 0), row(d, 0)],
        out_shape=[jax.ShapeDtypeStruct((m, d), F32), jax.ShapeDtypeStruct((m, d), BF16)],
        compiler_params=pltpu.CompilerParams(
            dimension_semantics=("arbitrary",),
            vmem_limit_bytes=_vmem_limit(est)),
        name="mix",
    )(hg, sg, p_gate, p_gate, x2, mod3, g_post.reshape(1, d), g_next.reshape(1, d), wa, wb, wo)


def _resproj_kernel(lhs_ref, w_ref, res_ref, mod_ref, gpost_ref, h_ref, *, gate_row, sub):
    k = pl.program_id(1)
    last = pl.num_programs(1) - 1

    @pl.when(k == 0)
    def _():
        h_ref[...] = _dot(lhs_ref[...], w_ref[...].astype(BF16))

    @pl.when((k > 0) & (k < last))
    def _():
        h_ref[...] += _dot(lhs_ref[...], w_ref[...].astype(BF16))

    @pl.when(k == last)
    def _():
        w = w_ref[...].astype(BF16)
        gate = mod_ref[0, gate_row:gate_row + 1, :]
        for s in range(h_ref.shape[0] // sub):
            rs = slice(s * sub, (s + 1) * sub)
            acc = h_ref[rs, :] + _dot(lhs_ref[rs, :], w)
            h_ref[rs, :] = res_ref[rs, :] + gate * (_rms(acc) * gpost_ref[...])


def _resproj(lhs, w, res, mod3, g_post, gate_row, rows_per_batch, tm, tk, name):
    m, kdim = lhs.shape
    d = w.shape[1]
    bpb = rows_per_batch // tm
    row_spec = pl.BlockSpec((tm, d), lambda i, k: (i, 0))
    est = 2 * tm * tk * 2 + 2 * tk * d * 4 + tk * d * 2 + 4 * tm * d * 4 + 2 * tm * d * 4 + (4 << 20)
    return pl.pallas_call(
        functools.partial(_resproj_kernel, gate_row=gate_row, sub=256),
        grid=(m // tm, kdim // tk),
        in_specs=[
            pl.BlockSpec((tm, tk), lambda i, k: (i, k)),
            pl.BlockSpec((tk, d), lambda i, k: (k, 0)),
            row_spec,
            pl.BlockSpec((1, N_MOD, d), lambda i, k: (i // bpb, 0, 0)),
            pl.BlockSpec((1, d), lambda i, k: (0, 0)),
        ],
        out_specs=row_spec,
        out_shape=jax.ShapeDtypeStruct((m, d), F32),
        compiler_params=pltpu.CompilerParams(
            dimension_semantics=("arbitrary", "arbitrary"),
            vmem_limit_bytes=_vmem_limit(est)),
        name=name,
    )(lhs, w, res, mod3, g_post.reshape(1, d))


def kernel(x, c, w_ada, b_ada, g_pre_mix, g_post_mix, g_pre_ffn, g_post_ffn, w_in, lb_logits,
           g_hgrn_norm, w_a_out, g_sgu_norm, w_spatial, b_spatial, w_b_out, w_o, w_ff1, w_ff2):
    batch, seq_len, d = x.shape
    depth = w_in.shape[0]
    assert depth == 1 and lb_logits.shape == (2, depth + 1, A_WIDTH)
    assert w_in.shape[2] == 11 * SEG and d == 2 * SEG
    assert g_hgrn_norm.shape == (depth, A_DIM) and w_spatial.shape == (depth, B_GROUPS, B_CHUNK, B_CHUNK)
    assert seq_len % 1024 == 0
    m = batch * seq_len

    c8 = jnp.zeros((8, d), F32).at[:batch].set(c)
    mod = _modulation(c8, w_ada[0], b_ada[0])
    mod3 = mod[:batch].reshape(batch, N_MOD, d)

    x2 = x.reshape(m, d)
    w_in0 = w_in[0]
    a1, p_og = _normproj(x2, mod3, g_pre_mix[0], w_in0, SEG_OG, seq_len)
    p_lin, wa = _proj(a1, w_in0, SEG_Q, 4, _act_linear, "inproj_lin", side=w_a_out[0])
    p_z, wb = _proj(a1, w_in0, SEG_Z, 2, _act_gelu, "inproj_gelu", side=w_b_out[0])
    p_gate, wo = _proj(a1, w_in0, SEG_GATE, 4, _act_sigmoid, "inproj_sig", side=w_o[0])

    hg = _hgrn(p_lin, p_og, lb_logits.reshape(2 * (depth + 1), A_WIDTH), g_hgrn_norm[0], batch, seq_len)

    bias_full = jnp.repeat(b_spatial[0].T, 128, axis=1)
    sg = _sgu(p_z, g_sgu_norm[0], w_spatial[0], bias_full)

    h1, a2 = _mix(hg, sg, p_gate, x2, mod3, g_post_mix[0], g_pre_ffn[0], wa, wb, wo, seq_len)
    hid, w2 = _proj(a2, w_ff1[0], 0, w_ff1.shape[2] // SEG, _act_relu2, "ff1", side=w_ff2[0])
    out = _resproj(hid, w2, h1, mod3, g_post_ffn[0], 5, seq_len, tm=1024, tk=1024, name="ff2")
    return out.reshape(batch, seq_len, d)
```

```python
import functools

import jax
import jax.numpy as jnp
from jax import lax
from jax.experimental import pallas as pl
from jax.experimental.pallas import tpu as pltpu

F32 = jnp.float32
BF16 = jnp.bfloat16
EPS = 1e-6
LOG2E = 1.4426950408889634

A_HEADS = 8
A_DIM = 128
A_WIDTH = A_HEADS * A_DIM
B_GROUPS = 8
B_CHUNK = 128
B_WIDTH = B_GROUPS * 128
N_MOD = 6
SEG = 1024
SEG_Q, SEG_F, SEG_V, SEG_OG, SEG_Z, SEG_GATE = 0, 1, 3, 4, 5, 7

HGRN_CHUNK = 64
PROJ_SUB = 128
V7X_VMEM_BYTES = 64 * 1024 * 1024


def _vmem_limit(estimate_bytes):
    return int(min(estimate_bytes, V7X_VMEM_BYTES - 4 * 1024 * 1024))


def _sigmoid(x):
    return 1.0 / (1.0 + jnp.exp(-x))


def _dot(a, b):
    return jnp.dot(a, b, preferred_element_type=F32)


def _dot_nt(a, b):
    return lax.dot_general(a, b, (((1,), (1,)), ((), ())), preferred_element_type=F32)


def _dot_tn(a, b):
    return lax.dot_general(a, b, (((0,), (0,)), ((), ())), preferred_element_type=F32)


def _rms(x):
    return x * lax.rsqrt(jnp.mean(x * x, axis=-1, keepdims=True) + EPS)


def _mod_block(c_ref, w_ref, b_ref):
    c = c_ref[...]
    s = c * _sigmoid(c)
    return _dot(s.astype(BF16), w_ref[...].astype(BF16)) + b_ref[...]


def _mod_kernel(c_ref, w_ref, b_ref, o_ref):
    o_ref[...] = _mod_block(c_ref, w_ref, b_ref)


def _modulation(c8, w_ada, b_ada, n):
    d = w_ada.shape[0]
    tn = 1024
    return pl.pallas_call(
        _mod_kernel,
        grid=(n // tn,),
        in_specs=[
            pl.BlockSpec((8, d), lambda j: (0, 0)),
            pl.BlockSpec((d, tn), lambda j: (0, j)),
            pl.BlockSpec((1, tn), lambda j: (0, j)),
        ],
        out_specs=pl.BlockSpec((8, tn), lambda j: (0, j)),
        out_shape=jax.ShapeDtypeStruct((8, n), F32),
        compiler_params=pltpu.CompilerParams(
            dimension_semantics=("arbitrary",),
            vmem_limit_bytes=_vmem_limit(2 * d * tn * 4 + d * tn * 2 + (8 << 20))),
        name="mod",
    )(c8, w_ada, b_ada)


def _normproj_kernel(x_ref, mod_ref, g_ref, w_ref, a_ref, o_ref, w_scr):
    @pl.when(pl.program_id(0) == 0)
    def _():
        w_scr[...] = w_ref[...].astype(BF16)

    sh = mod_ref[0, 0:1, :]
    sc = mod_ref[0, 1:2, :]
    for s in range(x_ref.shape[0] // PROJ_SUB):
        rs = slice(s * PROJ_SUB, (s + 1) * PROJ_SUB)
        a = (_rms(x_ref[rs, :]) * g_ref[...] * (1.0 + sc) + sh).astype(BF16)
        a_ref[rs, :] = a
        o_ref[rs, :] = _act_silu(_dot(a, w_scr[...]), 0).astype(BF16)


def _normproj(x2, mod3, g, w, seg, rows_per_batch):
    m, d = x2.shape
    tm, tn = 1024, SEG
    bpb = rows_per_batch // tm
    est = 2 * tm * d * 6 + 2 * d * tn * 4 + d * tn * 2 + 2 * tm * tn * 2 + 8 * PROJ_SUB * d * 4 + (6 << 20)
    return pl.pallas_call(
        _normproj_kernel,
        grid=(m // tm,),
        in_specs=[
            pl.BlockSpec((tm, d), lambda i: (i, 0)),
            pl.BlockSpec((1, mod3.shape[1], d), lambda i: (i // bpb, 0, 0)),
            pl.BlockSpec((1, d), lambda i: (0, 0)),
            pl.BlockSpec((d, tn), lambda i: (0, seg)),
        ],
        out_specs=[pl.BlockSpec((tm, d), lambda i: (i, 0)), pl.BlockSpec((tm, tn), lambda i: (i, 0))],
        out_shape=[jax.ShapeDtypeStruct((m, d), BF16), jax.ShapeDtypeStruct((m, tn), BF16)],
        scratch_shapes=[pltpu.VMEM((d, tn), BF16)],
        compiler_params=pltpu.CompilerParams(
            dimension_semantics=("arbitrary",),
            vmem_limit_bytes=_vmem_limit(est)),
        name="norm_silu",
    )(x2, mod3, g.reshape(1, d), w)


def _act_linear(acc, j):
    return acc * jnp.where(j == 0, A_DIM ** -0.5, 1.0)


def _act_silu(acc, j):
    return acc * _sigmoid(acc)


def _act_gelu(acc, j):
    return 0.5 * acc * (1.0 + lax.erf(acc * (2.0 ** -0.5)))


def _act_sigmoid(acc, j):
    return _sigmoid(acc)


def _act_relu2(acc, j):
    r = jnp.maximum(acc, 0.0)
    return r * r


def _proj_kernel(a_ref, w_ref, *refs, act, has_side, has_mod):
    refs = list(refs)
    side_ref = refs.pop(0) if has_side else None
    mod_in = [refs.pop(0) for _ in range(3)] if has_mod else None
    o_ref = refs.pop(0)
    if has_side:
        refs.pop(0)[...] = side_ref[...].astype(BF16)
    if has_mod:
        refs.pop(0)[...] = _mod_block(*mod_in)
    (w_scr,) = refs
    j = pl.program_id(0)

    @pl.when(pl.program_id(1) == 0)
    def _():
        w_scr[...] = w_ref[...].astype(BF16)

    for s in range(a_ref.shape[0] // PROJ_SUB):
        rs = slice(s * PROJ_SUB, (s + 1) * PROJ_SUB)
        o_ref[rs, :] = act(_dot(a_ref[rs, :], w_scr[...]), j).astype(BF16)


def _proj(a, w, seg0, nseg, act, name, seg_stride=1, tm=2048, side=None, mod_tail=None):
    m, kdim = a.shape
    tn = SEG
    ni = m // tm
    in_specs = [
        pl.BlockSpec((tm, kdim), lambda j, i: (i, 0)),
        pl.BlockSpec((kdim, tn), lambda j, i: (0, seg0 + j * seg_stride)),
    ]
    out_specs = [pl.BlockSpec((tm, tn), lambda j, i: (i, j))]
    out_shape = [jax.ShapeDtypeStruct((m, nseg * tn), BF16)]
    args = [a, w]
    est = 2 * tm * kdim * 2 + 2 * kdim * tn * 4 + kdim * tn * 2 + 2 * tm * tn * 2 + 8 * PROJ_SUB * tn * 4 + (6 << 20)
    if side is not None:
        rows, cols = side.shape
        blk = rows // (nseg * ni)
        assert blk * nseg * ni == rows and blk % 16 == 0
        side_spec = pl.BlockSpec((blk, cols), lambda j, i: (j * ni + i, 0))
        in_specs.append(side_spec)
        out_specs.append(side_spec)
        out_shape.append(jax.ShapeDtypeStruct(side.shape, BF16))
        args.append(side)
        est += 2 * blk * cols * 6
    if mod_tail is not None:
        c8, w_ada, b_ada, col0 = mod_tail
        dm, n_all = w_ada.shape
        mblk = (n_all - col0) // (nseg * ni)
        assert mblk * nseg * ni == n_all - col0 and mblk % 128 == 0 and col0 % mblk == 0
        in_specs += [
            pl.BlockSpec((8, dm), lambda j, i: (0, 0)),
            pl.BlockSpec((dm, mblk), lambda j, i: (0, col0 // mblk + j * ni + i)),
            pl.BlockSpec((1, mblk), lambda j, i: (0, col0 // mblk + j * ni + i)),
        ]
        out_specs.append(pl.BlockSpec((8, mblk), lambda j, i: (0, j * ni + i)))
        out_shape.append(jax.ShapeDtypeStruct((8, n_all - col0), F32))
        args += [c8, w_ada, b_ada]
        est += 2 * dm * mblk * 4 + dm * mblk * 2
    single = side is None and mod_tail is None
    outs = pl.pallas_call(
        functools.partial(_proj_kernel, act=act, has_side=side is not None, has_mod=mod_tail is not None),
        grid=(nseg, ni),
        in_specs=in_specs,
        out_specs=out_specs,
        out_shape=out_shape,
        scratch_shapes=[pltpu.VMEM((kdim, tn), BF16)],
        compiler_params=pltpu.CompilerParams(
            dimension_semantics=("arbitrary", "arbitrary"),
            vmem_limit_bytes=_vmem_limit(est)),
        name=name,
    )(*args)
    return outs[0] if single else outs


def _hgrn_kernel(q_ref, ffw_ref, fbw_ref, v_ref, og_ref, lbl_ref, gn_ref,
                 o_ref, ofw_scr, obw_scr, st_scr, *, heads_per_step, seq_len):
    t = HGRN_CHUNK
    n_chunks = seq_len // t
    half = t // 2
    f_refs = (ffw_ref, fbw_ref)

    lbl = lbl_ref[...]
    c0s, c1s = [], []
    for d in range(2):
        l0 = lbl[2 * d:2 * d + 1, :]
        l1 = lbl[2 * d + 1:2 * d + 2, :]
        mx = jnp.maximum(l0, l1)
        e0 = jnp.exp(l0 - mx)
        lb = e0 / (e0 + jnp.exp(l1 - mx))
        c0s.append(0.5 * (1.0 + lb))
        c1s.append(0.5 * (1.0 - lb))

    st_scr[...] = jnp.zeros_like(st_scr)

    row = lax.broadcasted_iota(jnp.int32, (t, t), 0)
    col = lax.broadcasted_iota(jnp.int32, (t, t), 1)
    masks = (col <= row, col >= row)
    row2 = lax.broadcasted_iota(jnp.int32, (t, 2 * t), 0)
    col2 = lax.broadcasted_iota(jnp.int32, (t, 2 * t), 1) & (t - 1)
    tris = (jnp.where(col2 <= row2, 1.0, 0.0).astype(BF16), jnp.where(col2 >= row2, 1.0, 0.0).astype(BF16))

    def stage0(c, h, d):
        rows = pl.ds(pl.multiple_of(c * t, t), t)
        hs = slice(h * A_DIM, (h + 1) * A_DIM)
        ct = c1s[d][:, hs] * jnp.tanh(0.5 * f_refs[d][rows, hs].astype(F32))
        k = c1s[d][:, hs] - ct
        lf = jnp.log(c0s[d][:, hs] + ct)
        hi = lf.astype(BF16)
        lo = (lf - hi.astype(F32)).astype(BF16)
        return h, d, rows, hs, k, _dot(tris[d], jnp.concatenate([hi, lo], axis=0))

    def stage1(h, d, rows, hs, k, b):
        b = b * LOG2E
        q = q_ref[rows, hs].astype(F32)
        if d == 0:
            b_end = b[t - 1:t, :]
            b_mid = b[half - 1:half, :]
        else:
            b_end = b[0:1, :]
            b_mid = b[half:half + 1, :]
        qd = (q * jnp.exp2(b)).astype(BF16)
        ke = (k * jnp.exp2(b_end - b)).astype(BF16)
        qm = (q * jnp.exp2(b - b_mid)).astype(BF16)
        km = (k * jnp.exp2(b_mid - b)).astype(BF16)
        return h, d, rows, hs, _dot_nt(qm, km), qd, ke, jnp.exp2(b_end)

    def stage2(finalize, h, d, rows, hs, att, qd, ke, decay):
        v = v_ref[rows, hs]
        att = jnp.where(masks[d], att, 0.0).astype(BF16)
        idx = 2 * h + d
        s_t = st_scr[idx]
        o = _dot(att, v) + _dot_nt(qd, s_t.astype(BF16))
        st_scr[idx] = s_t * decay + _dot_tn(v, ke)
        if finalize:
            other = obw_scr if d == 0 else ofw_scr
            y = _rms(o + other[rows, hs]) * gn_ref[...]
            o_ref[rows, hs] = (y * og_ref[rows, hs].astype(F32)).astype(BF16)
        else:
            o_scr = ofw_scr if d == 0 else obw_scr
            o_scr[rows, hs] = o

    units = [(h, d) for h in range(heads_per_step) for d in range(2)]

    def half_sweep(c0, finalize):
        after0, after1 = {}, {}
        n_half = n_chunks // 2
        for step in range(n_half + 2):
            if step < n_half:
                c = c0 + step
                after0[step] = [stage0(c if d == 0 else n_chunks - 1 - c, h, d) for h, d in units]
            if 1 <= step <= n_half:
                after1[step - 1] = [stage1(*vals) for vals in after0.pop(step - 1)]
            if step >= 2:
                for vals in after1.pop(step - 2):
                    stage2(finalize, *vals)

    half_sweep(0, False)
    half_sweep(n_chunks // 2, True)


def _hgrn(p_lin, p_og, lb_logits4, g_norm, batch, seq_len):
    hps = 2
    wblk = hps * A_DIM
    nblk = A_WIDTH // wblk

    def seg_spec(seg):
        return pl.BlockSpec((seq_len, wblk), lambda b, p: (b, seg * nblk + p))

    est = 2 * 6 * seq_len * wblk * 2 + 2 * seq_len * wblk * 4 + (6 << 20)
    return pl.pallas_call(
        functools.partial(_hgrn_kernel, heads_per_step=hps, seq_len=seq_len),
        grid=(batch, nblk),
        in_specs=[
            seg_spec(0), seg_spec(1), seg_spec(2), seg_spec(3),
            seg_spec(0),
            pl.BlockSpec((4, wblk), lambda b, p: (0, p)),
            pl.BlockSpec((1, A_DIM), lambda b, p: (0, 0)),
        ],
        out_specs=pl.BlockSpec((seq_len, wblk), lambda b, p: (b, p)),
        out_shape=jax.ShapeDtypeStruct((batch * seq_len, A_WIDTH), BF16),
        scratch_shapes=[
            pltpu.VMEM((seq_len, wblk), F32),
            pltpu.VMEM((seq_len, wblk), F32),
            pltpu.VMEM((2 * hps, A_DIM, A_DIM), F32),
        ],
        compiler_params=pltpu.CompilerParams(
            dimension_semantics=("arbitrary", "arbitrary"),
            vmem_limit_bytes=_vmem_limit(est)),
        name="hgrn",
    )(p_lin, p_lin, p_lin, p_lin, p_og, lb_logits4, g_norm.reshape(1, A_DIM))


def _sgu_kernel(u_ref, v_ref, g_ref, ws_ref, bs_ref, o_ref, *, chunks_per_step):
    v = v_ref[...].astype(F32)
    mu = jnp.mean(v, axis=-1, keepdims=True)
    dlt = v - mu
    y = (dlt * lax.rsqrt(jnp.mean(dlt * dlt, axis=-1, keepdims=True) + EPS) * g_ref[...]).astype(BF16)
    for n in range(chunks_per_step):
        rs = slice(n * B_CHUNK, (n + 1) * B_CHUNK)
        for g in range(B_GROUPS):
            cs = slice(g * 128, (g + 1) * 128)
            vm = _dot(ws_ref[g].astype(BF16), y[rs, cs]) + bs_ref[:, cs]
            o_ref[rs, cs] = (u_ref[rs, cs].astype(F32) * vm).astype(BF16)


def _sgu(p_z, g_v, w_s, bias_full):
    m = p_z.shape[0]
    cps = 4
    tm = cps * B_CHUNK
    return pl.pallas_call(
        functools.partial(_sgu_kernel, chunks_per_step=cps),
        grid=(m // tm,),
        in_specs=[
            pl.BlockSpec((tm, B_WIDTH), lambda i: (i, 0)),
            pl.BlockSpec((tm, B_WIDTH), lambda i: (i, 1)),
            pl.BlockSpec((1, B_WIDTH), lambda i: (0, 0)),
            pl.BlockSpec((B_GROUPS, B_CHUNK, B_CHUNK), lambda i: (0, 0, 0)),
            pl.BlockSpec((B_CHUNK, B_WIDTH), lambda i: (0, 0)),
        ],
        out_specs=pl.BlockSpec((tm, B_WIDTH), lambda i: (i, 0)),
        out_shape=jax.ShapeDtypeStruct((m, B_WIDTH), BF16),
        compiler_params=pltpu.CompilerParams(
            dimension_semantics=("arbitrary",),
            vmem_limit_bytes=_vmem_limit(32 << 20)),
        name="sgu",
    )(p_z, p_z, g_v.reshape(1, B_WIDTH), w_s, bias_full)


def _mix_kernel(hg_ref, sg_ref, ga_ref, gb_ref, x_ref, mod_ref, gpost_ref, gnext_ref,
                wa_ref, wb_ref, wo_ref, h_ref, a_ref, *, sub):
    gt1 = mod_ref[0, 0:1, :]
    sh2 = mod_ref[0, 1:2, :]
    sc2 = mod_ref[0, 2:3, :]
    def branches(rs):
        return rs, _dot(hg_ref[rs, :], wa_ref[...]), _dot(sg_ref[rs, :], wb_ref[...])

    def project(rs, ya, yb):
        merged = ga_ref[rs, :].astype(F32) * ya + gb_ref[rs, :].astype(F32) * yb
        return rs, _dot(merged.astype(BF16), wo_ref[...])

    def finish(rs, mo):
        h = x_ref[rs, :] + gt1 * (_rms(mo) * gpost_ref[...])
        h_ref[rs, :] = h
        a_ref[rs, :] = (_rms(h) * gnext_ref[...] * (1.0 + sc2) + sh2).astype(BF16)

    bounds = [0]
    for size in sub:
        bounds.append(bounds[-1] + size)
    assert bounds[-1] == hg_ref.shape[0]
    n_sub = len(sub)
    after0, after1 = {}, {}
    for step in range(n_sub + 2):
        if step < n_sub:
            after0[step] = branches(slice(bounds[step], bounds[step + 1]))
        if 1 <= step <= n_sub:
            after1[step - 1] = project(*after0.pop(step - 1))
        if step >= 2:
            finish(*after1.pop(step - 2))


def _mix(hg, sg, p_gate, x2, mod3, g_post, g_next, wa, wb, wo, rows_per_batch):
    m, d = x2.shape
    tm, sub = 512, (384, 128)
    bpb = rows_per_batch // tm
    const = lambda shape: pl.BlockSpec(shape, lambda i: (0, 0), pipeline_mode=pl.Buffered(1))
    row = lambda width, col: pl.BlockSpec((tm, width), lambda i: (i, col))
    vec_spec = pl.BlockSpec((1, d), lambda i: (0, 0))
    est = (2 * A_WIDTH * d + d * d) * 2 + 2 * tm * (2 * A_WIDTH * 2 + 2 * d * 2 + d * 4 + d * 4 + d * 2) \
        + 6 * tm * d * 4 + (4 << 20)
    return pl.pallas_call(
        functools.partial(_mix_kernel, sub=sub),
        grid=(m // tm,),
        in_specs=[
            row(A_WIDTH, 0), row(B_WIDTH, 0), row(d, 0), row(d, 1), row(d, 0),
            pl.BlockSpec((1, mod3.shape[1], d), lambda i: (i // bpb, 0, 0)),
            vec_spec, vec_spec,
            const((A_WIDTH, d)), const((B_WIDTH, d)), const((d, d)),
        ],
        out_specs=[row(d, 0), row(d, 0)],
        out_shape=[jax.ShapeDtypeStruct((m, d), F32), jax.ShapeDtypeStruct((m, d), BF16)],
        compiler_params=pltpu.CompilerParams(
            dimension_semantics=("arbitrary",),
            vmem_limit_bytes=_vmem_limit(est)),
        name="mix",
    )(hg, sg, p_gate, p_gate, x2, mod3, g_post.reshape(1, d), g_next.reshape(1, d), wa, wb, wo)


def _resproj_kernel(lhs_ref, w_ref, res_ref, mod_ref, gpost_ref, h_ref, *, gate_row, sub):
    k = pl.program_id(1)
    last = pl.num_programs(1) - 1

    @pl.when(k == 0)
    def _():
        h_ref[...] = _dot(lhs_ref[...], w_ref[...].astype(BF16))

    @pl.when((k > 0) & (k < last))
    def _():
        h_ref[...] += _dot(lhs_ref[...], w_ref[...].astype(BF16))

    @pl.when(k == last)
    def _():
        w = w_ref[...].astype(BF16)
        gate = mod_ref[0, gate_row:gate_row + 1, :]
        for s in range(h_ref.shape[0] // sub):
            rs = slice(s * sub, (s + 1) * sub)
            acc = h_ref[rs, :] + _dot(lhs_ref[rs, :], w)
            h_ref[rs, :] = res_ref[rs, :] + gate * (_rms(acc) * gpost_ref[...])


def _resproj(lhs, w, res, mod3, g_post, gate_row, rows_per_batch, tm, tk, name):
    m, kdim = lhs.shape
    d = w.shape[1]
    bpb = rows_per_batch // tm
    row_spec = pl.BlockSpec((tm, d), lambda i, k: (i, 0))
    est = 2 * tm * tk * 2 + 2 * tk * d * 4 + tk * d * 2 + 4 * tm * d * 4 + 2 * tm * d * 4 + (4 << 20)
    return pl.pallas_call(
        functools.partial(_resproj_kernel, gate_row=gate_row, sub=256),
        grid=(m // tm, kdim // tk),
        in_specs=[
            pl.BlockSpec((tm, tk), lambda i, k: (i, k)),
            pl.BlockSpec((tk, d), lambda i, k: (k, 0)),
            row_spec,
            pl.BlockSpec((1, mod3.shape[1], d), lambda i, k: (i // bpb, 0, 0)),
            pl.BlockSpec((1, d), lambda i, k: (0, 0)),
        ],
        out_specs=row_spec,
        out_shape=jax.ShapeDtypeStruct((m, d), F32),
        compiler_params=pltpu.CompilerParams(
            dimension_semantics=("arbitrary", "arbitrary"),
            vmem_limit_bytes=_vmem_limit(est)),
        name=name,
    )(lhs, w, res, mod3, g_post.reshape(1, d))


def kernel(x, c, w_ada, b_ada, g_pre_mix, g_post_mix, g_pre_ffn, g_post_ffn, w_in, lb_logits,
           g_hgrn_norm, w_a_out, g_sgu_norm, w_spatial, b_spatial, w_b_out, w_o, w_ff1, w_ff2):
    batch, seq_len, d = x.shape
    depth = w_in.shape[0]
    assert depth == 1 and lb_logits.shape == (2, depth + 1, A_WIDTH)
    assert w_in.shape[2] == 11 * SEG and d == 2 * SEG
    assert g_hgrn_norm.shape == (depth, A_DIM) and w_spatial.shape == (depth, B_GROUPS, B_CHUNK, B_CHUNK)
    assert seq_len % 1024 == 0
    m = batch * seq_len

    c8 = jnp.zeros((8, d), F32).at[:batch].set(c)
    b_ada2 = b_ada[0].reshape(1, N_MOD * d)
    n_mix = 2 * d
    mod_mix = _modulation(c8, w_ada[0], b_ada2, n_mix)[:batch].reshape(batch, 2, d)

    x2 = x.reshape(m, d)
    w_in0 = w_in[0]
    a1, p_og = _normproj(x2, mod_mix, g_pre_mix[0], w_in0, SEG_OG, seq_len)
    p_lin, wa, mod_rest = _proj(a1, w_in0, SEG_Q, 4, _act_linear, "inproj_lin", side=w_a_out[0],
                                mod_tail=(c8, w_ada[0], b_ada2, n_mix))
    mod3 = mod_rest[:batch].reshape(batch, N_MOD - 2, d)
    p_z, wb = _proj(a1, w_in0, SEG_Z, 2, _act_gelu, "inproj_gelu", side=w_b_out[0])
    p_gate, wo = _proj(a1, w_in0, SEG_GATE, 4, _act_sigmoid, "inproj_sig", side=w_o[0])

    hg = _hgrn(p_lin, p_og, lb_logits.reshape(2 * (depth + 1), A_WIDTH), g_hgrn_norm[0], batch, seq_len)

    bias_full = jnp.repeat(b_spatial[0].T, 128, axis=1)
    sg = _sgu(p_z, g_sgu_norm[0], w_spatial[0], bias_full)

    h1, a2 = _mix(hg, sg, p_gate, x2, mod3, g_post_mix[0], g_pre_ffn[0], wa, wb, wo, seq_len)
    hid, w2 = _proj(a2, w_ff1[0], 0, w_ff1.shape[2] // SEG, _act_relu2, "ff1", side=w_ff2[0])
    out = _resproj(hid, w2, h1, mod3, g_post_ffn[0], 3, seq_len, tm=1024, tk=1024, name="ff2")
    return out.reshape(batch, seq_len, d)
```

```python
import functools

import jax
import jax.numpy as jnp
from jax import lax
from jax.experimental import pallas as pl
from jax.experimental.pallas import tpu as pltpu

F32 = jnp.float32
BF16 = jnp.bfloat16
EPS = 1e-6
LOG2E = 1.4426950408889634

A_HEADS = 8
A_DIM = 128
A_WIDTH = A_HEADS * A_DIM
B_GROUPS = 8
B_CHUNK = 128
B_WIDTH = B_GROUPS * 128
N_MOD = 6
SEG = 1024
SEG_Q, SEG_F, SEG_V, SEG_OG, SEG_Z, SEG_GATE = 0, 1, 3, 4, 5, 7

HGRN_CHUNK = 64
PROJ_SUB = 128
V7X_VMEM_BYTES = 64 * 1024 * 1024


def _vmem_limit(estimate_bytes):
    return int(min(estimate_bytes, V7X_VMEM_BYTES - 4 * 1024 * 1024))


def _sigmoid(x):
    return 1.0 / (1.0 + jnp.exp(-x))


def _dot(a, b):
    return jnp.dot(a, b, preferred_element_type=F32)


def _dot_nt(a, b):
    return lax.dot_general(a, b, (((1,), (1,)), ((), ())), preferred_element_type=F32)


def _dot_tn(a, b):
    return lax.dot_general(a, b, (((0,), (0,)), ((), ())), preferred_element_type=F32)


def _rms(x):
    return x * lax.rsqrt(jnp.mean(x * x, axis=-1, keepdims=True) + EPS)


def _mod_block(c_ref, w_ref, b_ref):
    c = c_ref[...]
    s = c * _sigmoid(c)
    return _dot(s.astype(BF16), w_ref[...].astype(BF16)) + b_ref[...]


def _mod_kernel(c_ref, w_ref, b_ref, o_ref):
    o_ref[...] = _mod_block(c_ref, w_ref, b_ref)


def _modulation(c8, w_ada, b_ada, n):
    d = w_ada.shape[0]
    tn = 1024
    return pl.pallas_call(
        _mod_kernel,
        grid=(n // tn,),
        in_specs=[
            pl.BlockSpec((8, d), lambda j: (0, 0)),
            pl.BlockSpec((d, tn), lambda j: (0, j)),
            pl.BlockSpec((1, tn), lambda j: (0, j)),
        ],
        out_specs=pl.BlockSpec((8, tn), lambda j: (0, j)),
        out_shape=jax.ShapeDtypeStruct((8, n), F32),
        compiler_params=pltpu.CompilerParams(
            dimension_semantics=("arbitrary",),
            vmem_limit_bytes=_vmem_limit(2 * d * tn * 4 + d * tn * 2 + (8 << 20))),
        name="mod",
    )(c8, w_ada, b_ada)


def _normproj_kernel(x_ref, mod_ref, g_ref, w_ref, a_ref, o_ref, w_scr):
    @pl.when(pl.program_id(0) == 0)
    def _():
        w_scr[...] = w_ref[...].astype(BF16)

    sh = mod_ref[0, 0:1, :]
    sc = mod_ref[0, 1:2, :]
    for s in range(x_ref.shape[0] // PROJ_SUB):
        rs = slice(s * PROJ_SUB, (s + 1) * PROJ_SUB)
        a = (_rms(x_ref[rs, :]) * g_ref[...] * (1.0 + sc) + sh).astype(BF16)
        a_ref[rs, :] = a
        o_ref[rs, :] = _act_silu(_dot(a, w_scr[...]), 0).astype(BF16)


def _normproj(x2, mod3, g, w, seg, rows_per_batch):
    m, d = x2.shape
    tm, tn = 1024, SEG
    bpb = rows_per_batch // tm
    est = 2 * tm * d * 6 + 2 * d * tn * 4 + d * tn * 2 + 2 * tm * tn * 2 + 8 * PROJ_SUB * d * 4 + (6 << 20)
    return pl.pallas_call(
        _normproj_kernel,
        grid=(m // tm,),
        in_specs=[
            pl.BlockSpec((tm, d), lambda i: (i, 0)),
            pl.BlockSpec((1, mod3.shape[1], d), lambda i: (i // bpb, 0, 0)),
            pl.BlockSpec((1, d), lambda i: (0, 0)),
            pl.BlockSpec((d, tn), lambda i: (0, seg)),
        ],
        out_specs=[pl.BlockSpec((tm, d), lambda i: (i, 0)), pl.BlockSpec((tm, tn), lambda i: (i, 0))],
        out_shape=[jax.ShapeDtypeStruct((m, d), BF16), jax.ShapeDtypeStruct((m, tn), BF16)],
        scratch_shapes=[pltpu.VMEM((d, tn), BF16)],
        compiler_params=pltpu.CompilerParams(
            dimension_semantics=("arbitrary",),
            vmem_limit_bytes=_vmem_limit(est)),
        name="norm_silu",
    )(x2, mod3, g.reshape(1, d), w)


def _act_linear(acc, j):
    return acc * jnp.where(j == 0, A_DIM ** -0.5, jnp.where(j == 3, 1.0, 0.5))


def _act_silu(acc, j):
    return acc * _sigmoid(acc)


def _act_gelu(acc, j):
    return 0.5 * acc * (1.0 + lax.erf(acc * (2.0 ** -0.5)))


def _act_sigmoid(acc, j):
    return _sigmoid(acc)


def _act_relu2(acc, j):
    r = jnp.maximum(acc, 0.0)
    return r * r


def _proj_kernel(a_ref, w_ref, *refs, act, has_side, has_mod):
    refs = list(refs)
    side_ref = refs.pop(0) if has_side else None
    mod_in = [refs.pop(0) for _ in range(3)] if has_mod else None
    o_ref = refs.pop(0)
    if has_side:
        refs.pop(0)[...] = side_ref[...].astype(BF16)
    if has_mod:
        refs.pop(0)[...] = _mod_block(*mod_in)
    (w_scr,) = refs
    j = pl.program_id(0)

    @pl.when(pl.program_id(1) == 0)
    def _():
        w_scr[...] = w_ref[...].astype(BF16)

    for s in range(a_ref.shape[0] // PROJ_SUB):
        rs = slice(s * PROJ_SUB, (s + 1) * PROJ_SUB)
        o_ref[rs, :] = act(_dot(a_ref[rs, :], w_scr[...]), j).astype(BF16)


def _proj(a, w, seg0, nseg, act, name, seg_stride=1, tm=2048, side=None, mod_tail=None):
    m, kdim = a.shape
    tn = SEG
    ni = m // tm
    in_specs = [
        pl.BlockSpec((tm, kdim), lambda j, i: (i, 0)),
        pl.BlockSpec((kdim, tn), lambda j, i: (0, seg0 + j * seg_stride)),
    ]
    out_specs = [pl.BlockSpec((tm, tn), lambda j, i: (i, j))]
    out_shape = [jax.ShapeDtypeStruct((m, nseg * tn), BF16)]
    args = [a, w]
    est = 2 * tm * kdim * 2 + 2 * kdim * tn * 4 + kdim * tn * 2 + 2 * tm * tn * 2 + 8 * PROJ_SUB * tn * 4 + (6 << 20)
    if side is not None:
        rows, cols = side.shape
        blk = rows // (nseg * ni)
        assert blk * nseg * ni == rows and blk % 16 == 0
        side_spec = pl.BlockSpec((blk, cols), lambda j, i: (j * ni + i, 0))
        in_specs.append(side_spec)
        out_specs.append(side_spec)
        out_shape.append(jax.ShapeDtypeStruct(side.shape, BF16))
        args.append(side)
        est += 2 * blk * cols * 6
    if mod_tail is not None:
        c8, w_ada, b_ada, col0 = mod_tail
        dm, n_all = w_ada.shape
        mblk = (n_all - col0) // (nseg * ni)
        assert mblk * nseg * ni == n_all - col0 and mblk % 128 == 0 and col0 % mblk == 0
        in_specs += [
            pl.BlockSpec((8, dm), lambda j, i: (0, 0)),
            pl.BlockSpec((dm, mblk), lambda j, i: (0, col0 // mblk + j * ni + i)),
            pl.BlockSpec((1, mblk), lambda j, i: (0, col0 // mblk + j * ni + i)),
        ]
        out_specs.append(pl.BlockSpec((8, mblk), lambda j, i: (0, j * ni + i)))
        out_shape.append(jax.ShapeDtypeStruct((8, n_all - col0), F32))
        args += [c8, w_ada, b_ada]
        est += 2 * dm * mblk * 4 + dm * mblk * 2
    single = side is None and mod_tail is None
    outs = pl.pallas_call(
        functools.partial(_proj_kernel, act=act, has_side=side is not None, has_mod=mod_tail is not None),
        grid=(nseg, ni),
        in_specs=in_specs,
        out_specs=out_specs,
        out_shape=out_shape,
        scratch_shapes=[pltpu.VMEM((kdim, tn), BF16)],
        compiler_params=pltpu.CompilerParams(
            dimension_semantics=("arbitrary", "arbitrary"),
            vmem_limit_bytes=_vmem_limit(est)),
        name=name,
    )(*args)
    return outs[0] if single else outs


def _hgrn_kernel(q_ref, ffw_ref, fbw_ref, v_ref, og_ref, lbl_ref, gn_ref,
                 o_ref, ofw_scr, obw_scr, st_scr, *, heads_per_step, seq_len):
    t = HGRN_CHUNK
    n_chunks = seq_len // t
    half = t // 2
    f_refs = (ffw_ref, fbw_ref)

    lbl = lbl_ref[...]
    c0s, c1s = [], []
    for d in range(2):
        l0 = lbl[2 * d:2 * d + 1, :]
        l1 = lbl[2 * d + 1:2 * d + 2, :]
        mx = jnp.maximum(l0, l1)
        e0 = jnp.exp(l0 - mx)
        lb = e0 / (e0 + jnp.exp(l1 - mx))
        c0s.append(0.5 * (1.0 + lb))
        c1s.append(0.5 * (1.0 - lb))

    st_scr[...] = jnp.zeros_like(st_scr)

    row = lax.broadcasted_iota(jnp.int32, (t, t), 0)
    col = lax.broadcasted_iota(jnp.int32, (t, t), 1)
    masks = (col <= row, col >= row)
    row2 = lax.broadcasted_iota(jnp.int32, (t, 2 * t), 0)
    col2 = lax.broadcasted_iota(jnp.int32, (t, 2 * t), 1) & (t - 1)
    tris = (jnp.where(col2 <= row2, 1.0, 0.0).astype(BF16), jnp.where(col2 >= row2, 1.0, 0.0).astype(BF16))

    def stage0(c, h, d):
        rows = pl.ds(pl.multiple_of(c * t, t), t)
        hs = slice(h * A_DIM, (h + 1) * A_DIM)
        ct = c1s[d][:, hs] * jnp.tanh(f_refs[d][rows, hs].astype(F32))
        k = c1s[d][:, hs] - ct
        lf = jnp.log(c0s[d][:, hs] + ct)
        hi = lf.astype(BF16)
        lo = (lf - hi.astype(F32)).astype(BF16)
        return h, d, rows, hs, k, _dot(tris[d], jnp.concatenate([hi, lo], axis=0))

    def stage1(h, d, rows, hs, k, b):
        b = b * LOG2E
        q = q_ref[rows, hs].astype(F32)
        if d == 0:
            b_end = b[t - 1:t, :]
            b_mid = b[half - 1:half, :]
        else:
            b_end = b[0:1, :]
            b_mid = b[half:half + 1, :]
        qd = (q * jnp.exp2(b)).astype(BF16)
        ke = (k * jnp.exp2(b_end - b)).astype(BF16)
        qm = (q * jnp.exp2(b - b_mid)).astype(BF16)
        km = (k * jnp.exp2(b_mid - b)).astype(BF16)
        return h, d, rows, hs, _dot_nt(qm, km), qd, ke, jnp.exp2(b_end)

    def stage2(finalize, h, d, rows, hs, att, qd, ke, decay):
        v = v_ref[rows, hs]
        att = jnp.where(masks[d], att, 0.0).astype(BF16)
        idx = 2 * h + d
        s_t = st_scr[idx]
        o = _dot(att, v) + _dot_nt(qd, s_t.astype(BF16))
        st_scr[idx] = s_t * decay + _dot_tn(v, ke)
        if finalize:
            other = obw_scr if d == 0 else ofw_scr
            y = _rms(o + other[rows, hs]) * gn_ref[...]
            o_ref[rows, hs] = (y * og_ref[rows, hs].astype(F32)).astype(BF16)
        else:
            o_scr = ofw_scr if d == 0 else obw_scr
            o_scr[rows, hs] = o

    units = [(h, d) for h in range(heads_per_step) for d in range(2)]

    def half_sweep(c0, finalize):
        after0, after1 = {}, {}
        n_half = n_chunks // 2
        for step in range(n_half + 2):
            if step < n_half:
                c = c0 + step
                after0[step] = [stage0(c if d == 0 else n_chunks - 1 - c, h, d) for h, d in units]
            if 1 <= step <= n_half:
                after1[step - 1] = [stage1(*vals) for vals in after0.pop(step - 1)]
            if step >= 2:
                for vals in after1.pop(step - 2):
                    stage2(finalize, *vals)

    half_sweep(0, False)
    half_sweep(n_chunks // 2, True)


def _hgrn(p_lin, p_og, lb_logits4, g_norm, batch, seq_len):
    hps = 2
    wblk = hps * A_DIM
    nblk = A_WIDTH // wblk

    def seg_spec(seg):
        return pl.BlockSpec((seq_len, wblk), lambda b, p: (b, seg * nblk + p))

    est = 2 * 6 * seq_len * wblk * 2 + 2 * seq_len * wblk * 4 + (6 << 20)
    return pl.pallas_call(
        functools.partial(_hgrn_kernel, heads_per_step=hps, seq_len=seq_len),
        grid=(batch, nblk),
        in_specs=[
            seg_spec(0), seg_spec(1), seg_spec(2), seg_spec(3),
            seg_spec(0),
            pl.BlockSpec((4, wblk), lambda b, p: (0, p)),
            pl.BlockSpec((1, A_DIM), lambda b, p: (0, 0)),
        ],
        out_specs=pl.BlockSpec((seq_len, wblk), lambda b, p: (b, p)),
        out_shape=jax.ShapeDtypeStruct((batch * seq_len, A_WIDTH), BF16),
        scratch_shapes=[
            pltpu.VMEM((seq_len, wblk), F32),
            pltpu.VMEM((seq_len, wblk), F32),
            pltpu.VMEM((2 * hps, A_DIM, A_DIM), F32),
        ],
        compiler_params=pltpu.CompilerParams(
            dimension_semantics=("arbitrary", "arbitrary"),
            vmem_limit_bytes=_vmem_limit(est)),
        name="hgrn",
    )(p_lin, p_lin, p_lin, p_lin, p_og, lb_logits4, g_norm.reshape(1, A_DIM))


def _sgu_kernel(u_ref, v_ref, g_ref, ws_ref, bs_ref, o_ref, *, chunks_per_step):
    v = v_ref[...].astype(F32)
    mu = jnp.mean(v, axis=-1, keepdims=True)
    dlt = v - mu
    y = (dlt * lax.rsqrt(jnp.mean(dlt * dlt, axis=-1, keepdims=True) + EPS) * g_ref[...]).astype(BF16)
    for n in range(chunks_per_step):
        rs = slice(n * B_CHUNK, (n + 1) * B_CHUNK)
        for g in range(B_GROUPS):
            cs = slice(g * 128, (g + 1) * 128)
            vm = _dot(ws_ref[g].astype(BF16), y[rs, cs]) + bs_ref[:, cs]
            o_ref[rs, cs] = (u_ref[rs, cs].astype(F32) * vm).astype(BF16)


def _sgu(p_z, g_v, w_s, bias_full):
    m = p_z.shape[0]
    cps = 4
    tm = cps * B_CHUNK
    return pl.pallas_call(
        functools.partial(_sgu_kernel, chunks_per_step=cps),
        grid=(m // tm,),
        in_specs=[
            pl.BlockSpec((tm, B_WIDTH), lambda i: (i, 0)),
            pl.BlockSpec((tm, B_WIDTH), lambda i: (i, 1)),
            pl.BlockSpec((1, B_WIDTH), lambda i: (0, 0)),
            pl.BlockSpec((B_GROUPS, B_CHUNK, B_CHUNK), lambda i: (0, 0, 0)),
            pl.BlockSpec((B_CHUNK, B_WIDTH), lambda i: (0, 0)),
        ],
        out_specs=pl.BlockSpec((tm, B_WIDTH), lambda i: (i, 0)),
        out_shape=jax.ShapeDtypeStruct((m, B_WIDTH), BF16),
        compiler_params=pltpu.CompilerParams(
            dimension_semantics=("arbitrary",),
            vmem_limit_bytes=_vmem_limit(V7X_VMEM_BYTES)),
        name="sgu",
    )(p_z, p_z, g_v.reshape(1, B_WIDTH), w_s, bias_full)


def _mix_kernel(hg_ref, sg_ref, ga_ref, gb_ref, x_ref, mod_ref, gpost_ref, gnext_ref,
                wa_ref, wb_ref, wo_ref, h_ref, a_ref, *, sub):
    gt1 = mod_ref[0, 0:1, :]
    sh2 = mod_ref[0, 1:2, :]
    sc2 = mod_ref[0, 2:3, :]
    def branches(rs):
        return rs, _dot(hg_ref[rs, :], wa_ref[...]), _dot(sg_ref[rs, :], wb_ref[...])

    def project(rs, ya, yb):
        merged = ga_ref[rs, :].astype(F32) * ya + gb_ref[rs, :].astype(F32) * yb
        return rs, _dot(merged.astype(BF16), wo_ref[...])

    def finish(rs, mo):
        h = x_ref[rs, :] + gt1 * (_rms(mo) * gpost_ref[...])
        h_ref[rs, :] = h
        a_ref[rs, :] = (_rms(h) * gnext_ref[...] * (1.0 + sc2) + sh2).astype(BF16)

    bounds = [0]
    for size in sub:
        bounds.append(bounds[-1] + size)
    assert bounds[-1] == hg_ref.shape[0]
    n_sub = len(sub)
    after0, after1 = {}, {}
    for step in range(n_sub + 2):
        if step < n_sub:
            after0[step] = branches(slice(bounds[step], bounds[step + 1]))
        if 1 <= step <= n_sub:
            after1[step - 1] = project(*after0.pop(step - 1))
        if step >= 2:
            finish(*after1.pop(step - 2))


def _mix(hg, sg, p_gate, x2, mod3, g_post, g_next, wa, wb, wo, rows_per_batch):
    m, d = x2.shape
    tm, sub = 512, (384, 128)
    bpb = rows_per_batch // tm
    const = lambda shape: pl.BlockSpec(shape, lambda i: (0, 0), pipeline_mode=pl.Buffered(1))
    row = lambda width, col: pl.BlockSpec((tm, width), lambda i: (i, col))
    vec_spec = pl.BlockSpec((1, d), lambda i: (0, 0))
    est = (2 * A_WIDTH * d + d * d) * 2 + 2 * tm * (2 * A_WIDTH * 2 + 2 * d * 2 + d * 4 + d * 4 + d * 2) \
        + 6 * tm * d * 4 + (4 << 20)
    return pl.pallas_call(
        functools.partial(_mix_kernel, sub=sub),
        grid=(m // tm,),
        in_specs=[
            row(A_WIDTH, 0), row(B_WIDTH, 0), row(d, 0), row(d, 1), row(d, 0),
            pl.BlockSpec((1, mod3.shape[1], d), lambda i: (i // bpb, 0, 0)),
            vec_spec, vec_spec,
            const((A_WIDTH, d)), const((B_WIDTH, d)), const((d, d)),
        ],
        out_specs=[row(d, 0), row(d, 0)],
        out_shape=[jax.ShapeDtypeStruct((m, d), F32), jax.ShapeDtypeStruct((m, d), BF16)],
        compiler_params=pltpu.CompilerParams(
            dimension_semantics=("arbitrary",),
            vmem_limit_bytes=_vmem_limit(est)),
        name="mix",
    )(hg, sg, p_gate, p_gate, x2, mod3, g_post.reshape(1, d), g_next.reshape(1, d), wa, wb, wo)


def _resproj_kernel(lhs_ref, w_ref, res_ref, mod_ref, gpost_ref, h_ref, *, gate_row, sub):
    k = pl.program_id(1)
    last = pl.num_programs(1) - 1

    @pl.when(k == 0)
    def _():
        h_ref[...] = _dot(lhs_ref[...], w_ref[...].astype(BF16))

    @pl.when((k > 0) & (k < last))
    def _():
        h_ref[...] += _dot(lhs_ref[...], w_ref[...].astype(BF16))

    @pl.when(k == last)
    def _():
        w = w_ref[...].astype(BF16)
        gate = mod_ref[0, gate_row:gate_row + 1, :]
        for s in range(h_ref.shape[0] // sub):
            rs = slice(s * sub, (s + 1) * sub)
            acc = h_ref[rs, :] + _dot(lhs_ref[rs, :], w)
            h_ref[rs, :] = res_ref[rs, :] + gate * (_rms(acc) * gpost_ref[...])


def _resproj(lhs, w, res, mod3, g_post, gate_row, rows_per_batch, tm, tk, name):
    m, kdim = lhs.shape
    d = w.shape[1]
    bpb = rows_per_batch // tm
    row_spec = pl.BlockSpec((tm, d), lambda i, k: (i, 0))
    est = 2 * tm * tk * 2 + 2 * tk * d * 4 + tk * d * 2 + 4 * tm * d * 4 + 2 * tm * d * 4 + (4 << 20)
    return pl.pallas_call(
        functools.partial(_resproj_kernel, gate_row=gate_row, sub=256),
        grid=(m // tm, kdim // tk),
        in_specs=[
            pl.BlockSpec((tm, tk), lambda i, k: (i, k)),
            pl.BlockSpec((tk, d), lambda i, k: (k, 0)),
            row_spec,
            pl.BlockSpec((1, mod3.shape[1], d), lambda i, k: (i // bpb, 0, 0)),
            pl.BlockSpec((1, d), lambda i, k: (0, 0)),
        ],
        out_specs=row_spec,
        out_shape=jax.ShapeDtypeStruct((m, d), F32),
        compiler_params=pltpu.CompilerParams(
            dimension_semantics=("arbitrary", "arbitrary"),
            vmem_limit_bytes=_vmem_limit(est)),
        name=name,
    )(lhs, w, res, mod3, g_post.reshape(1, d))


def kernel(x, c, w_ada, b_ada, g_pre_mix, g_post_mix, g_pre_ffn, g_post_ffn, w_in, lb_logits,
           g_hgrn_norm, w_a_out, g_sgu_norm, w_spatial, b_spatial, w_b_out, w_o, w_ff1, w_ff2):
    batch, seq_len, d = x.shape
    depth = w_in.shape[0]
    assert depth == 1 and lb_logits.shape == (2, depth + 1, A_WIDTH)
    assert w_in.shape[2] == 11 * SEG and d == 2 * SEG
    assert g_hgrn_norm.shape == (depth, A_DIM) and w_spatial.shape == (depth, B_GROUPS, B_CHUNK, B_CHUNK)
    assert seq_len % 1024 == 0
    m = batch * seq_len

    c8 = jnp.zeros((8, d), F32).at[:batch].set(c)
    b_ada2 = b_ada[0].reshape(1, N_MOD * d)
    n_mix = 2 * d
    mod_mix = _modulation(c8, w_ada[0], b_ada2, n_mix)[:batch].reshape(batch, 2, d)

    x2 = x.reshape(m, d)
    w_in0 = w_in[0]
    a1, p_og = _normproj(x2, mod_mix, g_pre_mix[0], w_in0, SEG_OG, seq_len)
    p_lin, wa, mod_rest = _proj(a1, w_in0, SEG_Q, 4, _act_linear, "inproj_lin", side=w_a_out[0],
                                mod_tail=(c8, w_ada[0], b_ada2, n_mix))
    mod3 = mod_rest[:batch].reshape(batch, N_MOD - 2, d)
    p_z, wb = _proj(a1, w_in0, SEG_Z, 2, _act_gelu, "inproj_gelu", side=w_b_out[0])
    p_gate, wo = _proj(a1, w_in0, SEG_GATE, 4, _act_sigmoid, "inproj_sig", side=w_o[0])

    hg = _hgrn(p_lin, p_og, lb_logits.reshape(2 * (depth + 1), A_WIDTH), g_hgrn_norm[0], batch, seq_len)

    bias_full = jnp.repeat(b_spatial[0].T, 128, axis=1)
    sg = _sgu(p_z, g_sgu_norm[0], w_spatial[0], bias_full)

    h1, a2 = _mix(hg, sg, p_gate, x2, mod3, g_post_mix[0], g_pre_ffn[0], wa, wb, wo, seq_len)
    hid, w2 = _proj(a2, w_ff1[0], 0, w_ff1.shape[2] // SEG, _act_relu2, "ff1", side=w_ff2[0])
    out = _resproj(hid, w2, h1, mod3, g_post_ffn[0], 3, seq_len, tm=1024, tk=1024, name="ff2")
    return out.reshape(batch, seq_len, d)
```

```python
import functools

import jax
import jax.numpy as jnp
from jax import lax
from jax.experimental import pallas as pl
from jax.experimental.pallas import tpu as pltpu

F32 = jnp.float32
BF16 = jnp.bfloat16
EPS = 1e-6
LOG2E = 1.4426950408889634

A_HEADS = 8
A_DIM = 128
A_WIDTH = A_HEADS * A_DIM
B_GROUPS = 8
B_CHUNK = 128
B_WIDTH = B_GROUPS * 128
N_MOD = 6
SEG = 1024
SEG_Q, SEG_F, SEG_V, SEG_OG, SEG_Z, SEG_GATE = 0, 1, 3, 4, 5, 7

HGRN_CHUNK = 64
PROJ_SUB = 128
V7X_VMEM_BYTES = 64 * 1024 * 1024


def _vmem_limit(estimate_bytes):
    return int(min(estimate_bytes, V7X_VMEM_BYTES - 4 * 1024 * 1024))


def _sigmoid(x):
    return 1.0 / (1.0 + jnp.exp(-x))


def _dot(a, b):
    return jnp.dot(a, b, preferred_element_type=F32)


def _dot_nt(a, b):
    return lax.dot_general(a, b, (((1,), (1,)), ((), ())), preferred_element_type=F32)


def _dot_tn(a, b):
    return lax.dot_general(a, b, (((0,), (0,)), ((), ())), preferred_element_type=F32)


def _rms(x):
    return x * lax.rsqrt(jnp.mean(x * x, axis=-1, keepdims=True) + EPS)


def _mod_block(c_ref, w_ref, b_ref):
    c = c_ref[...]
    s = c * _sigmoid(c)
    return _dot(s.astype(BF16), w_ref[...].astype(BF16)) + b_ref[...]


def _mod_kernel(c_ref, w_ref, b_ref, o_ref):
    o_ref[...] = _mod_block(c_ref, w_ref, b_ref)


def _modulation(c8, w_ada, b_ada, n):
    d = w_ada.shape[0]
    tn = 1024
    return pl.pallas_call(
        _mod_kernel,
        grid=(n // tn,),
        in_specs=[
            pl.BlockSpec((8, d), lambda j: (0, 0)),
            pl.BlockSpec((d, tn), lambda j: (0, j)),
            pl.BlockSpec((1, tn), lambda j: (0, j)),
        ],
        out_specs=pl.BlockSpec((8, tn), lambda j: (0, j)),
        out_shape=jax.ShapeDtypeStruct((8, n), F32),
        compiler_params=pltpu.CompilerParams(
            dimension_semantics=("arbitrary",),
            vmem_limit_bytes=_vmem_limit(2 * d * tn * 4 + d * tn * 2 + (8 << 20))),
        name="mod",
    )(c8, w_ada, b_ada)


def _normproj_kernel(x_ref, mod_ref, g_ref, w_ref, a_ref, o_ref, w_scr):
    @pl.when(pl.program_id(0) == 0)
    def _():
        w_scr[...] = w_ref[...].astype(BF16)

    sh = mod_ref[0, 0:1, :]
    sc = mod_ref[0, 1:2, :]
    for s in range(x_ref.shape[0] // PROJ_SUB):
        rs = slice(s * PROJ_SUB, (s + 1) * PROJ_SUB)
        a = (_rms(x_ref[rs, :]) * g_ref[...] * (1.0 + sc) + sh).astype(BF16)
        a_ref[rs, :] = a
        o_ref[rs, :] = _act_silu(_dot(a, w_scr[...]), 0).astype(BF16)


def _normproj(x2, mod3, g, w, seg, rows_per_batch):
    m, d = x2.shape
    tm, tn = 1024, SEG
    bpb = rows_per_batch // tm
    est = 2 * tm * d * 6 + 2 * d * tn * 4 + d * tn * 2 + 2 * tm * tn * 2 + 8 * PROJ_SUB * d * 4 + (6 << 20)
    return pl.pallas_call(
        _normproj_kernel,
        grid=(m // tm,),
        in_specs=[
            pl.BlockSpec((tm, d), lambda i: (i, 0)),
            pl.BlockSpec((1, mod3.shape[1], d), lambda i: (i // bpb, 0, 0)),
            pl.BlockSpec((1, d), lambda i: (0, 0)),
            pl.BlockSpec((d, tn), lambda i: (0, seg)),
        ],
        out_specs=[pl.BlockSpec((tm, d), lambda i: (i, 0)), pl.BlockSpec((tm, tn), lambda i: (i, 0))],
        out_shape=[jax.ShapeDtypeStruct((m, d), BF16), jax.ShapeDtypeStruct((m, tn), BF16)],
        scratch_shapes=[pltpu.VMEM((d, tn), BF16)],
        compiler_params=pltpu.CompilerParams(
            dimension_semantics=("arbitrary",),
            vmem_limit_bytes=_vmem_limit(est)),
        name="norm_silu",
    )(x2, mod3, g.reshape(1, d), w)


def _act_linear(acc, j):
    return acc * jnp.where(j == 0, A_DIM ** -0.5, 1.0)


def _act_silu(acc, j):
    return acc * _sigmoid(acc)


def _act_gelu(acc, j):
    return 0.5 * acc * (1.0 + lax.erf(acc * (2.0 ** -0.5)))


def _act_sigmoid(acc, j):
    return _sigmoid(acc)


def _act_relu2(acc, j):
    r = jnp.maximum(acc, 0.0)
    return r * r


def _proj_kernel(a_ref, w_ref, *refs, act, has_side, has_mod):
    refs = list(refs)
    side_ref = refs.pop(0) if has_side else None
    mod_in = [refs.pop(0) for _ in range(3)] if has_mod else None
    o_ref = refs.pop(0)
    if has_side:
        refs.pop(0)[...] = side_ref[...].astype(BF16)
    if has_mod:
        refs.pop(0)[...] = _mod_block(*mod_in)
    (w_scr,) = refs
    j = pl.program_id(0)

    @pl.when(pl.program_id(1) == 0)
    def _():
        w_scr[...] = w_ref[...].astype(BF16)

    for s in range(a_ref.shape[0] // PROJ_SUB):
        rs = slice(s * PROJ_SUB, (s + 1) * PROJ_SUB)
        o_ref[rs, :] = act(_dot(a_ref[rs, :], w_scr[...]), j).astype(BF16)


def _proj(a, w, seg0, nseg, act, name, seg_stride=1, tm=2048, side=None, mod_tail=None):
    m, kdim = a.shape
    tn = SEG
    ni = m // tm
    in_specs = [
        pl.BlockSpec((tm, kdim), lambda j, i: (i, 0)),
        pl.BlockSpec((kdim, tn), lambda j, i: (0, seg0 + j * seg_stride)),
    ]
    out_specs = [pl.BlockSpec((tm, tn), lambda j, i: (i, j))]
    out_shape = [jax.ShapeDtypeStruct((m, nseg * tn), BF16)]
    args = [a, w]
    est = 2 * tm * kdim * 2 + 2 * kdim * tn * 4 + kdim * tn * 2 + 2 * tm * tn * 2 + 8 * PROJ_SUB * tn * 4 + (6 << 20)
    if side is not None:
        rows, cols = side.shape
        blk = rows // (nseg * ni)
        assert blk * nseg * ni == rows and blk % 16 == 0
        side_spec = pl.BlockSpec((blk, cols), lambda j, i: (j * ni + i, 0))
        in_specs.append(side_spec)
        out_specs.append(side_spec)
        out_shape.append(jax.ShapeDtypeStruct(side.shape, BF16))
        args.append(side)
        est += 2 * blk * cols * 6
    if mod_tail is not None:
        c8, w_ada, b_ada, col0 = mod_tail
        dm, n_all = w_ada.shape
        mblk = (n_all - col0) // (nseg * ni)
        assert mblk * nseg * ni == n_all - col0 and mblk % 128 == 0 and col0 % mblk == 0
        in_specs += [
            pl.BlockSpec((8, dm), lambda j, i: (0, 0)),
            pl.BlockSpec((dm, mblk), lambda j, i: (0, col0 // mblk + j * ni + i)),
            pl.BlockSpec((1, mblk), lambda j, i: (0, col0 // mblk + j * ni + i)),
        ]
        out_specs.append(pl.BlockSpec((8, mblk), lambda j, i: (0, j * ni + i)))
        out_shape.append(jax.ShapeDtypeStruct((8, n_all - col0), F32))
        args += [c8, w_ada, b_ada]
        est += 2 * dm * mblk * 4 + dm * mblk * 2
    single = side is None and mod_tail is None
    outs = pl.pallas_call(
        functools.partial(_proj_kernel, act=act, has_side=side is not None, has_mod=mod_tail is not None),
        grid=(nseg, ni),
        in_specs=in_specs,
        out_specs=out_specs,
        out_shape=out_shape,
        scratch_shapes=[pltpu.VMEM((kdim, tn), BF16)],
        compiler_params=pltpu.CompilerParams(
            dimension_semantics=("arbitrary", "arbitrary"),
            vmem_limit_bytes=_vmem_limit(est)),
        name=name,
    )(*args)
    return outs[0] if single else outs


def _hgrn_kernel(q_ref, ffw_ref, fbw_ref, v_ref, og_ref, lbl_ref, gn_ref,
                 o_ref, ofw_scr, obw_scr, st_scr, *, heads_per_step, seq_len):
    t = HGRN_CHUNK
    n_chunks = seq_len // t
    half = t // 2
    f_refs = (ffw_ref, fbw_ref)

    lbl = lbl_ref[...]
    c0s, c1s = [], []
    for d in range(2):
        l0 = lbl[2 * d:2 * d + 1, :]
        l1 = lbl[2 * d + 1:2 * d + 2, :]
        mx = jnp.maximum(l0, l1)
        e0 = jnp.exp(l0 - mx)
        lb = e0 / (e0 + jnp.exp(l1 - mx))
        c0s.append(0.5 * (1.0 + lb))
        c1s.append(0.5 * (1.0 - lb))

    st_scr[...] = jnp.zeros_like(st_scr)

    row = lax.broadcasted_iota(jnp.int32, (t, t), 0)
    col = lax.broadcasted_iota(jnp.int32, (t, t), 1)
    masks = (col <= row, col >= row)
    row2 = lax.broadcasted_iota(jnp.int32, (t, 2 * t), 0)
    col2 = lax.broadcasted_iota(jnp.int32, (t, 2 * t), 1) & (t - 1)
    tris = (jnp.where(col2 <= row2, 1.0, 0.0).astype(BF16), jnp.where(col2 >= row2, 1.0, 0.0).astype(BF16))

    def stage0(c, h, d):
        rows = pl.ds(pl.multiple_of(c * t, t), t)
        hs = slice(h * A_DIM, (h + 1) * A_DIM)
        ct = c1s[d][:, hs] * jnp.tanh(0.5 * f_refs[d][rows, hs].astype(F32))
        k = c1s[d][:, hs] - ct
        lf = jnp.log(c0s[d][:, hs] + ct)
        hi = lf.astype(BF16)
        lo = (lf - hi.astype(F32)).astype(BF16)
        return h, d, rows, hs, k, _dot(tris[d], jnp.concatenate([hi, lo], axis=0))

    def stage1(h, d, rows, hs, k, b):
        b = b * LOG2E
        q = q_ref[rows, hs].astype(F32)
        if d == 0:
            b_end = b[t - 1:t, :]
            b_mid = b[half - 1:half, :]
        else:
            b_end = b[0:1, :]
            b_mid = b[half:half + 1, :]
        qd = (q * jnp.exp2(b)).astype(BF16)
        ke = (k * jnp.exp2(b_end - b)).astype(BF16)
        qm = (q * jnp.exp2(b - b_mid)).astype(BF16)
        km = (k * jnp.exp2(b_mid - b)).astype(BF16)
        return h, d, rows, hs, _dot_nt(qm, km), qd, ke, jnp.exp2(b_end)

    def stage2(finalize, h, d, rows, hs, att, qd, ke, decay):
        v = v_ref[rows, hs]
        att = jnp.where(masks[d], att, 0.0).astype(BF16)
        idx = 2 * h + d
        s_t = st_scr[idx]
        o = _dot(att, v) + _dot_nt(qd, s_t.astype(BF16))
        st_scr[idx] = s_t * decay + _dot_tn(v, ke)
        if finalize:
            other = obw_scr if d == 0 else ofw_scr
            y = _rms(o + other[rows, hs]) * gn_ref[...]
            o_ref[rows, hs] = (y * og_ref[rows, hs].astype(F32)).astype(BF16)
        else:
            o_scr = ofw_scr if d == 0 else obw_scr
            o_scr[rows, hs] = o

    units = [(h, d) for h in range(heads_per_step) for d in range(2)]

    def half_sweep(c0, finalize):
        after0, after1 = {}, {}
        n_half = n_chunks // 2
        for step in range(n_half + 2):
            if step < n_half:
                c = c0 + step
                after0[step] = [stage0(c if d == 0 else n_chunks - 1 - c, h, d) for h, d in units]
            if 1 <= step <= n_half:
                after1[step - 1] = [stage1(*vals) for vals in after0.pop(step - 1)]
            if step >= 2:
                for vals in after1.pop(step - 2):
                    stage2(finalize, *vals)

    half_sweep(0, False)
    half_sweep(n_chunks // 2, True)


def _hgrn(p_lin, p_og, lb_logits4, g_norm, batch, seq_len):
    hps = 2
    wblk = hps * A_DIM
    nblk = A_WIDTH // wblk

    def seg_spec(seg):
        return pl.BlockSpec((seq_len, wblk), lambda b, p: (b, seg * nblk + p))

    est = 2 * 6 * seq_len * wblk * 2 + 2 * seq_len * wblk * 4 + (6 << 20)
    return pl.pallas_call(
        functools.partial(_hgrn_kernel, heads_per_step=hps, seq_len=seq_len),
        grid=(batch, nblk),
        in_specs=[
            seg_spec(0), seg_spec(1), seg_spec(2), seg_spec(3),
            seg_spec(0),
            pl.BlockSpec((4, wblk), lambda b, p: (0, p)),
            pl.BlockSpec((1, A_DIM), lambda b, p: (0, 0)),
        ],
        out_specs=pl.BlockSpec((seq_len, wblk), lambda b, p: (b, p)),
        out_shape=jax.ShapeDtypeStruct((batch * seq_len, A_WIDTH), BF16),
        scratch_shapes=[
            pltpu.VMEM((seq_len, wblk), F32),
            pltpu.VMEM((seq_len, wblk), F32),
            pltpu.VMEM((2 * hps, A_DIM, A_DIM), F32),
        ],
        compiler_params=pltpu.CompilerParams(
            dimension_semantics=("arbitrary", "arbitrary"),
            vmem_limit_bytes=_vmem_limit(est)),
        name="hgrn",
    )(p_lin, p_lin, p_lin, p_lin, p_og, lb_logits4, g_norm.reshape(1, A_DIM))


def _sgu_kernel(u_ref, v_ref, g_ref, ws_ref, bs_ref, o_ref, *, chunks_per_step):
    v = v_ref[...].astype(F32)
    mu = jnp.mean(v, axis=-1, keepdims=True)
    dlt = v - mu
    y = (dlt * lax.rsqrt(jnp.mean(dlt * dlt, axis=-1, keepdims=True) + EPS) * g_ref[...]).astype(BF16)
    for n in range(chunks_per_step):
        rs = slice(n * B_CHUNK, (n + 1) * B_CHUNK)
        for g in range(B_GROUPS):
            cs = slice(g * 128, (g + 1) * 128)
            vm = _dot(ws_ref[g].astype(BF16), y[rs, cs]) + bs_ref[:, cs]
            o_ref[rs, cs] = (u_ref[rs, cs].astype(F32) * vm).astype(BF16)


def _sgu(p_z, g_v, w_s, bias_full):
    m = p_z.shape[0]
    cps = 4
    tm = cps * B_CHUNK
    return pl.pallas_call(
        functools.partial(_sgu_kernel, chunks_per_step=cps),
        grid=(m // tm,),
        in_specs=[
            pl.BlockSpec((tm, B_WIDTH), lambda i: (i, 0)),
            pl.BlockSpec((tm, B_WIDTH), lambda i: (i, 1)),
            pl.BlockSpec((1, B_WIDTH), lambda i: (0, 0)),
            pl.BlockSpec((B_GROUPS, B_CHUNK, B_CHUNK), lambda i: (0, 0, 0)),
            pl.BlockSpec((B_CHUNK, B_WIDTH), lambda i: (0, 0)),
        ],
        out_specs=pl.BlockSpec((tm, B_WIDTH), lambda i: (i, 0)),
        out_shape=jax.ShapeDtypeStruct((m, B_WIDTH), BF16),
        compiler_params=pltpu.CompilerParams(
            dimension_semantics=("arbitrary",),
            vmem_limit_bytes=_vmem_limit(32 << 20)),
        name="sgu",
    )(p_z, p_z, g_v.reshape(1, B_WIDTH), w_s, bias_full)


def _mix_kernel(hg_ref, sg_ref, ga_ref, gb_ref, x_ref, mod_ref, gpost_ref, gnext_ref,
                wa_ref, wb_ref, wo_ref, h_ref, a_ref, *, sub):
    gt1 = mod_ref[0, 0:1, :]
    sh2 = mod_ref[0, 1:2, :]
    sc2 = mod_ref[0, 2:3, :]
    def branches(rs):
        return rs, _dot(hg_ref[rs, :], wa_ref[...]), _dot(sg_ref[rs, :], wb_ref[...])

    def project(rs, ya, yb):
        merged = ga_ref[rs, :].astype(F32) * ya + gb_ref[rs, :].astype(F32) * yb
        return rs, _dot(merged.astype(BF16), wo_ref[...])

    def finish(rs, mo):
        h = x_ref[rs, :] + gt1 * (_rms(mo) * gpost_ref[...])
        h_ref[rs, :] = h
        a_ref[rs, :] = (_rms(h) * gnext_ref[...] * (1.0 + sc2) + sh2).astype(BF16)

    bounds = [0]
    for size in sub:
        bounds.append(bounds[-1] + size)
    assert bounds[-1] == hg_ref.shape[0]
    n_sub = len(sub)
    after0, after1 = {}, {}
    for step in range(n_sub + 2):
        if step < n_sub:
            after0[step] = branches(slice(bounds[step], bounds[step + 1]))
        if 1 <= step <= n_sub:
            after1[step - 1] = project(*after0.pop(step - 1))
        if step >= 2:
            finish(*after1.pop(step - 2))


def _mix(hg, sg, p_gate, x2, mod3, g_post, g_next, wa, wb, wo, rows_per_batch):
    m, d = x2.shape
    tm, sub = 512, (384, 128)
    bpb = rows_per_batch // tm
    const = lambda shape: pl.BlockSpec(shape, lambda i: (0, 0), pipeline_mode=pl.Buffered(1))
    row = lambda width, col: pl.BlockSpec((tm, width), lambda i: (i, col))
    vec_spec = pl.BlockSpec((1, d), lambda i: (0, 0))
    est = (2 * A_WIDTH * d + d * d) * 2 + 2 * tm * (2 * A_WIDTH * 2 + 2 * d * 2 + d * 4 + d * 4 + d * 2) \
        + 6 * tm * d * 4 + (4 << 20)
    return pl.pallas_call(
        functools.partial(_mix_kernel, sub=sub),
        grid=(m // tm,),
        in_specs=[
            row(A_WIDTH, 0), row(B_WIDTH, 0), row(d, 0), row(d, 1), row(d, 0),
            pl.BlockSpec((1, mod3.shape[1], d), lambda i: (i // bpb, 0, 0)),
            vec_spec, vec_spec,
            const((A_WIDTH, d)), const((B_WIDTH, d)), const((d, d)),
        ],
        out_specs=[row(d, 0), row(d, 0)],
        out_shape=[jax.ShapeDtypeStruct((m, d), F32), jax.ShapeDtypeStruct((m, d), BF16)],
        compiler_params=pltpu.CompilerParams(
            dimension_semantics=("arbitrary",),
            vmem_limit_bytes=_vmem_limit(est)),
        name="mix",
    )(hg, sg, p_gate, p_gate, x2, mod3, g_post.reshape(1, d), g_next.reshape(1, d), wa, wb, wo)


def _resproj_kernel(lhs_ref, w_ref, res_ref, mod_ref, gpost_ref, h_ref, *, gate_row, sub):
    k = pl.program_id(1)
    last = pl.num_programs(1) - 1

    @pl.when(k == 0)
    def _():
        h_ref[...] = _dot(lhs_ref[...], w_ref[...].astype(BF16))

    @pl.when((k > 0) & (k < last))
    def _():
        h_ref[...] += _dot(lhs_ref[...], w_ref[...].astype(BF16))

    @pl.when(k == last)
    def _():
        w = w_ref[...].astype(BF16)
        gate = mod_ref[0, gate_row:gate_row + 1, :]
        for s in range(h_ref.shape[0] // sub):
            rs = slice(s * sub, (s + 1) * sub)
            acc = h_ref[rs, :] + _dot(lhs_ref[rs, :], w)
            h_ref[rs, :] = res_ref[rs, :] + gate * (_rms(acc) * gpost_ref[...])


def _resproj(lhs, w, res, mod3, g_post, gate_row, rows_per_batch, tm, tk, name):
    m, kdim = lhs.shape
    d = w.shape[1]
    bpb = rows_per_batch // tm
    row_spec = pl.BlockSpec((tm, d), lambda i, k: (i, 0))
    est = 2 * tm * tk * 2 + 2 * tk * d * 4 + tk * d * 2 + 4 * tm * d * 4 + 2 * tm * d * 4 + (4 << 20)
    return pl.pallas_call(
        functools.partial(_resproj_kernel, gate_row=gate_row, sub=128),
        grid=(m // tm, kdim // tk),
        in_specs=[
            pl.BlockSpec((tm, tk), lambda i, k: (i, k)),
            pl.BlockSpec((tk, d), lambda i, k: (k, 0)),
            row_spec,
            pl.BlockSpec((1, mod3.shape[1], d), lambda i, k: (i // bpb, 0, 0)),
            pl.BlockSpec((1, d), lambda i, k: (0, 0)),
        ],
        out_specs=row_spec,
        out_shape=jax.ShapeDtypeStruct((m, d), F32),
        compiler_params=pltpu.CompilerParams(
            dimension_semantics=("arbitrary", "arbitrary"),
            vmem_limit_bytes=_vmem_limit(est)),
        name=name,
    )(lhs, w, res, mod3, g_post.reshape(1, d))


def kernel(x, c, w_ada, b_ada, g_pre_mix, g_post_mix, g_pre_ffn, g_post_ffn, w_in, lb_logits,
           g_hgrn_norm, w_a_out, g_sgu_norm, w_spatial, b_spatial, w_b_out, w_o, w_ff1, w_ff2):
    batch, seq_len, d = x.shape
    depth = w_in.shape[0]
    assert depth == 1 and lb_logits.shape == (2, depth + 1, A_WIDTH)
    assert w_in.shape[2] == 11 * SEG and d == 2 * SEG
    assert g_hgrn_norm.shape == (depth, A_DIM) and w_spatial.shape == (depth, B_GROUPS, B_CHUNK, B_CHUNK)
    assert seq_len % 1024 == 0
    m = batch * seq_len

    c8 = jnp.zeros((8, d), F32).at[:batch].set(c)
    b_ada2 = b_ada[0].reshape(1, N_MOD * d)
    n_mix = 2 * d
    mod_mix = _modulation(c8, w_ada[0], b_ada2, n_mix)[:batch].reshape(batch, 2, d)

    x2 = x.reshape(m, d)
    w_in0 = w_in[0]
    a1, p_og = _normproj(x2, mod_mix, g_pre_mix[0], w_in0, SEG_OG, seq_len)
    p_lin, wa, mod_rest = _proj(a1, w_in0, SEG_Q, 4, _act_linear, "inproj_lin", side=w_a_out[0],
                                mod_tail=(c8, w_ada[0], b_ada2, n_mix))
    mod3 = mod_rest[:batch].reshape(batch, N_MOD - 2, d)
    p_z, wb = _proj(a1, w_in0, SEG_Z, 2, _act_gelu, "inproj_gelu", side=w_b_out[0])
    p_gate, wo = _proj(a1, w_in0, SEG_GATE, 4, _act_sigmoid, "inproj_sig", side=w_o[0])

    hg = _hgrn(p_lin, p_og, lb_logits.reshape(2 * (depth + 1), A_WIDTH), g_hgrn_norm[0], batch, seq_len)

    bias_full = jnp.repeat(b_spatial[0].T, 128, axis=1)
    sg = _sgu(p_z, g_sgu_norm[0], w_spatial[0], bias_full)

    h1, a2 = _mix(hg, sg, p_gate, x2, mod3, g_post_mix[0], g_pre_ffn[0], wa, wb, wo, seq_len)
    hid, w2 = _proj(a2, w_ff1[0], 0, w_ff1.shape[2] // SEG, _act_relu2, "ff1", side=w_ff2[0])
    out = _resproj(hid, w2, h1, mod3, g_post_ffn[0], 3, seq_len, tm=1024, tk=1024, name="ff2")
    return out.reshape(batch, seq_len, d)
```

```python
import functools

import jax
import jax.numpy as jnp
from jax import lax
from jax.experimental import pallas as pl
from jax.experimental.pallas import tpu as pltpu

F32 = jnp.float32
BF16 = jnp.bfloat16
EPS = 1e-6
LOG2E = 1.4426950408889634

A_HEADS = 8
A_DIM = 128
A_WIDTH = A_HEADS * A_DIM
B_GROUPS = 8
B_CHUNK = 128
B_WIDTH = B_GROUPS * 128
N_MOD = 6
SEG = 1024
SEG_Q, SEG_F, SEG_V, SEG_OG, SEG_Z, SEG_GATE = 0, 1, 3, 4, 5, 7

HGRN_CHUNK = 64
PROJ_SUB = 128
V7X_VMEM_BYTES = 64 * 1024 * 1024


def _vmem_limit(estimate_bytes):
    return int(min(estimate_bytes, V7X_VMEM_BYTES - 4 * 1024 * 1024))


def _sigmoid(x):
    return 1.0 / (1.0 + jnp.exp(-x))


def _dot(a, b):
    return jnp.dot(a, b, preferred_element_type=F32)


def _dot_nt(a, b):
    return lax.dot_general(a, b, (((1,), (1,)), ((), ())), preferred_element_type=F32)


def _dot_tn(a, b):
    return lax.dot_general(a, b, (((0,), (0,)), ((), ())), preferred_element_type=F32)


def _rms(x):
    return x * lax.rsqrt(jnp.mean(x * x, axis=-1, keepdims=True) + EPS)


def _mod_block(c_ref, w_ref, b_ref):
    c = c_ref[...]
    s = c * _sigmoid(c)
    return _dot(s.astype(BF16), w_ref[...].astype(BF16)) + b_ref[...]


def _mod_kernel(c_ref, w_ref, b_ref, o_ref):
    o_ref[...] = _mod_block(c_ref, w_ref, b_ref)


def _modulation(c8, w_ada, b_ada, n):
    d = w_ada.shape[0]
    tn = 1024
    return pl.pallas_call(
        _mod_kernel,
        grid=(n // tn,),
        in_specs=[
            pl.BlockSpec((8, d), lambda j: (0, 0)),
            pl.BlockSpec((d, tn), lambda j: (0, j)),
            pl.BlockSpec((1, tn), lambda j: (0, j)),
        ],
        out_specs=pl.BlockSpec((8, tn), lambda j: (0, j)),
        out_shape=jax.ShapeDtypeStruct((8, n), F32),
        compiler_params=pltpu.CompilerParams(
            dimension_semantics=("arbitrary",),
            vmem_limit_bytes=_vmem_limit(2 * d * tn * 4 + d * tn * 2 + (8 << 20))),
        name="mod",
    )(c8, w_ada, b_ada)


def _normproj_kernel(x_ref, mod_ref, g_ref, w_ref, a_ref, o_ref, w_scr):
    @pl.when(pl.program_id(0) == 0)
    def _():
        w_scr[...] = w_ref[...].astype(BF16)

    sh = mod_ref[0, 0:1, :]
    scale = g_ref[...] * (1.0 + mod_ref[0, 1:2, :])
    for s in range(x_ref.shape[0] // PROJ_SUB):
        rs = slice(s * PROJ_SUB, (s + 1) * PROJ_SUB)
        a = (_rms(x_ref[rs, :]) * scale + sh).astype(BF16)
        a_ref[rs, :] = a
        o_ref[rs, :] = _act_silu(_dot(a, w_scr[...]), 0).astype(BF16)


def _normproj(x2, mod3, g, w, seg, rows_per_batch):
    m, d = x2.shape
    tm, tn = 1024, SEG
    bpb = rows_per_batch // tm
    est = 2 * tm * d * 6 + 2 * d * tn * 4 + d * tn * 2 + 2 * tm * tn * 2 + 8 * PROJ_SUB * d * 4 + (6 << 20)
    return pl.pallas_call(
        _normproj_kernel,
        grid=(m // tm,),
        in_specs=[
            pl.BlockSpec((tm, d), lambda i: (i, 0)),
            pl.BlockSpec((1, mod3.shape[1], d), lambda i: (i // bpb, 0, 0)),
            pl.BlockSpec((1, d), lambda i: (0, 0)),
            pl.BlockSpec((d, tn), lambda i: (0, seg)),
        ],
        out_specs=[pl.BlockSpec((tm, d), lambda i: (i, 0)), pl.BlockSpec((tm, tn), lambda i: (i, 0))],
        out_shape=[jax.ShapeDtypeStruct((m, d), BF16), jax.ShapeDtypeStruct((m, tn), BF16)],
        scratch_shapes=[pltpu.VMEM((d, tn), BF16)],
        compiler_params=pltpu.CompilerParams(
            dimension_semantics=("arbitrary",),
            vmem_limit_bytes=_vmem_limit(est)),
        name="norm_silu",
    )(x2, mod3, g.reshape(1, d), w)


def _act_linear(acc, j):
    return acc * jnp.where(j == 0, A_DIM ** -0.5, 1.0)


def _act_silu(acc, j):
    return acc * _sigmoid(acc)


def _act_gelu(acc, j):
    return 0.5 * acc * (1.0 + lax.erf(acc * (2.0 ** -0.5)))


def _act_sigmoid(acc, j):
    return _sigmoid(acc)


def _act_relu2(acc, j):
    r = jnp.maximum(acc, 0.0)
    return r * r


def _proj_kernel(a_ref, w_ref, *refs, act, has_side, has_mod):
    refs = list(refs)
    side_ref = refs.pop(0) if has_side else None
    mod_in = [refs.pop(0) for _ in range(3)] if has_mod else None
    o_ref = refs.pop(0)
    if has_side:
        refs.pop(0)[...] = side_ref[...].astype(BF16)
    if has_mod:
        refs.pop(0)[...] = _mod_block(*mod_in)
    (w_scr,) = refs
    j = pl.program_id(0)

    @pl.when(pl.program_id(1) == 0)
    def _():
        w_scr[...] = w_ref[...].astype(BF16)

    for s in range(a_ref.shape[0] // PROJ_SUB):
        rs = slice(s * PROJ_SUB, (s + 1) * PROJ_SUB)
        o_ref[rs, :] = act(_dot(a_ref[rs, :], w_scr[...]), j).astype(BF16)


def _proj(a, w, seg0, nseg, act, name, seg_stride=1, tm=2048, side=None, mod_tail=None):
    m, kdim = a.shape
    tn = SEG
    ni = m // tm
    in_specs = [
        pl.BlockSpec((tm, kdim), lambda j, i: (i, 0)),
        pl.BlockSpec((kdim, tn), lambda j, i: (0, seg0 + j * seg_stride)),
    ]
    out_specs = [pl.BlockSpec((tm, tn), lambda j, i: (i, j))]
    out_shape = [jax.ShapeDtypeStruct((m, nseg * tn), BF16)]
    args = [a, w]
    est = 2 * tm * kdim * 2 + 2 * kdim * tn * 4 + kdim * tn * 2 + 2 * tm * tn * 2 + 8 * PROJ_SUB * tn * 4 + (6 << 20)
    if side is not None:
        rows, cols = side.shape
        blk = rows // (nseg * ni)
        assert blk * nseg * ni == rows and blk % 16 == 0
        side_spec = pl.BlockSpec((blk, cols), lambda j, i: (j * ni + i, 0))
        in_specs.append(side_spec)
        out_specs.append(side_spec)
        out_shape.append(jax.ShapeDtypeStruct(side.shape, BF16))
        args.append(side)
        est += 2 * blk * cols * 6
    if mod_tail is not None:
        c8, w_ada, b_ada, col0 = mod_tail
        dm, n_all = w_ada.shape
        mblk = (n_all - col0) // (nseg * ni)
        assert mblk * nseg * ni == n_all - col0 and mblk % 128 == 0 and col0 % mblk == 0
        in_specs += [
            pl.BlockSpec((8, dm), lambda j, i: (0, 0)),
            pl.BlockSpec((dm, mblk), lambda j, i: (0, col0 // mblk + j * ni + i)),
            pl.BlockSpec((1, mblk), lambda j, i: (0, col0 // mblk + j * ni + i)),
        ]
        out_specs.append(pl.BlockSpec((8, mblk), lambda j, i: (0, j * ni + i)))
        out_shape.append(jax.ShapeDtypeStruct((8, n_all - col0), F32))
        args += [c8, w_ada, b_ada]
        est += 2 * dm * mblk * 4 + dm * mblk * 2
    single = side is None and mod_tail is None
    outs = pl.pallas_call(
        functools.partial(_proj_kernel, act=act, has_side=side is not None, has_mod=mod_tail is not None),
        grid=(nseg, ni),
        in_specs=in_specs,
        out_specs=out_specs,
        out_shape=out_shape,
        scratch_shapes=[pltpu.VMEM((kdim, tn), BF16)],
        compiler_params=pltpu.CompilerParams(
            dimension_semantics=("arbitrary", "arbitrary"),
            vmem_limit_bytes=_vmem_limit(est)),
        name=name,
    )(*args)
    return outs[0] if single else outs


def _hgrn_kernel(q_ref, ffw_ref, fbw_ref, v_ref, og_ref, lbl_ref, gn_ref,
                 o_ref, ofw_scr, obw_scr, st_scr, *, heads_per_step, seq_len):
    t = HGRN_CHUNK
    n_chunks = seq_len // t
    half = t // 2
    f_refs = (ffw_ref, fbw_ref)

    lbl = lbl_ref[...]
    c0s, c1s = [], []
    for d in range(2):
        l0 = lbl[2 * d:2 * d + 1, :]
        l1 = lbl[2 * d + 1:2 * d + 2, :]
        mx = jnp.maximum(l0, l1)
        e0 = jnp.exp(l0 - mx)
        lb = e0 / (e0 + jnp.exp(l1 - mx))
        c0s.append(0.5 * (1.0 + lb))
        c1s.append(0.5 * (1.0 - lb))

    st_scr[...] = jnp.zeros_like(st_scr)

    row = lax.broadcasted_iota(jnp.int32, (t, t), 0)
    col = lax.broadcasted_iota(jnp.int32, (t, t), 1)
    masks = (col <= row, col >= row)
    row2 = lax.broadcasted_iota(jnp.int32, (t, 2 * t), 0)
    col2 = lax.broadcasted_iota(jnp.int32, (t, 2 * t), 1) & (t - 1)
    tris = (jnp.where(col2 <= row2, 1.0, 0.0).astype(BF16), jnp.where(col2 >= row2, 1.0, 0.0).astype(BF16))

    def stage0(c, h, d):
        rows = pl.ds(pl.multiple_of(c * t, t), t)
        hs = slice(h * A_DIM, (h + 1) * A_DIM)
        ct = c1s[d][:, hs] * jnp.tanh(0.5 * f_refs[d][rows, hs].astype(F32))
        k = c1s[d][:, hs] - ct
        lf = jnp.log(c0s[d][:, hs] + ct)
        hi = lf.astype(BF16)
        lo = (lf - hi.astype(F32)).astype(BF16)
        return h, d, rows, hs, k, _dot(tris[d], jnp.concatenate([hi, lo], axis=0))

    def stage1(h, d, rows, hs, k, b):
        b = b * LOG2E
        q = q_ref[rows, hs].astype(F32)
        if d == 0:
            b_end = b[t - 1:t, :]
            b_mid = b[half - 1:half, :]
        else:
            b_end = b[0:1, :]
            b_mid = b[half:half + 1, :]
        qd = (q * jnp.exp2(b)).astype(BF16)
        ke = (k * jnp.exp2(b_end - b)).astype(BF16)
        qm = (q * jnp.exp2(b - b_mid)).astype(BF16)
        km = (k * jnp.exp2(b_mid - b)).astype(BF16)
        return h, d, rows, hs, _dot_nt(qm, km), qd, ke, jnp.exp2(b_end)

    def stage2(finalize, h, d, rows, hs, att, qd, ke, decay):
        v = v_ref[rows, hs]
        att = jnp.where(masks[d], att, 0.0).astype(BF16)
        idx = 2 * h + d
        s_t = st_scr[idx]
        o = _dot(att, v) + _dot_nt(qd, s_t.astype(BF16))
        st_scr[idx] = s_t * decay + _dot_tn(v, ke)
        if finalize:
            other = obw_scr if d == 0 else ofw_scr
            y = _rms(o + other[rows, hs]) * gn_ref[...]
            o_ref[rows, hs] = (y * og_ref[rows, hs].astype(F32)).astype(BF16)
        else:
            o_scr = ofw_scr if d == 0 else obw_scr
            o_scr[rows, hs] = o

    units = [(h, d) for h in range(heads_per_step) for d in range(2)]

    def half_sweep(c0, finalize):
        after0, after1 = {}, {}
        n_half = n_chunks // 2
        for step in range(n_half + 2):
            if step < n_half:
                c = c0 + step
                after0[step] = [stage0(c if d == 0 else n_chunks - 1 - c, h, d) for h, d in units]
            if 1 <= step <= n_half:
                after1[step - 1] = [stage1(*vals) for vals in after0.pop(step - 1)]
            if step >= 2:
                for vals in after1.pop(step - 2):
                    stage2(finalize, *vals)

    half_sweep(0, False)
    half_sweep(n_chunks // 2, True)


def _hgrn(p_lin, p_og, lb_logits4, g_norm, batch, seq_len):
    hps = 2
    wblk = hps * A_DIM
    nblk = A_WIDTH // wblk

    def seg_spec(seg):
        return pl.BlockSpec((seq_len, wblk), lambda b, p: (b, seg * nblk + p))

    est = 2 * 6 * seq_len * wblk * 2 + 2 * seq_len * wblk * 4 + (6 << 20)
    return pl.pallas_call(
        functools.partial(_hgrn_kernel, heads_per_step=hps, seq_len=seq_len),
        grid=(batch, nblk),
        in_specs=[
            seg_spec(0), seg_spec(1), seg_spec(2), seg_spec(3),
            seg_spec(0),
            pl.BlockSpec((4, wblk), lambda b, p: (0, p)),
            pl.BlockSpec((1, A_DIM), lambda b, p: (0, 0)),
        ],
        out_specs=pl.BlockSpec((seq_len, wblk), lambda b, p: (b, p)),
        out_shape=jax.ShapeDtypeStruct((batch * seq_len, A_WIDTH), BF16),
        scratch_shapes=[
            pltpu.VMEM((seq_len, wblk), F32),
            pltpu.VMEM((seq_len, wblk), F32),
            pltpu.VMEM((2 * hps, A_DIM, A_DIM), F32),
        ],
        compiler_params=pltpu.CompilerParams(
            dimension_semantics=("arbitrary", "arbitrary"),
            vmem_limit_bytes=_vmem_limit(est)),
        name="hgrn",
    )(p_lin, p_lin, p_lin, p_lin, p_og, lb_logits4, g_norm.reshape(1, A_DIM))


def _sgu_kernel(u_ref, v_ref, g_ref, ws_ref, bs_ref, o_ref, *, chunks_per_step):
    v = v_ref[...].astype(F32)
    mu = jnp.mean(v, axis=-1, keepdims=True)
    dlt = v - mu
    y = (dlt * lax.rsqrt(jnp.mean(dlt * dlt, axis=-1, keepdims=True) + EPS) * g_ref[...]).astype(BF16)
    for n in range(chunks_per_step):
        rs = slice(n * B_CHUNK, (n + 1) * B_CHUNK)
        for g in range(B_GROUPS):
            cs = slice(g * 128, (g + 1) * 128)
            vm = _dot(ws_ref[g].astype(BF16), y[rs, cs]) + bs_ref[:, cs]
            o_ref[rs, cs] = (u_ref[rs, cs].astype(F32) * vm).astype(BF16)


def _sgu(p_z, g_v, w_s, bias_full):
    m = p_z.shape[0]
    cps = 4
    tm = cps * B_CHUNK
    return pl.pallas_call(
        functools.partial(_sgu_kernel, chunks_per_step=cps),
        grid=(m // tm,),
        in_specs=[
            pl.BlockSpec((tm, B_WIDTH), lambda i: (i, 0)),
            pl.BlockSpec((tm, B_WIDTH), lambda i: (i, 1)),
            pl.BlockSpec((1, B_WIDTH), lambda i: (0, 0)),
            pl.BlockSpec((B_GROUPS, B_CHUNK, B_CHUNK), lambda i: (0, 0, 0)),
            pl.BlockSpec((B_CHUNK, B_WIDTH), lambda i: (0, 0)),
        ],
        out_specs=pl.BlockSpec((tm, B_WIDTH), lambda i: (i, 0)),
        out_shape=jax.ShapeDtypeStruct((m, B_WIDTH), BF16),
        compiler_params=pltpu.CompilerParams(
            dimension_semantics=("arbitrary",),
            vmem_limit_bytes=_vmem_limit(32 << 20)),
        name="sgu",
    )(p_z, p_z, g_v.reshape(1, B_WIDTH), w_s, bias_full)


def _mix_kernel(hg_ref, sg_ref, ga_ref, gb_ref, x_ref, mod_ref, gpost_ref, gnext_ref,
                wa_ref, wb_ref, wo_ref, h_ref, a_ref, *, sub):
    gt1 = mod_ref[0, 0:1, :]
    sh2 = mod_ref[0, 1:2, :]
    sc2 = mod_ref[0, 2:3, :]
    def branches(rs):
        return rs, _dot(hg_ref[rs, :], wa_ref[...]), _dot(sg_ref[rs, :], wb_ref[...])

    def project(rs, ya, yb):
        merged = ga_ref[rs, :].astype(F32) * ya + gb_ref[rs, :].astype(F32) * yb
        return rs, _dot(merged.astype(BF16), wo_ref[...])

    post_scale = gt1 * gpost_ref[...]
    next_scale = gnext_ref[...] * (1.0 + sc2)

    def finish(rs, mo):
        h = x_ref[rs, :] + _rms(mo) * post_scale
        h_ref[rs, :] = h
        a_ref[rs, :] = (_rms(h) * next_scale + sh2).astype(BF16)

    bounds = [0]
    for size in sub:
        bounds.append(bounds[-1] + size)
    assert bounds[-1] == hg_ref.shape[0]
    n_sub = len(sub)
    after0, after1 = {}, {}
    for step in range(n_sub + 2):
        if step < n_sub:
            after0[step] = branches(slice(bounds[step], bounds[step + 1]))
        if 1 <= step <= n_sub:
            after1[step - 1] = project(*after0.pop(step - 1))
        if step >= 2:
            finish(*after1.pop(step - 2))


def _mix(hg, sg, p_gate, x2, mod3, g_post, g_next, wa, wb, wo, rows_per_batch):
    m, d = x2.shape
    tm, sub = 512, (384, 128)
    bpb = rows_per_batch // tm
    const = lambda shape: pl.BlockSpec(shape, lambda i: (0, 0), pipeline_mode=pl.Buffered(1))
    row = lambda width, col: pl.BlockSpec((tm, width), lambda i: (i, col))
    vec_spec = pl.BlockSpec((1, d), lambda i: (0, 0))
    est = (2 * A_WIDTH * d + d * d) * 2 + 2 * tm * (2 * A_WIDTH * 2 + 2 * d * 2 + d * 4 + d * 4 + d * 2) \
        + 6 * tm * d * 4 + (4 << 20)
    return pl.pallas_call(
        functools.partial(_mix_kernel, sub=sub),
        grid=(m // tm,),
        in_specs=[
            row(A_WIDTH, 0), row(B_WIDTH, 0), row(d, 0), row(d, 1), row(d, 0),
            pl.BlockSpec((1, mod3.shape[1], d), lambda i: (i // bpb, 0, 0)),
            vec_spec, vec_spec,
            const((A_WIDTH, d)), const((B_WIDTH, d)), const((d, d)),
        ],
        out_specs=[row(d, 0), row(d, 0)],
        out_shape=[jax.ShapeDtypeStruct((m, d), F32), jax.ShapeDtypeStruct((m, d), BF16)],
        compiler_params=pltpu.CompilerParams(
            dimension_semantics=("arbitrary",),
            vmem_limit_bytes=_vmem_limit(est)),
        name="mix",
    )(hg, sg, p_gate, p_gate, x2, mod3, g_post.reshape(1, d), g_next.reshape(1, d), wa, wb, wo)


def _resproj_kernel(lhs_ref, w_ref, res_ref, mod_ref, gpost_ref, h_ref, *, gate_row, sub):
    k = pl.program_id(1)
    last = pl.num_programs(1) - 1

    @pl.when(k == 0)
    def _():
        h_ref[...] = _dot(lhs_ref[...], w_ref[...].astype(BF16))

    @pl.when((k > 0) & (k < last))
    def _():
        h_ref[...] += _dot(lhs_ref[...], w_ref[...].astype(BF16))

    @pl.when(k == last)
    def _():
        w = w_ref[...].astype(BF16)
        scale = mod_ref[0, gate_row:gate_row + 1, :] * gpost_ref[...]
        for s in range(h_ref.shape[0] // sub):
            rs = slice(s * sub, (s + 1) * sub)
            acc = h_ref[rs, :] + _dot(lhs_ref[rs, :], w)
            h_ref[rs, :] = res_ref[rs, :] + _rms(acc) * scale


def _resproj(lhs, w, res, mod3, g_post, gate_row, rows_per_batch, tm, tk, name):
    m, kdim = lhs.shape
    d = w.shape[1]
    bpb = rows_per_batch // tm
    row_spec = pl.BlockSpec((tm, d), lambda i, k: (i, 0))
    est = 2 * tm * tk * 2 + 2 * tk * d * 4 + tk * d * 2 + 4 * tm * d * 4 + 2 * tm * d * 4 + (4 << 20)
    return pl.pallas_call(
        functools.partial(_resproj_kernel, gate_row=gate_row, sub=128),
        grid=(m // tm, kdim // tk),
        in_specs=[
            pl.BlockSpec((tm, tk), lambda i, k: (i, k)),
            pl.BlockSpec((tk, d), lambda i, k: (k, 0)),
            row_spec,
            pl.BlockSpec((1, mod3.shape[1], d), lambda i, k: (i // bpb, 0, 0)),
            pl.BlockSpec((1, d), lambda i, k: (0, 0)),
        ],
        out_specs=row_spec,
        out_shape=jax.ShapeDtypeStruct((m, d), F32),
        compiler_params=pltpu.CompilerParams(
            dimension_semantics=("arbitrary", "arbitrary"),
            vmem_limit_bytes=_vmem_limit(est)),
        name=name,
    )(lhs, w, res, mod3, g_post.reshape(1, d))


def kernel(x, c, w_ada, b_ada, g_pre_mix, g_post_mix, g_pre_ffn, g_post_ffn, w_in, lb_logits,
           g_hgrn_norm, w_a_out, g_sgu_norm, w_spatial, b_spatial, w_b_out, w_o, w_ff1, w_ff2):
    batch, seq_len, d = x.shape
    depth = w_in.shape[0]
    assert depth == 1 and lb_logits.shape == (2, depth + 1, A_WIDTH)
    assert w_in.shape[2] == 11 * SEG and d == 2 * SEG
    assert g_hgrn_norm.shape == (depth, A_DIM) and w_spatial.shape == (depth, B_GROUPS, B_CHUNK, B_CHUNK)
    assert seq_len % 1024 == 0
    m = batch * seq_len

    c8 = jnp.zeros((8, d), F32).at[:batch].set(c)
    b_ada2 = b_ada[0].reshape(1, N_MOD * d)
    n_mix = 2 * d
    mod_mix = _modulation(c8, w_ada[0], b_ada2, n_mix)[:batch].reshape(batch, 2, d)

    x2 = x.reshape(m, d)
    w_in0 = w_in[0]
    a1, p_og = _normproj(x2, mod_mix, g_pre_mix[0], w_in0, SEG_OG, seq_len)
    p_lin, wa, mod_rest = _proj(a1, w_in0, SEG_Q, 4, _act_linear, "inproj_lin", side=w_a_out[0],
                                mod_tail=(c8, w_ada[0], b_ada2, n_mix))
    mod3 = mod_rest[:batch].reshape(batch, N_MOD - 2, d)
    p_z, wb = _proj(a1, w_in0, SEG_Z, 2, _act_gelu, "inproj_gelu", side=w_b_out[0])
    p_gate, wo = _proj(a1, w_in0, SEG_GATE, 4, _act_sigmoid, "inproj_sig", side=w_o[0])

    hg = _hgrn(p_lin, p_og, lb_logits.reshape(2 * (depth + 1), A_WIDTH), g_hgrn_norm[0], batch, seq_len)

    bias_full = jnp.repeat(b_spatial[0].T, 128, axis=1)
    sg = _sgu(p_z, g_sgu_norm[0], w_spatial[0], bias_full)

    h1, a2 = _mix(hg, sg, p_gate, x2, mod3, g_post_mix[0], g_pre_ffn[0], wa, wb, wo, seq_len)
    hid, w2 = _proj(a2, w_ff1[0], 0, w_ff1.shape[2] // SEG, _act_relu2, "ff1", side=w_ff2[0])
    out = _resproj(hid, w2, h1, mod3, g_post_ffn[0], 3, seq_len, tm=1024, tk=1024, name="ff2")
    return out.reshape(batch, seq_len, d)
```

```python
import functools

import jax
import jax.numpy as jnp
from jax import lax
from jax.experimental import pallas as pl
from jax.experimental.pallas import tpu as pltpu

F32 = jnp.float32
BF16 = jnp.bfloat16
EPS = 1e-6
LOG2E = 1.4426950408889634

A_HEADS = 8
A_DIM = 128
A_WIDTH = A_HEADS * A_DIM
B_GROUPS = 8
B_CHUNK = 128
B_WIDTH = B_GROUPS * 128
N_MOD = 6
SEG = 1024
SEG_Q, SEG_F, SEG_V, SEG_OG, SEG_Z, SEG_GATE = 0, 1, 3, 4, 5, 7

HGRN_CHUNK = 64
PROJ_SUB = 128
V7X_VMEM_BYTES = 64 * 1024 * 1024


def _vmem_limit(estimate_bytes):
    return int(min(estimate_bytes, V7X_VMEM_BYTES - 4 * 1024 * 1024))


def _sigmoid(x):
    return 1.0 / (1.0 + jnp.exp(-x))


def _dot(a, b):
    return jnp.dot(a, b, preferred_element_type=F32)


def _dot_nt(a, b):
    return lax.dot_general(a, b, (((1,), (1,)), ((), ())), preferred_element_type=F32)


def _dot_tn(a, b):
    return lax.dot_general(a, b, (((0,), (0,)), ((), ())), preferred_element_type=F32)


def _rms(x):
    return x * lax.rsqrt(jnp.mean(x * x, axis=-1, keepdims=True) + EPS)


def _mod_block(c_ref, w_ref, b_ref):
    c = c_ref[...]
    s = c * _sigmoid(c)
    return _dot(s.astype(BF16), w_ref[...].astype(BF16)) + b_ref[...]


def _mod_kernel(c_ref, w_ref, b_ref, o_ref):
    o_ref[...] = _mod_block(c_ref, w_ref, b_ref)


def _modulation(c8, w_ada, b_ada, n):
    d = w_ada.shape[0]
    tn = 1024
    return pl.pallas_call(
        _mod_kernel,
        grid=(n // tn,),
        in_specs=[
            pl.BlockSpec((8, d), lambda j: (0, 0)),
            pl.BlockSpec((d, tn), lambda j: (0, j)),
            pl.BlockSpec((1, tn), lambda j: (0, j)),
        ],
        out_specs=pl.BlockSpec((8, tn), lambda j: (0, j)),
        out_shape=jax.ShapeDtypeStruct((8, n), F32),
        compiler_params=pltpu.CompilerParams(
            dimension_semantics=("arbitrary",),
            vmem_limit_bytes=_vmem_limit(2 * d * tn * 4 + d * tn * 2 + (8 << 20))),
        name="mod",
    )(c8, w_ada, b_ada)


def _normproj_kernel(x_ref, mod_ref, g_ref, w_ref, a_ref, o_ref, w_scr):
    @pl.when(pl.program_id(0) == 0)
    def _():
        w_scr[...] = w_ref[...].astype(BF16)

    sh = mod_ref[0, 0:1, :]
    scale = g_ref[...] * (1.0 + mod_ref[0, 1:2, :])
    for s in range(x_ref.shape[0] // PROJ_SUB):
        rs = slice(s * PROJ_SUB, (s + 1) * PROJ_SUB)
        a = (_rms(x_ref[rs, :]) * scale + sh).astype(BF16)
        a_ref[rs, :] = a
        o_ref[rs, :] = _act_silu(_dot(a, w_scr[...]), 0).astype(BF16)


def _normproj(x2, mod3, g, w, seg, rows_per_batch):
    m, d = x2.shape
    tm, tn = 1024, SEG
    bpb = rows_per_batch // tm
    est = 2 * tm * d * 6 + 2 * d * tn * 4 + d * tn * 2 + 2 * tm * tn * 2 + 8 * PROJ_SUB * d * 4 + (6 << 20)
    return pl.pallas_call(
        _normproj_kernel,
        grid=(m // tm,),
        in_specs=[
            pl.BlockSpec((tm, d), lambda i: (i, 0)),
            pl.BlockSpec((1, mod3.shape[1], d), lambda i: (i // bpb, 0, 0)),
            pl.BlockSpec((1, d), lambda i: (0, 0)),
            pl.BlockSpec((d, tn), lambda i: (0, seg)),
        ],
        out_specs=[pl.BlockSpec((tm, d), lambda i: (i, 0)), pl.BlockSpec((tm, tn), lambda i: (i, 0))],
        out_shape=[jax.ShapeDtypeStruct((m, d), BF16), jax.ShapeDtypeStruct((m, tn), BF16)],
        scratch_shapes=[pltpu.VMEM((d, tn), BF16)],
        compiler_params=pltpu.CompilerParams(
            dimension_semantics=("arbitrary",),
            vmem_limit_bytes=_vmem_limit(est)),
        name="norm_silu",
    )(x2, mod3, g.reshape(1, d), w)


def _act_linear(acc, j):
    return acc * jnp.where(j == 0, A_DIM ** -0.5, 1.0)


def _act_silu(acc, j):
    return acc * _sigmoid(acc)


def _act_gelu(acc, j):
    return 0.5 * acc * (1.0 + lax.erf(acc * (2.0 ** -0.5)))


def _act_sigmoid(acc, j):
    return _sigmoid(acc)


def _act_relu2(acc, j):
    r = jnp.maximum(acc, 0.0)
    return r * r


def _proj_kernel(a_ref, w_ref, *refs, act, has_side, has_mod):
    refs = list(refs)
    side_ref = refs.pop(0) if has_side else None
    mod_in = [refs.pop(0) for _ in range(3)] if has_mod else None
    o_ref = refs.pop(0)
    if has_side:
        refs.pop(0)[...] = side_ref[...].astype(BF16)
    if has_mod:
        refs.pop(0)[...] = _mod_block(*mod_in)
    (w_scr,) = refs
    j = pl.program_id(0)

    @pl.when(pl.program_id(1) == 0)
    def _():
        w_scr[...] = w_ref[...].astype(BF16)

    for s in range(a_ref.shape[0] // PROJ_SUB):
        rs = slice(s * PROJ_SUB, (s + 1) * PROJ_SUB)
        o_ref[rs, :] = act(_dot(a_ref[rs, :], w_scr[...]), j).astype(BF16)


def _proj(a, w, seg0, nseg, act, name, seg_stride=1, tm=2048, side=None, mod_tail=None):
    m, kdim = a.shape
    tn = SEG
    ni = m // tm
    in_specs = [
        pl.BlockSpec((tm, kdim), lambda j, i: (i, 0)),
        pl.BlockSpec((kdim, tn), lambda j, i: (0, seg0 + j * seg_stride)),
    ]
    out_specs = [pl.BlockSpec((tm, tn), lambda j, i: (i, j))]
    out_shape = [jax.ShapeDtypeStruct((m, nseg * tn), BF16)]
    args = [a, w]
    est = 2 * tm * kdim * 2 + 2 * kdim * tn * 4 + kdim * tn * 2 + 2 * tm * tn * 2 + 8 * PROJ_SUB * tn * 4 + (6 << 20)
    if side is not None:
        rows, cols = side.shape
        blk = rows // (nseg * ni)
        assert blk * nseg * ni == rows and blk % 16 == 0
        side_spec = pl.BlockSpec((blk, cols), lambda j, i: (j * ni + i, 0))
        in_specs.append(side_spec)
        out_specs.append(side_spec)
        out_shape.append(jax.ShapeDtypeStruct(side.shape, BF16))
        args.append(side)
        est += 2 * blk * cols * 6
    if mod_tail is not None:
        c8, w_ada, b_ada, col0 = mod_tail
        dm, n_all = w_ada.shape
        mblk = (n_all - col0) // (nseg * ni)
        assert mblk * nseg * ni == n_all - col0 and mblk % 128 == 0 and col0 % mblk == 0
        in_specs += [
            pl.BlockSpec((8, dm), lambda j, i: (0, 0)),
            pl.BlockSpec((dm, mblk), lambda j, i: (0, col0 // mblk + j * ni + i)),
            pl.BlockSpec((1, mblk), lambda j, i: (0, col0 // mblk + j * ni + i)),
        ]
        out_specs.append(pl.BlockSpec((8, mblk), lambda j, i: (0, j * ni + i)))
        out_shape.append(jax.ShapeDtypeStruct((8, n_all - col0), F32))
        args += [c8, w_ada, b_ada]
        est += 2 * dm * mblk * 4 + dm * mblk * 2
    single = side is None and mod_tail is None
    outs = pl.pallas_call(
        functools.partial(_proj_kernel, act=act, has_side=side is not None, has_mod=mod_tail is not None),
        grid=(nseg, ni),
        in_specs=in_specs,
        out_specs=out_specs,
        out_shape=out_shape,
        scratch_shapes=[pltpu.VMEM((kdim, tn), BF16)],
        compiler_params=pltpu.CompilerParams(
            dimension_semantics=("arbitrary", "arbitrary"),
            vmem_limit_bytes=_vmem_limit(est)),
        name=name,
    )(*args)
    return outs[0] if single else outs


_INPROJ_CLASSES = ((4, _act_linear), (2, _act_gelu), (4, _act_sigmoid))
_INPROJ_SEG_SKIP = SEG_OG


def _inproj_all_kernel(a_ref, w_ref, c_ref, wm_ref, bm_ref, o_ref, mod_ref, w_scr, *, n_mod_steps):
    j = pl.program_id(0)
    i = pl.program_id(1)

    @pl.when(i == 0)
    def _():
        w_scr[...] = w_ref[...].astype(BF16)

    @pl.when(j * pl.num_programs(1) + i < n_mod_steps)
    def _():
        mod_ref[...] = _mod_block(c_ref, wm_ref, bm_ref)

    lo = 0
    for count, act in _INPROJ_CLASSES:
        @pl.when((j >= lo) & (j < lo + count))
        def _(act=act):
            for s in range(a_ref.shape[0] // PROJ_SUB):
                rs = slice(s * PROJ_SUB, (s + 1) * PROJ_SUB)
                o_ref[rs, :] = act(_dot(a_ref[rs, :], w_scr[...]), j).astype(BF16)
        lo += count


def _inproj_all(a, w, mod_tail, tm=2048):
    m, kdim = a.shape
    tn = SEG
    ni = m // tm
    nseg = sum(count for count, _ in _INPROJ_CLASSES)
    c8, w_ada, b_ada, col0 = mod_tail
    dm, n_all = w_ada.shape
    mblk = 512
    n_mod_steps = (n_all - col0) // mblk
    assert n_mod_steps * mblk == n_all - col0 and n_mod_steps <= nseg * ni and col0 % mblk == 0
    mod_idx = lambda j, i: jnp.minimum(j * ni + i, n_mod_steps - 1)
    est = 2 * tm * kdim * 2 + 2 * kdim * tn * 4 + kdim * tn * 2 + 2 * tm * tn * 2 + 8 * PROJ_SUB * tn * 4 \
        + 2 * dm * mblk * 4 + dm * mblk * 2 + (6 << 20)
    return pl.pallas_call(
        functools.partial(_inproj_all_kernel, n_mod_steps=n_mod_steps),
        grid=(nseg, ni),
        in_specs=[
            pl.BlockSpec((tm, kdim), lambda j, i: (i, 0)),
            pl.BlockSpec((kdim, tn), lambda j, i: (0, j + jnp.where(j >= _INPROJ_SEG_SKIP, 1, 0))),
            pl.BlockSpec((8, dm), lambda j, i: (0, 0)),
            pl.BlockSpec((dm, mblk), lambda j, i: (0, col0 // mblk + mod_idx(j, i))),
            pl.BlockSpec((1, mblk), lambda j, i: (0, col0 // mblk + mod_idx(j, i))),
        ],
        out_specs=[
            pl.BlockSpec((tm, tn), lambda j, i: (i, j)),
            pl.BlockSpec((8, mblk), lambda j, i: (0, mod_idx(j, i))),
        ],
        out_shape=[jax.ShapeDtypeStruct((m, nseg * tn), BF16), jax.ShapeDtypeStruct((8, n_all - col0), F32)],
        scratch_shapes=[pltpu.VMEM((kdim, tn), BF16)],
        compiler_params=pltpu.CompilerParams(
            dimension_semantics=("arbitrary", "arbitrary"),
            vmem_limit_bytes=_vmem_limit(est)),
        name="inproj",
    )(a, w, c8, w_ada, b_ada)


def _hgrn_kernel(q_ref, ffw_ref, fbw_ref, v_ref, og_ref, lbl_ref, gn_ref, *refs,
                 heads_per_step, seq_len, n_side):
    side_in = refs[:n_side]
    o_ref = refs[n_side]
    side_out = refs[n_side + 1:2 * n_side + 1]
    ofw_scr, obw_scr, st_scr = refs[2 * n_side + 1:]
    for src, dst in zip(side_in, side_out):
        dst[...] = src[...].astype(BF16)
    t = HGRN_CHUNK
    n_chunks = seq_len // t
    half = t // 2
    f_refs = (ffw_ref, fbw_ref)

    lbl = lbl_ref[...]
    c0s, c1s = [], []
    for d in range(2):
        l0 = lbl[2 * d:2 * d + 1, :]
        l1 = lbl[2 * d + 1:2 * d + 2, :]
        mx = jnp.maximum(l0, l1)
        e0 = jnp.exp(l0 - mx)
        lb = e0 / (e0 + jnp.exp(l1 - mx))
        c0s.append(0.5 * (1.0 + lb))
        c1s.append(0.5 * (1.0 - lb))

    st_scr[...] = jnp.zeros_like(st_scr)

    row = lax.broadcasted_iota(jnp.int32, (t, t), 0)
    col = lax.broadcasted_iota(jnp.int32, (t, t), 1)
    masks = (col <= row, col >= row)
    row2 = lax.broadcasted_iota(jnp.int32, (t, 2 * t), 0)
    col2 = lax.broadcasted_iota(jnp.int32, (t, 2 * t), 1) & (t - 1)
    tris = (jnp.where(col2 <= row2, 1.0, 0.0).astype(BF16), jnp.where(col2 >= row2, 1.0, 0.0).astype(BF16))

    def stage0(c, h, d):
        rows = pl.ds(pl.multiple_of(c * t, t), t)
        hs = slice(h * A_DIM, (h + 1) * A_DIM)
        ct = c1s[d][:, hs] * jnp.tanh(0.5 * f_refs[d][rows, hs].astype(F32))
        k = c1s[d][:, hs] - ct
        lf = jnp.log(c0s[d][:, hs] + ct)
        hi = lf.astype(BF16)
        lo = (lf - hi.astype(F32)).astype(BF16)
        return h, d, rows, hs, k, _dot(tris[d], jnp.concatenate([hi, lo], axis=0))

    def stage1(h, d, rows, hs, k, b):
        b = b * LOG2E
        q = q_ref[rows, hs].astype(F32)
        if d == 0:
            b_end = b[t - 1:t, :]
            b_mid = b[half - 1:half, :]
        else:
            b_end = b[0:1, :]
            b_mid = b[half:half + 1, :]
        qd = (q * jnp.exp2(b)).astype(BF16)
        ke = (k * jnp.exp2(b_end - b)).astype(BF16)
        qm = (q * jnp.exp2(b - b_mid)).astype(BF16)
        km = (k * jnp.exp2(b_mid - b)).astype(BF16)
        return h, d, rows, hs, _dot_nt(qm, km), qd, ke, jnp.exp2(b_end)

    def stage2(finalize, h, d, rows, hs, att, qd, ke, decay):
        v = v_ref[rows, hs]
        att = jnp.where(masks[d], att, 0.0).astype(BF16)
        idx = 2 * h + d
        s_t = st_scr[idx]
        o = _dot(att, v) + _dot_nt(qd, s_t.astype(BF16))
        st_scr[idx] = s_t * decay + _dot_tn(v, ke)
        if finalize:
            other = obw_scr if d == 0 else ofw_scr
            y = _rms(o + other[rows, hs]) * gn_ref[...]
            o_ref[rows, hs] = (y * og_ref[rows, hs].astype(F32)).astype(BF16)
        else:
            o_scr = ofw_scr if d == 0 else obw_scr
            o_scr[rows, hs] = o

    units = [(h, d) for h in range(heads_per_step) for d in range(2)]

    def half_sweep(c0, finalize):
        after0, after1 = {}, {}
        n_half = n_chunks // 2
        for step in range(n_half + 2):
            if step < n_half:
                c = c0 + step
                after0[step] = [stage0(c if d == 0 else n_chunks - 1 - c, h, d) for h, d in units]
            if 1 <= step <= n_half:
                after1[step - 1] = [stage1(*vals) for vals in after0.pop(step - 1)]
            if step >= 2:
                for vals in after1.pop(step - 2):
                    stage2(finalize, *vals)

    half_sweep(0, False)
    half_sweep(n_chunks // 2, True)


def _hgrn(p_lin, p_og, lb_logits4, g_norm, batch, seq_len, sides=()):
    hps = 2
    wblk = hps * A_DIM
    nblk = A_WIDTH // wblk
    n_steps = batch * nblk

    def seg_spec(seg):
        return pl.BlockSpec((seq_len, wblk), lambda b, p: (b, seg * nblk + p))

    side_specs = []
    for w in sides:
        rows, cols = w.shape
        assert rows % (16 * n_steps) == 0
        side_specs.append(pl.BlockSpec((rows // n_steps, cols), lambda b, p: (b * nblk + p, 0)))

    est = 2 * 6 * seq_len * wblk * 2 + 2 * seq_len * wblk * 4 + sum(2 * w.size * 6 // n_steps for w in sides) \
        + (6 << 20)
    return pl.pallas_call(
        functools.partial(_hgrn_kernel, heads_per_step=hps, seq_len=seq_len, n_side=len(sides)),
        grid=(batch, nblk),
        in_specs=[
            seg_spec(0), seg_spec(1), seg_spec(2), seg_spec(3),
            seg_spec(0),
            pl.BlockSpec((4, wblk), lambda b, p: (0, p)),
            pl.BlockSpec((1, A_DIM), lambda b, p: (0, 0)),
        ] + side_specs,
        out_specs=[pl.BlockSpec((seq_len, wblk), lambda b, p: (b, p))] + side_specs,
        out_shape=[jax.ShapeDtypeStruct((batch * seq_len, A_WIDTH), BF16)]
        + [jax.ShapeDtypeStruct(w.shape, BF16) for w in sides],
        scratch_shapes=[
            pltpu.VMEM((seq_len, wblk), F32),
            pltpu.VMEM((seq_len, wblk), F32),
            pltpu.VMEM((2 * hps, A_DIM, A_DIM), F32),
        ],
        compiler_params=pltpu.CompilerParams(
            dimension_semantics=("arbitrary", "arbitrary"),
            vmem_limit_bytes=_vmem_limit(est)),
        name="hgrn",
    )(p_lin, p_lin, p_lin, p_lin, p_og, lb_logits4, g_norm.reshape(1, A_DIM), *sides)


def _sgu_kernel(u_ref, v_ref, g_ref, ws_ref, bs_ref, o_ref, *, chunks_per_step):
    v = v_ref[...].astype(F32)
    mu = jnp.mean(v, axis=-1, keepdims=True)
    dlt = v - mu
    y = (dlt * lax.rsqrt(jnp.mean(dlt * dlt, axis=-1, keepdims=True) + EPS) * g_ref[...]).astype(BF16)
    for n in range(chunks_per_step):
        rs = slice(n * B_CHUNK, (n + 1) * B_CHUNK)
        for g in range(B_GROUPS):
            cs = slice(g * 128, (g + 1) * 128)
            vm = _dot(ws_ref[g].astype(BF16), y[rs, cs]) + bs_ref[:, cs]
            o_ref[rs, cs] = (u_ref[rs, cs].astype(F32) * vm).astype(BF16)


def _sgu(p_z, z_blk, g_v, w_s, bias_full):
    m = p_z.shape[0]
    cps = 4
    tm = cps * B_CHUNK
    return pl.pallas_call(
        functools.partial(_sgu_kernel, chunks_per_step=cps),
        grid=(m // tm,),
        in_specs=[
            pl.BlockSpec((tm, B_WIDTH), lambda i: (i, z_blk)),
            pl.BlockSpec((tm, B_WIDTH), lambda i: (i, z_blk + 1)),
            pl.BlockSpec((1, B_WIDTH), lambda i: (0, 0)),
            pl.BlockSpec((B_GROUPS, B_CHUNK, B_CHUNK), lambda i: (0, 0, 0)),
            pl.BlockSpec((B_CHUNK, B_WIDTH), lambda i: (0, 0)),
        ],
        out_specs=pl.BlockSpec((tm, B_WIDTH), lambda i: (i, 0)),
        out_shape=jax.ShapeDtypeStruct((m, B_WIDTH), BF16),
        compiler_params=pltpu.CompilerParams(
            dimension_semantics=("arbitrary",),
            vmem_limit_bytes=_vmem_limit(32 << 20)),
        name="sgu",
    )(p_z, p_z, g_v.reshape(1, B_WIDTH), w_s, bias_full)


def _mix_kernel(hg_ref, sg_ref, ga_ref, gb_ref, x_ref, mod_ref, gpost_ref, gnext_ref,
                wa_ref, wb_ref, wo_ref, h_ref, a_ref, *, sub):
    gt1 = mod_ref[0, 0:1, :]
    sh2 = mod_ref[0, 1:2, :]
    sc2 = mod_ref[0, 2:3, :]
    def branches(rs):
        return rs, _dot(hg_ref[rs, :], wa_ref[...]), _dot(sg_ref[rs, :], wb_ref[...])

    def project(rs, ya, yb):
        merged = ga_ref[rs, :].astype(F32) * ya + gb_ref[rs, :].astype(F32) * yb
        return rs, _dot(merged.astype(BF16), wo_ref[...])

    post_scale = gt1 * gpost_ref[...]
    next_scale = gnext_ref[...] * (1.0 + sc2)

    def finish(rs, mo):
        h = x_ref[rs, :] + _rms(mo) * post_scale
        h_ref[rs, :] = h
        a_ref[rs, :] = (_rms(h) * next_scale + sh2).astype(BF16)

    bounds = [0]
    for size in sub:
        bounds.append(bounds[-1] + size)
    assert bounds[-1] == hg_ref.shape[0]
    n_sub = len(sub)
    after0, after1 = {}, {}
    for step in range(n_sub + 2):
        if step < n_sub:
            after0[step] = branches(slice(bounds[step], bounds[step + 1]))
        if 1 <= step <= n_sub:
            after1[step - 1] = project(*after0.pop(step - 1))
        if step >= 2:
            finish(*after1.pop(step - 2))


def _mix(hg, sg, p_gate, gate_blk, x2, mod3, g_post, g_next, wa, wb, wo, rows_per_batch):
    m, d = x2.shape
    tm, sub = 512, (384, 128)
    bpb = rows_per_batch // tm
    const = lambda shape: pl.BlockSpec(shape, lambda i: (0, 0), pipeline_mode=pl.Buffered(1))
    row = lambda width, col: pl.BlockSpec((tm, width), lambda i: (i, col))
    vec_spec = pl.BlockSpec((1, d), lambda i: (0, 0))
    est = (2 * A_WIDTH * d + d * d) * 2 + 2 * tm * (2 * A_WIDTH * 2 + 2 * d * 2 + d * 4 + d * 4 + d * 2) \
        + 6 * tm * d * 4 + (4 << 20)
    return pl.pallas_call(
        functools.partial(_mix_kernel, sub=sub),
        grid=(m // tm,),
        in_specs=[
            row(A_WIDTH, 0), row(B_WIDTH, 0), row(d, gate_blk), row(d, gate_blk + 1), row(d, 0),
            pl.BlockSpec((1, mod3.shape[1], d), lambda i: (i // bpb, 0, 0)),
            vec_spec, vec_spec,
            const((A_WIDTH, d)), const((B_WIDTH, d)), const((d, d)),
        ],
        out_specs=[row(d, 0), row(d, 0)],
        out_shape=[jax.ShapeDtypeStruct((m, d), F32), jax.ShapeDtypeStruct((m, d), BF16)],
        compiler_params=pltpu.CompilerParams(
            dimension_semantics=("arbitrary",),
            vmem_limit_bytes=_vmem_limit(est)),
        name="mix",
    )(hg, sg, p_gate, p_gate, x2, mod3, g_post.reshape(1, d), g_next.reshape(1, d), wa, wb, wo)


def _resproj_kernel(lhs_ref, w_ref, res_ref, mod_ref, gpost_ref, h_ref, *, gate_row, sub):
    k = pl.program_id(1)
    last = pl.num_programs(1) - 1

    @pl.when(k == 0)
    def _():
        h_ref[...] = _dot(lhs_ref[...], w_ref[...].astype(BF16))

    @pl.when((k > 0) & (k < last))
    def _():
        h_ref[...] += _dot(lhs_ref[...], w_ref[...].astype(BF16))

    @pl.when(k == last)
    def _():
        w = w_ref[...].astype(BF16)
        scale = mod_ref[0, gate_row:gate_row + 1, :] * gpost_ref[...]
        for s in range(h_ref.shape[0] // sub):
            rs = slice(s * sub, (s + 1) * sub)
            acc = h_ref[rs, :] + _dot(lhs_ref[rs, :], w)
            h_ref[rs, :] = res_ref[rs, :] + _rms(acc) * scale


def _resproj(lhs, w, res, mod3, g_post, gate_row, rows_per_batch, tm, tk, name):
    m, kdim = lhs.shape
    d = w.shape[1]
    bpb = rows_per_batch // tm
    row_spec = pl.BlockSpec((tm, d), lambda i, k: (i, 0))
    est = 2 * tm * tk * 2 + 2 * tk * d * 4 + tk * d * 2 + 4 * tm * d * 4 + 2 * tm * d * 4 + (4 << 20)
    return pl.pallas_call(
        functools.partial(_resproj_kernel, gate_row=gate_row, sub=128),
        grid=(m // tm, kdim // tk),
        in_specs=[
            pl.BlockSpec((tm, tk), lambda i, k: (i, k)),
            pl.BlockSpec((tk, d), lambda i, k: (k, 0)),
            row_spec,
            pl.BlockSpec((1, mod3.shape[1], d), lambda i, k: (i // bpb, 0, 0)),
            pl.BlockSpec((1, d), lambda i, k: (0, 0)),
        ],
        out_specs=row_spec,
        out_shape=jax.ShapeDtypeStruct((m, d), F32),
        compiler_params=pltpu.CompilerParams(
            dimension_semantics=("arbitrary", "arbitrary"),
            vmem_limit_bytes=_vmem_limit(est)),
        name=name,
    )(lhs, w, res, mod3, g_post.reshape(1, d))


def kernel(x, c, w_ada, b_ada, g_pre_mix, g_post_mix, g_pre_ffn, g_post_ffn, w_in, lb_logits,
           g_hgrn_norm, w_a_out, g_sgu_norm, w_spatial, b_spatial, w_b_out, w_o, w_ff1, w_ff2):
    batch, seq_len, d = x.shape
    depth = w_in.shape[0]
    assert depth == 1 and lb_logits.shape == (2, depth + 1, A_WIDTH)
    assert w_in.shape[2] == 11 * SEG and d == 2 * SEG
    assert g_hgrn_norm.shape == (depth, A_DIM) and w_spatial.shape == (depth, B_GROUPS, B_CHUNK, B_CHUNK)
    assert seq_len % 1024 == 0
    m = batch * seq_len

    c8 = jnp.zeros((8, d), F32).at[:batch].set(c)
    b_ada2 = b_ada[0].reshape(1, N_MOD * d)
    n_mix = 2 * d
    mod_mix = _modulation(c8, w_ada[0], b_ada2, n_mix)[:batch].reshape(batch, 2, d)

    x2 = x.reshape(m, d)
    w_in0 = w_in[0]
    a1, p_og = _normproj(x2, mod_mix, g_pre_mix[0], w_in0, SEG_OG, seq_len)
    p, mod_rest = _inproj_all(a1, w_in0, (c8, w_ada[0], b_ada2, n_mix))
    mod3 = mod_rest[:batch].reshape(batch, N_MOD - 2, d)

    hg, wa, wb, wo = _hgrn(p, p_og, lb_logits.reshape(2 * (depth + 1), A_WIDTH), g_hgrn_norm[0], batch, seq_len,
                           sides=(w_a_out[0], w_b_out[0], w_o[0]))

    bias_full = jnp.repeat(b_spatial[0].T, 128, axis=1)
    sg = _sgu(p, 4, g_sgu_norm[0], w_spatial[0], bias_full)

    h1, a2 = _mix(hg, sg, p, 3, x2, mod3, g_post_mix[0], g_pre_ffn[0], wa, wb, wo, seq_len)
    hid, w2 = _proj(a2, w_ff1[0], 0, w_ff1.shape[2] // SEG, _act_relu2, "ff1", side=w_ff2[0])
    out = _resproj(hid, w2, h1, mod3, g_post_ffn[0], 3, seq_len, tm=1024, tk=1024, name="ff2")
    return out.reshape(batch, seq_len, d)
```

```python
import functools

import jax
import jax.numpy as jnp
from jax import lax
from jax.experimental import pallas as pl
from jax.experimental.pallas import tpu as pltpu

F32 = jnp.float32
BF16 = jnp.bfloat16
EPS = 1e-6
LOG2E = 1.4426950408889634

A_HEADS = 8
A_DIM = 128
A_WIDTH = A_HEADS * A_DIM
B_GROUPS = 8
B_CHUNK = 128
B_WIDTH = B_GROUPS * 128
N_MOD = 6
SEG = 1024
SEG_Q, SEG_F, SEG_V, SEG_OG, SEG_Z, SEG_GATE = 0, 1, 3, 4, 5, 7

HGRN_CHUNK = 64
PROJ_SUB = 128
V7X_VMEM_BYTES = 64 * 1024 * 1024


def _vmem_limit(estimate_bytes):
    return int(min(estimate_bytes, V7X_VMEM_BYTES - 4 * 1024 * 1024))


def _sigmoid(x):
    return 1.0 / (1.0 + jnp.exp(-x))


def _dot(a, b):
    return jnp.dot(a, b, preferred_element_type=F32)


def _dot_nt(a, b):
    return lax.dot_general(a, b, (((1,), (1,)), ((), ())), preferred_element_type=F32)


def _dot_tn(a, b):
    return lax.dot_general(a, b, (((0,), (0,)), ((), ())), preferred_element_type=F32)


def _rms(x):
    return x * lax.rsqrt(jnp.mean(x * x, axis=-1, keepdims=True) + EPS)


def _mod_block(c_ref, w_ref, b_ref):
    c = c_ref[...]
    s = c * _sigmoid(c)
    return _dot(s.astype(BF16), w_ref[...].astype(BF16)) + b_ref[...]


def _mod_kernel(c_ref, w_ref, b_ref, o_ref):
    o_ref[...] = _mod_block(c_ref, w_ref, b_ref)


def _modulation(c8, w_ada, b_ada, n):
    d = w_ada.shape[0]
    tn = 1024
    return pl.pallas_call(
        _mod_kernel,
        grid=(n // tn,),
        in_specs=[
            pl.BlockSpec((8, d), lambda j: (0, 0)),
            pl.BlockSpec((d, tn), lambda j: (0, j)),
            pl.BlockSpec((1, tn), lambda j: (0, j)),
        ],
        out_specs=pl.BlockSpec((8, tn), lambda j: (0, j)),
        out_shape=jax.ShapeDtypeStruct((8, n), F32),
        compiler_params=pltpu.CompilerParams(
            dimension_semantics=("arbitrary",),
            vmem_limit_bytes=_vmem_limit(2 * d * tn * 4 + d * tn * 2 + (8 << 20))),
        name="mod",
    )(c8, w_ada, b_ada)


def _normproj_kernel(x_ref, mod_ref, g_ref, w_ref, a_ref, o_ref, w_scr):
    @pl.when(pl.program_id(0) == 0)
    def _():
        w_scr[...] = w_ref[...].astype(BF16)

    sh = mod_ref[0, 0:1, :]
    scale = g_ref[...] * (1.0 + mod_ref[0, 1:2, :])
    for s in range(x_ref.shape[0] // PROJ_SUB):
        rs = slice(s * PROJ_SUB, (s + 1) * PROJ_SUB)
        a = (_rms(x_ref[rs, :]) * scale + sh).astype(BF16)
        a_ref[rs, :] = a
        o_ref[rs, :] = _act_silu(_dot(a, w_scr[...]), 0).astype(BF16)


def _normproj(x2, mod3, g, w, seg, rows_per_batch):
    m, d = x2.shape
    tm, tn = 1024, SEG
    bpb = rows_per_batch // tm
    est = 2 * tm * d * 6 + 2 * d * tn * 4 + d * tn * 2 + 2 * tm * tn * 2 + 8 * PROJ_SUB * d * 4 + (6 << 20)
    return pl.pallas_call(
        _normproj_kernel,
        grid=(m // tm,),
        in_specs=[
            pl.BlockSpec((tm, d), lambda i: (i, 0)),
            pl.BlockSpec((1, mod3.shape[1], d), lambda i: (i // bpb, 0, 0)),
            pl.BlockSpec((1, d), lambda i: (0, 0)),
            pl.BlockSpec((d, tn), lambda i: (0, seg)),
        ],
        out_specs=[pl.BlockSpec((tm, d), lambda i: (i, 0)), pl.BlockSpec((tm, tn), lambda i: (i, 0))],
        out_shape=[jax.ShapeDtypeStruct((m, d), BF16), jax.ShapeDtypeStruct((m, tn), BF16)],
        scratch_shapes=[pltpu.VMEM((d, tn), BF16)],
        compiler_params=pltpu.CompilerParams(
            dimension_semantics=("arbitrary",),
            vmem_limit_bytes=_vmem_limit(est)),
        name="norm_silu",
    )(x2, mod3, g.reshape(1, d), w)


def _act_linear(acc, j):
    return acc * jnp.where(j == 0, A_DIM ** -0.5, 1.0)


def _act_silu(acc, j):
    return acc * _sigmoid(acc)


def _act_gelu(acc, j):
    return 0.5 * acc * (1.0 + lax.erf(acc * (2.0 ** -0.5)))


def _act_sigmoid(acc, j):
    return _sigmoid(acc)


def _act_relu2(acc, j):
    r = jnp.maximum(acc, 0.0)
    return r * r


def _proj_kernel(a_ref, w_ref, *refs, act, has_side, has_mod):
    refs = list(refs)
    side_ref = refs.pop(0) if has_side else None
    mod_in = [refs.pop(0) for _ in range(3)] if has_mod else None
    o_ref = refs.pop(0)
    if has_side:
        refs.pop(0)[...] = side_ref[...].astype(BF16)
    if has_mod:
        refs.pop(0)[...] = _mod_block(*mod_in)
    (w_scr,) = refs
    j = pl.program_id(0)

    @pl.when(pl.program_id(1) == 0)
    def _():
        w_scr[...] = w_ref[...].astype(BF16)

    for s in range(a_ref.shape[0] // PROJ_SUB):
        rs = slice(s * PROJ_SUB, (s + 1) * PROJ_SUB)
        o_ref[rs, :] = act(_dot(a_ref[rs, :], w_scr[...]), j).astype(BF16)


def _proj(a, w, seg0, nseg, act, name, seg_stride=1, tm=2048, side=None, mod_tail=None):
    m, kdim = a.shape
    tn = SEG
    ni = m // tm
    in_specs = [
        pl.BlockSpec((tm, kdim), lambda j, i: (i, 0)),
        pl.BlockSpec((kdim, tn), lambda j, i: (0, seg0 + j * seg_stride)),
    ]
    out_specs = [pl.BlockSpec((tm, tn), lambda j, i: (i, j))]
    out_shape = [jax.ShapeDtypeStruct((m, nseg * tn), BF16)]
    args = [a, w]
    est = 2 * tm * kdim * 2 + 2 * kdim * tn * 4 + kdim * tn * 2 + 2 * tm * tn * 2 + 8 * PROJ_SUB * tn * 4 + (6 << 20)
    if side is not None:
        rows, cols = side.shape
        blk = rows // (nseg * ni)
        assert blk * nseg * ni == rows and blk % 16 == 0
        side_spec = pl.BlockSpec((blk, cols), lambda j, i: (j * ni + i, 0))
        in_specs.append(side_spec)
        out_specs.append(side_spec)
        out_shape.append(jax.ShapeDtypeStruct(side.shape, BF16))
        args.append(side)
        est += 2 * blk * cols * 6
    if mod_tail is not None:
        c8, w_ada, b_ada, col0 = mod_tail
        dm, n_all = w_ada.shape
        mblk = (n_all - col0) // (nseg * ni)
        assert mblk * nseg * ni == n_all - col0 and mblk % 128 == 0 and col0 % mblk == 0
        in_specs += [
            pl.BlockSpec((8, dm), lambda j, i: (0, 0)),
            pl.BlockSpec((dm, mblk), lambda j, i: (0, col0 // mblk + j * ni + i)),
            pl.BlockSpec((1, mblk), lambda j, i: (0, col0 // mblk + j * ni + i)),
        ]
        out_specs.append(pl.BlockSpec((8, mblk), lambda j, i: (0, j * ni + i)))
        out_shape.append(jax.ShapeDtypeStruct((8, n_all - col0), F32))
        args += [c8, w_ada, b_ada]
        est += 2 * dm * mblk * 4 + dm * mblk * 2
    single = side is None and mod_tail is None
    outs = pl.pallas_call(
        functools.partial(_proj_kernel, act=act, has_side=side is not None, has_mod=mod_tail is not None),
        grid=(nseg, ni),
        in_specs=in_specs,
        out_specs=out_specs,
        out_shape=out_shape,
        scratch_shapes=[pltpu.VMEM((kdim, tn), BF16)],
        compiler_params=pltpu.CompilerParams(
            dimension_semantics=("arbitrary", "arbitrary"),
            vmem_limit_bytes=_vmem_limit(est)),
        name=name,
    )(*args)
    return outs[0] if single else outs


def _hgrn_kernel(q_ref, ffw_ref, fbw_ref, v_ref, og_ref, lbl_ref, gn_ref,
                 o_ref, ofw_scr, obw_scr, st_scr, *, heads_per_step, seq_len):
    t = HGRN_CHUNK
    n_chunks = seq_len // t
    half = t // 2
    f_refs = (ffw_ref, fbw_ref)

    lbl = lbl_ref[...]
    c0s, c1s = [], []
    for d in range(2):
        l0 = lbl[2 * d:2 * d + 1, :]
        l1 = lbl[2 * d + 1:2 * d + 2, :]
        mx = jnp.maximum(l0, l1)
        e0 = jnp.exp(l0 - mx)
        lb = e0 / (e0 + jnp.exp(l1 - mx))
        c0s.append(0.5 * (1.0 + lb))
        c1s.append(0.5 * (1.0 - lb))

    st_scr[...] = jnp.zeros_like(st_scr)

    row = lax.broadcasted_iota(jnp.int32, (t, t), 0)
    col = lax.broadcasted_iota(jnp.int32, (t, t), 1)
    masks = (col <= row, col >= row)
    row2 = lax.broadcasted_iota(jnp.int32, (t, 2 * t), 0)
    col2 = lax.broadcasted_iota(jnp.int32, (t, 2 * t), 1) & (t - 1)
    tris = (jnp.where(col2 <= row2, 1.0, 0.0).astype(BF16), jnp.where(col2 >= row2, 1.0, 0.0).astype(BF16))

    def stage0(c, h, d):
        rows = pl.ds(pl.multiple_of(c * t, t), t)
        hs = slice(h * A_DIM, (h + 1) * A_DIM)
        ct = c1s[d][:, hs] * jnp.tanh(0.5 * f_refs[d][rows, hs].astype(F32))
        k = c1s[d][:, hs] - ct
        lf = jnp.log(c0s[d][:, hs] + ct)
        hi = lf.astype(BF16)
        lo = (lf - hi.astype(F32)).astype(BF16)
        return h, d, rows, hs, k, _dot(tris[d], jnp.concatenate([hi, lo], axis=0))

    def stage1(h, d, rows, hs, k, b):
        b = b * LOG2E
        q = q_ref[rows, hs].astype(F32)
        if d == 0:
            b_end = b[t - 1:t, :]
            b_mid = b[half - 1:half, :]
        else:
            b_end = b[0:1, :]
            b_mid = b[half:half + 1, :]
        qd = (q * jnp.exp2(b)).astype(BF16)
        ke = (k * jnp.exp2(b_end - b)).astype(BF16)
        qm = (q * jnp.exp2(b - b_mid)).astype(BF16)
        km = (k * jnp.exp2(b_mid - b)).astype(BF16)
        return h, d, rows, hs, _dot_nt(qm, km), qd, ke, jnp.exp2(b_end)

    def stage2(finalize, h, d, rows, hs, att, qd, ke, decay):
        v = v_ref[rows, hs]
        att = jnp.where(masks[d], att, 0.0).astype(BF16)
        idx = 2 * h + d
        s_t = st_scr[idx]
        o = _dot(att, v) + _dot_nt(qd, s_t.astype(BF16))
        st_scr[idx] = s_t * decay + _dot_tn(v, ke)
        if finalize:
            other = obw_scr if d == 0 else ofw_scr
            y = _rms(o + other[rows, hs]) * gn_ref[...]
            o_ref[rows, hs] = (y * og_ref[rows, hs].astype(F32)).astype(BF16)
        else:
            o_scr = ofw_scr if d == 0 else obw_scr
            o_scr[rows, hs] = o

    units = [(h, d) for h in range(heads_per_step) for d in range(2)]

    def half_sweep(c0, finalize):
        after0, after1 = {}, {}
        n_half = n_chunks // 2
        for step in range(n_half + 2):
            if step < n_half:
                c = c0 + step
                after0[step] = [stage0(c if d == 0 else n_chunks - 1 - c, h, d) for h, d in units]
            if 1 <= step <= n_half:
                after1[step - 1] = [stage1(*vals) for vals in after0.pop(step - 1)]
            if step >= 2:
                for vals in after1.pop(step - 2):
                    stage2(finalize, *vals)

    half_sweep(0, False)
    half_sweep(n_chunks // 2, True)


def _hgrn(p_lin, p_og, lb_logits4, g_norm, batch, seq_len):
    hps = 2
    wblk = hps * A_DIM
    nblk = A_WIDTH // wblk

    def seg_spec(seg):
        return pl.BlockSpec((seq_len, wblk), lambda b, p: (b, seg * nblk + p))

    est = 2 * 6 * seq_len * wblk * 2 + 2 * seq_len * wblk * 4 + (6 << 20)
    return pl.pallas_call(
        functools.partial(_hgrn_kernel, heads_per_step=hps, seq_len=seq_len),
        grid=(batch, nblk),
        in_specs=[
            seg_spec(0), seg_spec(1), seg_spec(2), seg_spec(3),
            seg_spec(0),
            pl.BlockSpec((4, wblk), lambda b, p: (0, p)),
            pl.BlockSpec((1, A_DIM), lambda b, p: (0, 0)),
        ],
        out_specs=pl.BlockSpec((seq_len, wblk), lambda b, p: (b, p)),
        out_shape=jax.ShapeDtypeStruct((batch * seq_len, A_WIDTH), BF16),
        scratch_shapes=[
            pltpu.VMEM((seq_len, wblk), F32),
            pltpu.VMEM((seq_len, wblk), F32),
            pltpu.VMEM((2 * hps, A_DIM, A_DIM), F32),
        ],
        compiler_params=pltpu.CompilerParams(
            dimension_semantics=("arbitrary", "arbitrary"),
            vmem_limit_bytes=_vmem_limit(est)),
        name="hgrn",
    )(p_lin, p_lin, p_lin, p_lin, p_og, lb_logits4, g_norm.reshape(1, A_DIM))


def _sgu_kernel(u_ref, v_ref, g_ref, ws_ref, bs_ref, o_ref, *, chunks_per_step):
    v = v_ref[...].astype(F32)
    mu = jnp.mean(v, axis=-1, keepdims=True)
    dlt = v - mu
    y = (dlt * lax.rsqrt(jnp.mean(dlt * dlt, axis=-1, keepdims=True) + EPS) * g_ref[...]).astype(BF16)
    for n in range(chunks_per_step):
        rs = slice(n * B_CHUNK, (n + 1) * B_CHUNK)
        for g in range(B_GROUPS):
            cs = slice(g * 128, (g + 1) * 128)
            vm = _dot(ws_ref[g].astype(BF16), y[rs, cs]) + bs_ref[:, cs]
            o_ref[rs, cs] = (u_ref[rs, cs].astype(F32) * vm).astype(BF16)


def _sgu(p_z, g_v, w_s, bias_full):
    m = p_z.shape[0]
    cps = 4
    tm = cps * B_CHUNK
    return pl.pallas_call(
        functools.partial(_sgu_kernel, chunks_per_step=cps),
        grid=(m // tm,),
        in_specs=[
            pl.BlockSpec((tm, B_WIDTH), lambda i: (i, 0)),
            pl.BlockSpec((tm, B_WIDTH), lambda i: (i, 1)),
            pl.BlockSpec((1, B_WIDTH), lambda i: (0, 0)),
            pl.BlockSpec((B_GROUPS, B_CHUNK, B_CHUNK), lambda i: (0, 0, 0)),
            pl.BlockSpec((B_CHUNK, B_WIDTH), lambda i: (0, 0)),
        ],
        out_specs=pl.BlockSpec((tm, B_WIDTH), lambda i: (i, 0)),
        out_shape=jax.ShapeDtypeStruct((m, B_WIDTH), BF16),
        compiler_params=pltpu.CompilerParams(
            dimension_semantics=("arbitrary",),
            vmem_limit_bytes=_vmem_limit(32 << 20)),
        name="sgu",
    )(p_z, p_z, g_v.reshape(1, B_WIDTH), w_s, bias_full)


def _mix_kernel(hg_ref, sg_ref, ga_ref, gb_ref, x_ref, mod_ref, gpost_ref, gnext_ref,
                wa_ref, wb_ref, wo_ref, h_ref, a_ref, *, sub):
    gt1 = mod_ref[0, 0:1, :]
    sh2 = mod_ref[0, 1:2, :]
    sc2 = mod_ref[0, 2:3, :]
    def branches(rs):
        return rs, _dot(hg_ref[rs, :], wa_ref[...]), _dot(sg_ref[rs, :], wb_ref[...])

    def project(rs, ya, yb):
        merged = ga_ref[rs, :].astype(F32) * ya + gb_ref[rs, :].astype(F32) * yb
        return rs, _dot(merged.astype(BF16), wo_ref[...])

    post_scale = gt1 * gpost_ref[...]
    next_scale = gnext_ref[...] * (1.0 + sc2)

    def finish(rs, mo):
        h = x_ref[rs, :] + _rms(mo) * post_scale
        h_ref[rs, :] = h
        a_ref[rs, :] = (_rms(h) * next_scale + sh2).astype(BF16)

    bounds = [0]
    for size in sub:
        bounds.append(bounds[-1] + size)
    assert bounds[-1] == hg_ref.shape[0]
    n_sub = len(sub)
    after0, after1 = {}, {}
    for step in range(n_sub + 2):
        if step < n_sub:
            after0[step] = branches(slice(bounds[step], bounds[step + 1]))
        if 1 <= step <= n_sub:
            after1[step - 1] = project(*after0.pop(step - 1))
        if step >= 2:
            finish(*after1.pop(step - 2))


def _mix(hg, sg, p_gate, x2, mod3, g_post, g_next, wa, wb, wo, rows_per_batch):
    m, d = x2.shape
    tm, sub = 512, (384, 128)
    bpb = rows_per_batch // tm
    const = lambda shape: pl.BlockSpec(shape, lambda i: (0, 0), pipeline_mode=pl.Buffered(1))
    row = lambda width, col: pl.BlockSpec((tm, width), lambda i: (i, col))
    vec_spec = pl.BlockSpec((1, d), lambda i: (0, 0))
    est = (2 * A_WIDTH * d + d * d) * 2 + 2 * tm * (2 * A_WIDTH * 2 + 2 * d * 2 + d * 4 + d * 4 + d * 2) \
        + 6 * tm * d * 4 + (4 << 20)
    return pl.pallas_call(
        functools.partial(_mix_kernel, sub=sub),
        grid=(m // tm,),
        in_specs=[
            row(A_WIDTH, 0), row(B_WIDTH, 0), row(d, 0), row(d, 1), row(d, 0),
            pl.BlockSpec((1, mod3.shape[1], d), lambda i: (i // bpb, 0, 0)),
            vec_spec, vec_spec,
            const((A_WIDTH, d)), const((B_WIDTH, d)), const((d, d)),
        ],
        out_specs=[row(d, 0), row(d, 0)],
        out_shape=[jax.ShapeDtypeStruct((m, d), F32), jax.ShapeDtypeStruct((m, d), BF16)],
        compiler_params=pltpu.CompilerParams(
            dimension_semantics=("arbitrary",),
            vmem_limit_bytes=_vmem_limit(est)),
        name="mix",
    )(hg, sg, p_gate, p_gate, x2, mod3, g_post.reshape(1, d), g_next.reshape(1, d), wa, wb, wo)


def _resproj_kernel(lhs_ref, w_ref, res_hbm, mod_ref, gpost_ref, h_ref, res_scr, res_sem, *, gate_row, sub):
    i = pl.program_id(0)
    k = pl.program_id(1)
    last = pl.num_programs(1) - 1
    tm = h_ref.shape[0]
    res_copy = pltpu.make_async_copy(res_hbm.at[pl.ds(pl.multiple_of(i * tm, tm), tm), :], res_scr, res_sem)

    @pl.when(k == last - 1)
    def _():
        res_copy.start()

    @pl.when(k == 0)
    def _():
        h_ref[...] = _dot(lhs_ref[...], w_ref[...])

    @pl.when((k > 0) & (k < last))
    def _():
        h_ref[...] += _dot(lhs_ref[...], w_ref[...])

    @pl.when(k == last)
    def _():
        res_copy.wait()
        w = w_ref[...]
        scale = mod_ref[0, gate_row:gate_row + 1, :] * gpost_ref[...]
        for s in range(tm // sub):
            rs = slice(s * sub, (s + 1) * sub)
            acc = h_ref[rs, :] + _dot(lhs_ref[rs, :], w)
            h_ref[rs, :] = res_scr[rs, :] + _rms(acc) * scale


def _resproj(lhs, w, res, mod3, g_post, gate_row, rows_per_batch, tm, tk, name):
    m, kdim = lhs.shape
    d = w.shape[1]
    assert w.dtype == BF16 and kdim // tk >= 2 and kdim % tk == 0 and m % tm == 0
    bpb = rows_per_batch // tm
    est = 2 * tm * tk * 2 + 2 * tk * d * 2 + 2 * tm * d * 4 + tm * d * 4 + (12 << 20)
    return pl.pallas_call(
        functools.partial(_resproj_kernel, gate_row=gate_row, sub=128),
        grid=(m // tm, kdim // tk),
        in_specs=[
            pl.BlockSpec((tm, tk), lambda i, k: (i, k)),
            pl.BlockSpec((tk, d), lambda i, k: (k, 0)),
            pl.BlockSpec(memory_space=pl.ANY),
            pl.BlockSpec((1, mod3.shape[1], d), lambda i, k: (i // bpb, 0, 0)),
            pl.BlockSpec((1, d), lambda i, k: (0, 0)),
        ],
        out_specs=pl.BlockSpec((tm, d), lambda i, k: (i, 0)),
        out_shape=jax.ShapeDtypeStruct((m, d), F32),
        scratch_shapes=[pltpu.VMEM((tm, d), F32), pltpu.SemaphoreType.DMA(())],
        compiler_params=pltpu.CompilerParams(
            dimension_semantics=("arbitrary", "arbitrary"),
            vmem_limit_bytes=_vmem_limit(est)),
        name=name,
    )(lhs, w, res, mod3, g_post.reshape(1, d))


def kernel(x, c, w_ada, b_ada, g_pre_mix, g_post_mix, g_pre_ffn, g_post_ffn, w_in, lb_logits,
           g_hgrn_norm, w_a_out, g_sgu_norm, w_spatial, b_spatial, w_b_out, w_o, w_ff1, w_ff2):
    batch, seq_len, d = x.shape
    depth = w_in.shape[0]
    assert depth == 1 and lb_logits.shape == (2, depth + 1, A_WIDTH)
    assert w_in.shape[2] == 11 * SEG and d == 2 * SEG
    assert g_hgrn_norm.shape == (depth, A_DIM) and w_spatial.shape == (depth, B_GROUPS, B_CHUNK, B_CHUNK)
    assert seq_len % 1024 == 0
    m = batch * seq_len

    c8 = jnp.zeros((8, d), F32).at[:batch].set(c)
    b_ada2 = b_ada[0].reshape(1, N_MOD * d)
    n_mix = 2 * d
    mod_mix = _modulation(c8, w_ada[0], b_ada2, n_mix)[:batch].reshape(batch, 2, d)

    x2 = x.reshape(m, d)
    w_in0 = w_in[0]
    a1, p_og = _normproj(x2, mod_mix, g_pre_mix[0], w_in0, SEG_OG, seq_len)
    p_lin, wa, mod_rest = _proj(a1, w_in0, SEG_Q, 4, _act_linear, "inproj_lin", side=w_a_out[0],
                                mod_tail=(c8, w_ada[0], b_ada2, n_mix))
    mod3 = mod_rest[:batch].reshape(batch, N_MOD - 2, d)
    p_z, wb = _proj(a1, w_in0, SEG_Z, 2, _act_gelu, "inproj_gelu", side=w_b_out[0])
    p_gate, wo = _proj(a1, w_in0, SEG_GATE, 4, _act_sigmoid, "inproj_sig", side=w_o[0])

    hg = _hgrn(p_lin, p_og, lb_logits.reshape(2 * (depth + 1), A_WIDTH), g_hgrn_norm[0], batch, seq_len)

    bias_full = jnp.repeat(b_spatial[0].T, 128, axis=1)
    sg = _sgu(p_z, g_sgu_norm[0], w_spatial[0], bias_full)

    h1, a2 = _mix(hg, sg, p_gate, x2, mod3, g_post_mix[0], g_pre_ffn[0], wa, wb, wo, seq_len)
    hid, w2 = _proj(a2, w_ff1[0], 0, w_ff1.shape[2] // SEG, _act_relu2, "ff1", side=w_ff2[0])
    out = _resproj(hid, w2, h1, mod3, g_post_ffn[0], 3, seq_len, tm=1024, tk=2048, name="ff2")
    return out.reshape(batch, seq_len, d)
```

```python
import functools

import jax
import jax.numpy as jnp
from jax import lax
from jax.experimental import pallas as pl
from jax.experimental.pallas import tpu as pltpu

F32 = jnp.float32
BF16 = jnp.bfloat16
EPS = 1e-6
LOG2E = 1.4426950408889634

A_HEADS = 8
A_DIM = 128
A_WIDTH = A_HEADS * A_DIM
B_GROUPS = 8
B_CHUNK = 128
B_WIDTH = B_GROUPS * 128
N_MOD = 6
SEG = 1024
SEG_Q, SEG_F, SEG_V, SEG_OG, SEG_Z, SEG_GATE = 0, 1, 3, 4, 5, 7

HGRN_CHUNK = 64
PROJ_SUB = 128
V7X_VMEM_BYTES = 64 * 1024 * 1024


def _vmem_limit(estimate_bytes):
    return int(min(estimate_bytes, V7X_VMEM_BYTES - 4 * 1024 * 1024))


def _sigmoid(x):
    return 1.0 / (1.0 + jnp.exp(-x))


def _dot(a, b):
    return jnp.dot(a, b, preferred_element_type=F32)


def _dot_nt(a, b):
    return lax.dot_general(a, b, (((1,), (1,)), ((), ())), preferred_element_type=F32)


def _dot_tn(a, b):
    return lax.dot_general(a, b, (((0,), (0,)), ((), ())), preferred_element_type=F32)


def _rms(x):
    return x * lax.rsqrt(jnp.mean(x * x, axis=-1, keepdims=True) + EPS)


def _mod_block(c_ref, w_ref, b_ref):
    c = c_ref[...]
    s = c * _sigmoid(c)
    return _dot(s.astype(BF16), w_ref[...].astype(BF16)) + b_ref[...]


def _mod_kernel(c_ref, w_ref, b_ref, o_ref):
    o_ref[...] = _mod_block(c_ref, w_ref, b_ref)


def _modulation(c8, w_ada, b_ada, n):
    d = w_ada.shape[0]
    tn = 1024
    return pl.pallas_call(
        _mod_kernel,
        grid=(n // tn,),
        in_specs=[
            pl.BlockSpec((8, d), lambda j: (0, 0)),
            pl.BlockSpec((d, tn), lambda j: (0, j)),
            pl.BlockSpec((1, tn), lambda j: (0, j)),
        ],
        out_specs=pl.BlockSpec((8, tn), lambda j: (0, j)),
        out_shape=jax.ShapeDtypeStruct((8, n), F32),
        compiler_params=pltpu.CompilerParams(
            dimension_semantics=("arbitrary",),
            vmem_limit_bytes=_vmem_limit(2 * d * tn * 4 + d * tn * 2 + (8 << 20))),
        name="mod",
    )(c8, w_ada, b_ada)


def _normproj_kernel(x_ref, mod_ref, g_ref, w_ref, a_ref, o_ref, w_scr):
    @pl.when(pl.program_id(0) == 0)
    def _():
        w_scr[...] = w_ref[...].astype(BF16)

    sh = mod_ref[0, 0:1, :]
    scale = g_ref[...] * (1.0 + mod_ref[0, 1:2, :])
    for s in range(x_ref.shape[0] // PROJ_SUB):
        rs = slice(s * PROJ_SUB, (s + 1) * PROJ_SUB)
        a = (_rms(x_ref[rs, :]) * scale + sh).astype(BF16)
        a_ref[rs, :] = a
        o_ref[rs, :] = _act_silu(_dot(a, w_scr[...]), 0).astype(BF16)


def _normproj(x2, mod3, g, w, seg, rows_per_batch):
    m, d = x2.shape
    tm, tn = 1024, SEG
    bpb = rows_per_batch // tm
    est = 2 * tm * d * 6 + 2 * d * tn * 4 + d * tn * 2 + 2 * tm * tn * 2 + 8 * PROJ_SUB * d * 4 + (6 << 20)
    return pl.pallas_call(
        _normproj_kernel,
        grid=(m // tm,),
        in_specs=[
            pl.BlockSpec((tm, d), lambda i: (i, 0)),
            pl.BlockSpec((1, mod3.shape[1], d), lambda i: (i // bpb, 0, 0)),
            pl.BlockSpec((1, d), lambda i: (0, 0)),
            pl.BlockSpec((d, tn), lambda i: (0, seg)),
        ],
        out_specs=[pl.BlockSpec((tm, d), lambda i: (i, 0)), pl.BlockSpec((tm, tn), lambda i: (i, 0))],
        out_shape=[jax.ShapeDtypeStruct((m, d), BF16), jax.ShapeDtypeStruct((m, tn), BF16)],
        scratch_shapes=[pltpu.VMEM((d, tn), BF16)],
        compiler_params=pltpu.CompilerParams(
            dimension_semantics=("arbitrary",),
            vmem_limit_bytes=_vmem_limit(est)),
        name="norm_silu",
    )(x2, mod3, g.reshape(1, d), w)


def _act_linear(acc, j):
    return acc * jnp.where(j == 0, A_DIM ** -0.5, 1.0)


def _act_silu(acc, j):
    return acc * _sigmoid(acc)


def _act_gelu(acc, j):
    return 0.5 * acc * (1.0 + lax.erf(acc * (2.0 ** -0.5)))


def _act_sigmoid(acc, j):
    return _sigmoid(acc)


def _act_relu2(acc, j):
    r = jnp.maximum(acc, 0.0)
    return r * r


def _proj_kernel(a_ref, w_ref, *refs, act, has_side, has_mod):
    refs = list(refs)
    side_ref = refs.pop(0) if has_side else None
    mod_in = [refs.pop(0) for _ in range(3)] if has_mod else None
    o_ref = refs.pop(0)
    if has_side:
        refs.pop(0)[...] = side_ref[...].astype(BF16)
    if has_mod:
        refs.pop(0)[...] = _mod_block(*mod_in)
    (w_scr,) = refs
    j = pl.program_id(0)

    @pl.when(pl.program_id(1) == 0)
    def _():
        w_scr[...] = w_ref[...].astype(BF16)

    for s in range(a_ref.shape[0] // PROJ_SUB):
        rs = slice(s * PROJ_SUB, (s + 1) * PROJ_SUB)
        o_ref[rs, :] = act(_dot(a_ref[rs, :], w_scr[...]), j).astype(BF16)


def _proj(a, w, seg0, nseg, act, name, seg_stride=1, tm=2048, side=None, mod_tail=None):
    m, kdim = a.shape
    tn = SEG
    ni = m // tm
    in_specs = [
        pl.BlockSpec((tm, kdim), lambda j, i: (i, 0)),
        pl.BlockSpec((kdim, tn), lambda j, i: (0, seg0 + j * seg_stride)),
    ]
    out_specs = [pl.BlockSpec((tm, tn), lambda j, i: (i, j))]
    out_shape = [jax.ShapeDtypeStruct((m, nseg * tn), BF16)]
    args = [a, w]
    est = 2 * tm * kdim * 2 + 2 * kdim * tn * 4 + kdim * tn * 2 + 2 * tm * tn * 2 + 8 * PROJ_SUB * tn * 4 + (6 << 20)
    if side is not None:
        rows, cols = side.shape
        blk = rows // (nseg * ni)
        assert blk * nseg * ni == rows and blk % 16 == 0
        side_spec = pl.BlockSpec((blk, cols), lambda j, i: (j * ni + i, 0))
        in_specs.append(side_spec)
        out_specs.append(side_spec)
        out_shape.append(jax.ShapeDtypeStruct(side.shape, BF16))
        args.append(side)
        est += 2 * blk * cols * 6
    if mod_tail is not None:
        c8, w_ada, b_ada, col0 = mod_tail
        dm, n_all = w_ada.shape
        mblk = (n_all - col0) // (nseg * ni)
        assert mblk * nseg * ni == n_all - col0 and mblk % 128 == 0 and col0 % mblk == 0
        in_specs += [
            pl.BlockSpec((8, dm), lambda j, i: (0, 0)),
            pl.BlockSpec((dm, mblk), lambda j, i: (0, col0 // mblk + j * ni + i)),
            pl.BlockSpec((1, mblk), lambda j, i: (0, col0 // mblk + j * ni + i)),
        ]
        out_specs.append(pl.BlockSpec((8, mblk), lambda j, i: (0, j * ni + i)))
        out_shape.append(jax.ShapeDtypeStruct((8, n_all - col0), F32))
        args += [c8, w_ada, b_ada]
        est += 2 * dm * mblk * 4 + dm * mblk * 2
    single = side is None and mod_tail is None
    outs = pl.pallas_call(
        functools.partial(_proj_kernel, act=act, has_side=side is not None, has_mod=mod_tail is not None),
        grid=(nseg, ni),
        in_specs=in_specs,
        out_specs=out_specs,
        out_shape=out_shape,
        scratch_shapes=[pltpu.VMEM((kdim, tn), BF16)],
        compiler_params=pltpu.CompilerParams(
            dimension_semantics=("arbitrary", "arbitrary"),
            vmem_limit_bytes=_vmem_limit(est)),
        name=name,
    )(*args)
    return outs[0] if single else outs


def _hgrn_kernel(q_ref, ffw_ref, fbw_ref, v_ref, og_ref, lbl_ref, gn_ref,
                 o_ref, ofw_scr, obw_scr, st_scr, *, heads_per_step, seq_len):
    t = HGRN_CHUNK
    n_chunks = seq_len // t
    half = t // 2
    f_refs = (ffw_ref, fbw_ref)

    lbl = lbl_ref[...]
    c0s, c1s = [], []
    for d in range(2):
        l0 = lbl[2 * d:2 * d + 1, :]
        l1 = lbl[2 * d + 1:2 * d + 2, :]
        mx = jnp.maximum(l0, l1)
        e0 = jnp.exp(l0 - mx)
        lb = e0 / (e0 + jnp.exp(l1 - mx))
        c0s.append(0.5 * (1.0 + lb))
        c1s.append(0.5 * (1.0 - lb))

    st_scr[...] = jnp.zeros_like(st_scr)

    row = lax.broadcasted_iota(jnp.int32, (t, t), 0)
    col = lax.broadcasted_iota(jnp.int32, (t, t), 1)
    masks = (col <= row, col >= row)
    row2 = lax.broadcasted_iota(jnp.int32, (t, 2 * t), 0)
    col2 = lax.broadcasted_iota(jnp.int32, (t, 2 * t), 1) & (t - 1)
    tris = (jnp.where(col2 <= row2, 1.0, 0.0).astype(BF16), jnp.where(col2 >= row2, 1.0, 0.0).astype(BF16))

    def stage0(c, h, d):
        rows = pl.ds(pl.multiple_of(c * t, t), t)
        hs = slice(h * A_DIM, (h + 1) * A_DIM)
        ct = c1s[d][:, hs] * jnp.tanh(0.5 * f_refs[d][rows, hs].astype(F32))
        k = c1s[d][:, hs] - ct
        lf = jnp.log(c0s[d][:, hs] + ct)
        hi = lf.astype(BF16)
        lo = (lf - hi.astype(F32)).astype(BF16)
        return h, d, rows, hs, k, _dot(tris[d], jnp.concatenate([hi, lo], axis=0))

    def stage1(h, d, rows, hs, k, b):
        b = b * LOG2E
        q = q_ref[rows, hs].astype(F32)
        if d == 0:
            b_end = b[t - 1:t, :]
            b_mid = b[half - 1:half, :]
        else:
            b_end = b[0:1, :]
            b_mid = b[half:half + 1, :]
        qd = (q * jnp.exp2(b)).astype(BF16)
        ke = (k * jnp.exp2(b_end - b)).astype(BF16)
        qm = (q * jnp.exp2(b - b_mid)).astype(BF16)
        km = (k * jnp.exp2(b_mid - b)).astype(BF16)
        return h, d, rows, hs, _dot_nt(qm, km), qd, ke, jnp.exp2(b_end)

    def stage2(finalize, h, d, rows, hs, att, qd, ke, decay):
        v = v_ref[rows, hs]
        att = jnp.where(masks[d], att, 0.0).astype(BF16)
        idx = 2 * h + d
        s_t = st_scr[idx]
        o = _dot(att, v) + _dot_nt(qd, s_t.astype(BF16))
        st_scr[idx] = s_t * decay + _dot_tn(v, ke)
        if finalize:
            other = obw_scr if d == 0 else ofw_scr
            y = _rms(o + other[rows, hs]) * gn_ref[...]
            o_ref[rows, hs] = (y * og_ref[rows, hs].astype(F32)).astype(BF16)
        else:
            o_scr = ofw_scr if d == 0 else obw_scr
            o_scr[rows, hs] = o

    units = [(h, d) for h in range(heads_per_step) for d in range(2)]

    def half_sweep(c0, finalize):
        after0, after1 = {}, {}
        n_half = n_chunks // 2
        for step in range(n_half + 2):
            if step < n_half:
                c = c0 + step
                after0[step] = [stage0(c if d == 0 else n_chunks - 1 - c, h, d) for h, d in units]
            if 1 <= step <= n_half:
                after1[step - 1] = [stage1(*vals) for vals in after0.pop(step - 1)]
            if step >= 2:
                for vals in after1.pop(step - 2):
                    stage2(finalize, *vals)

    half_sweep(0, False)
    half_sweep(n_chunks // 2, True)


def _hgrn(p_lin, p_og, lb_logits4, g_norm, batch, seq_len):
    hps = 2
    wblk = hps * A_DIM
    nblk = A_WIDTH // wblk

    def seg_spec(seg):
        return pl.BlockSpec((seq_len, wblk), lambda b, p: (b, seg * nblk + p))

    est = 2 * 6 * seq_len * wblk * 2 + 2 * seq_len * wblk * 4 + (6 << 20)
    return pl.pallas_call(
        functools.partial(_hgrn_kernel, heads_per_step=hps, seq_len=seq_len),
        grid=(batch, nblk),
        in_specs=[
            seg_spec(0), seg_spec(1), seg_spec(2), seg_spec(3),
            seg_spec(0),
            pl.BlockSpec((4, wblk), lambda b, p: (0, p)),
            pl.BlockSpec((1, A_DIM), lambda b, p: (0, 0)),
        ],
        out_specs=pl.BlockSpec((seq_len, wblk), lambda b, p: (b, p)),
        out_shape=jax.ShapeDtypeStruct((batch * seq_len, A_WIDTH), BF16),
        scratch_shapes=[
            pltpu.VMEM((seq_len, wblk), F32),
            pltpu.VMEM((seq_len, wblk), F32),
            pltpu.VMEM((2 * hps, A_DIM, A_DIM), F32),
        ],
        compiler_params=pltpu.CompilerParams(
            dimension_semantics=("arbitrary", "arbitrary"),
            vmem_limit_bytes=_vmem_limit(est)),
        name="hgrn",
    )(p_lin, p_lin, p_lin, p_lin, p_og, lb_logits4, g_norm.reshape(1, A_DIM))


def _sgu_kernel(u_ref, v_ref, g_ref, ws_ref, bs_ref, o_ref, *, chunks_per_step):
    v = v_ref[...].astype(F32)
    mu = jnp.mean(v, axis=-1, keepdims=True)
    dlt = v - mu
    y = (dlt * lax.rsqrt(jnp.mean(dlt * dlt, axis=-1, keepdims=True) + EPS) * g_ref[...]).astype(BF16)
    for n in range(chunks_per_step):
        rs = slice(n * B_CHUNK, (n + 1) * B_CHUNK)
        for g in range(B_GROUPS):
            cs = slice(g * 128, (g + 1) * 128)
            vm = _dot(ws_ref[g].astype(BF16), y[rs, cs]) + bs_ref[:, cs]
            o_ref[rs, cs] = (u_ref[rs, cs].astype(F32) * vm).astype(BF16)


def _sgu(p_z, g_v, w_s, bias_full):
    m = p_z.shape[0]
    cps = 8
    tm = cps * B_CHUNK
    return pl.pallas_call(
        functools.partial(_sgu_kernel, chunks_per_step=cps),
        grid=(m // tm,),
        in_specs=[
            pl.BlockSpec((tm, B_WIDTH), lambda i: (i, 0)),
            pl.BlockSpec((tm, B_WIDTH), lambda i: (i, 1)),
            pl.BlockSpec((1, B_WIDTH), lambda i: (0, 0)),
            pl.BlockSpec((B_GROUPS, B_CHUNK, B_CHUNK), lambda i: (0, 0, 0)),
            pl.BlockSpec((B_CHUNK, B_WIDTH), lambda i: (0, 0)),
        ],
        out_specs=pl.BlockSpec((tm, B_WIDTH), lambda i: (i, 0)),
        out_shape=jax.ShapeDtypeStruct((m, B_WIDTH), BF16),
        compiler_params=pltpu.CompilerParams(
            dimension_semantics=("arbitrary",),
            vmem_limit_bytes=_vmem_limit(32 << 20)),
        name="sgu",
    )(p_z, p_z, g_v.reshape(1, B_WIDTH), w_s, bias_full)


def _mix_kernel(hg_ref, sg_ref, ga_ref, gb_ref, x_ref, mod_ref, gpost_ref, gnext_ref,
                wa_ref, wb_ref, wo_ref, h_ref, a_ref, *, sub):
    gt1 = mod_ref[0, 0:1, :]
    sh2 = mod_ref[0, 1:2, :]
    sc2 = mod_ref[0, 2:3, :]
    def branches(rs):
        return rs, _dot(hg_ref[rs, :], wa_ref[...]), _dot(sg_ref[rs, :], wb_ref[...])

    def project(rs, ya, yb):
        merged = ga_ref[rs, :].astype(F32) * ya + gb_ref[rs, :].astype(F32) * yb
        return rs, _dot(merged.astype(BF16), wo_ref[...])

    post_scale = gt1 * gpost_ref[...]
    next_scale = gnext_ref[...] * (1.0 + sc2)

    def finish(rs, mo):
        h = x_ref[rs, :] + _rms(mo) * post_scale
        h_ref[rs, :] = h
        a_ref[rs, :] = (_rms(h) * next_scale + sh2).astype(BF16)

    bounds = [0]
    for size in sub:
        bounds.append(bounds[-1] + size)
    assert bounds[-1] == hg_ref.shape[0]
    n_sub = len(sub)
    after0, after1 = {}, {}
    for step in range(n_sub + 2):
        if step < n_sub:
            after0[step] = branches(slice(bounds[step], bounds[step + 1]))
        if 1 <= step <= n_sub:
            after1[step - 1] = project(*after0.pop(step - 1))
        if step >= 2:
            finish(*after1.pop(step - 2))


def _mix(hg, sg, p_gate, x2, mod3, g_post, g_next, wa, wb, wo, rows_per_batch):
    m, d = x2.shape
    tm, sub = 512, (384, 128)
    bpb = rows_per_batch // tm
    const = lambda shape: pl.BlockSpec(shape, lambda i: (0, 0), pipeline_mode=pl.Buffered(1))
    row = lambda width, col: pl.BlockSpec((tm, width), lambda i: (i, col))
    vec_spec = pl.BlockSpec((1, d), lambda i: (0, 0))
    est = (2 * A_WIDTH * d + d * d) * 2 + 2 * tm * (2 * A_WIDTH * 2 + 2 * d * 2 + d * 4 + d * 4 + d * 2) \
        + 6 * tm * d * 4 + (4 << 20)
    return pl.pallas_call(
        functools.partial(_mix_kernel, sub=sub),
        grid=(m // tm,),
        in_specs=[
            row(A_WIDTH, 0), row(B_WIDTH, 0), row(d, 0), row(d, 1), row(d, 0),
            pl.BlockSpec((1, mod3.shape[1], d), lambda i: (i // bpb, 0, 0)),
            vec_spec, vec_spec,
            const((A_WIDTH, d)), const((B_WIDTH, d)), const((d, d)),
        ],
        out_specs=[row(d, 0), row(d, 0)],
        out_shape=[jax.ShapeDtypeStruct((m, d), F32), jax.ShapeDtypeStruct((m, d), BF16)],
        compiler_params=pltpu.CompilerParams(
            dimension_semantics=("arbitrary",),
            vmem_limit_bytes=_vmem_limit(est)),
        name="mix",
    )(hg, sg, p_gate, p_gate, x2, mod3, g_post.reshape(1, d), g_next.reshape(1, d), wa, wb, wo)


def _resproj_kernel(lhs_ref, w_ref, res_hbm, mod_ref, gpost_ref, h_ref, res_scr, res_sem, *, gate_row, sub):
    i = pl.program_id(0)
    k = pl.program_id(1)
    last = pl.num_programs(1) - 1
    tm = h_ref.shape[0]
    res_copy = pltpu.make_async_copy(res_hbm.at[pl.ds(pl.multiple_of(i * tm, tm), tm), :], res_scr, res_sem)

    @pl.when(k == last - 1)
    def _():
        res_copy.start()

    @pl.when(k == 0)
    def _():
        h_ref[...] = _dot(lhs_ref[...], w_ref[...])

    @pl.when((k > 0) & (k < last))
    def _():
        h_ref[...] += _dot(lhs_ref[...], w_ref[...])

    @pl.when(k == last)
    def _():
        res_copy.wait()
        w = w_ref[...]
        scale = mod_ref[0, gate_row:gate_row + 1, :] * gpost_ref[...]
        for s in range(tm // sub):
            rs = slice(s * sub, (s + 1) * sub)
            acc = h_ref[rs, :] + _dot(lhs_ref[rs, :], w)
            h_ref[rs, :] = res_scr[rs, :] + _rms(acc) * scale


def _resproj(lhs, w, res, mod3, g_post, gate_row, rows_per_batch, tm, tk, name):
    m, kdim = lhs.shape
    d = w.shape[1]
    assert w.dtype == BF16 and kdim // tk >= 2 and kdim % tk == 0 and m % tm == 0
    bpb = rows_per_batch // tm
    est = 2 * tm * tk * 2 + 2 * tk * d * 2 + 2 * tm * d * 4 + tm * d * 4 + (12 << 20)
    return pl.pallas_call(
        functools.partial(_resproj_kernel, gate_row=gate_row, sub=128),
        grid=(m // tm, kdim // tk),
        in_specs=[
            pl.BlockSpec((tm, tk), lambda i, k: (i, k)),
            pl.BlockSpec((tk, d), lambda i, k: (k, 0)),
            pl.BlockSpec(memory_space=pl.ANY),
            pl.BlockSpec((1, mod3.shape[1], d), lambda i, k: (i // bpb, 0, 0)),
            pl.BlockSpec((1, d), lambda i, k: (0, 0)),
        ],
        out_specs=pl.BlockSpec((tm, d), lambda i, k: (i, 0)),
        out_shape=jax.ShapeDtypeStruct((m, d), F32),
        scratch_shapes=[pltpu.VMEM((tm, d), F32), pltpu.SemaphoreType.DMA(())],
        compiler_params=pltpu.CompilerParams(
            dimension_semantics=("arbitrary", "arbitrary"),
            vmem_limit_bytes=_vmem_limit(est)),
        name=name,
    )(lhs, w, res, mod3, g_post.reshape(1, d))


def kernel(x, c, w_ada, b_ada, g_pre_mix, g_post_mix, g_pre_ffn, g_post_ffn, w_in, lb_logits,
           g_hgrn_norm, w_a_out, g_sgu_norm, w_spatial, b_spatial, w_b_out, w_o, w_ff1, w_ff2):
    batch, seq_len, d = x.shape
    depth = w_in.shape[0]
    assert depth == 1 and lb_logits.shape == (2, depth + 1, A_WIDTH)
    assert w_in.shape[2] == 11 * SEG and d == 2 * SEG
    assert g_hgrn_norm.shape == (depth, A_DIM) and w_spatial.shape == (depth, B_GROUPS, B_CHUNK, B_CHUNK)
    assert seq_len % 1024 == 0
    m = batch * seq_len

    c8 = jnp.zeros((8, d), F32).at[:batch].set(c)
    b_ada2 = b_ada[0].reshape(1, N_MOD * d)
    n_mix = 2 * d
    mod_mix = _modulation(c8, w_ada[0], b_ada2, n_mix)[:batch].reshape(batch, 2, d)

    x2 = x.reshape(m, d)
    w_in0 = w_in[0]
    a1, p_og = _normproj(x2, mod_mix, g_pre_mix[0], w_in0, SEG_OG, seq_len)
    p_lin, wa, mod_rest = _proj(a1, w_in0, SEG_Q, 4, _act_linear, "inproj_lin", side=w_a_out[0],
                                mod_tail=(c8, w_ada[0], b_ada2, n_mix))
    mod3 = mod_rest[:batch].reshape(batch, N_MOD - 2, d)
    p_z, wb = _proj(a1, w_in0, SEG_Z, 2, _act_gelu, "inproj_gelu", side=w_b_out[0])
    p_gate, wo = _proj(a1, w_in0, SEG_GATE, 4, _act_sigmoid, "inproj_sig", side=w_o[0])

    hg = _hgrn(p_lin, p_og, lb_logits.reshape(2 * (depth + 1), A_WIDTH), g_hgrn_norm[0], batch, seq_len)

    bias_full = jnp.repeat(b_spatial[0].T, 128, axis=1)
    sg = _sgu(p_z, g_sgu_norm[0], w_spatial[0], bias_full)

    h1, a2 = _mix(hg, sg, p_gate, x2, mod3, g_post_mix[0], g_pre_ffn[0], wa, wb, wo, seq_len)
    hid, w2 = _proj(a2, w_ff1[0], 0, w_ff1.shape[2] // SEG, _act_relu2, "ff1", side=w_ff2[0])
    out = _resproj(hid, w2, h1, mod3, g_post_ffn[0], 3, seq_len, tm=1024, tk=2048, name="ff2")
    return out.reshape(batch, seq_len, d)
```

```python
import functools

import jax
import jax.numpy as jnp
from jax import lax
from jax.experimental import pallas as pl
from jax.experimental.pallas import tpu as pltpu

F32 = jnp.float32
BF16 = jnp.bfloat16
EPS = 1e-6
LOG2E = 1.4426950408889634

A_HEADS = 8
A_DIM = 128
A_WIDTH = A_HEADS * A_DIM
B_GROUPS = 8
B_CHUNK = 128
B_WIDTH = B_GROUPS * 128
N_MOD = 6
SEG = 1024
SEG_Q, SEG_F, SEG_V, SEG_OG, SEG_Z, SEG_GATE = 0, 1, 3, 4, 5, 7

HGRN_CHUNK = 64
PROJ_SUB = 128
V7X_VMEM_BYTES = 64 * 1024 * 1024


def _vmem_limit(estimate_bytes):
    return int(min(estimate_bytes, V7X_VMEM_BYTES - 4 * 1024 * 1024))


def _sigmoid(x):
    return 1.0 / (1.0 + jnp.exp(-x))


def _dot(a, b):
    return jnp.dot(a, b, preferred_element_type=F32)


def _dot_nt(a, b):
    return lax.dot_general(a, b, (((1,), (1,)), ((), ())), preferred_element_type=F32)


def _dot_tn(a, b):
    return lax.dot_general(a, b, (((0,), (0,)), ((), ())), preferred_element_type=F32)


def _rms(x):
    return x * lax.rsqrt(jnp.mean(x * x, axis=-1, keepdims=True) + EPS)


def _mod_block(c_ref, w_ref, b_ref):
    c = c_ref[...]
    s = c * _sigmoid(c)
    return _dot(s.astype(BF16), w_ref[...].astype(BF16)) + b_ref[...]


def _mod_kernel(c_ref, w_ref, b_ref, o_ref):
    o_ref[...] = _mod_block(c_ref, w_ref, b_ref)


def _modulation(c8, w_ada, b_ada, n):
    d = w_ada.shape[0]
    tn = 512
    return pl.pallas_call(
        _mod_kernel,
        grid=(n // tn,),
        in_specs=[
            pl.BlockSpec((8, d), lambda j: (0, 0)),
            pl.BlockSpec((d, tn), lambda j: (0, j)),
            pl.BlockSpec((1, tn), lambda j: (0, j)),
        ],
        out_specs=pl.BlockSpec((8, tn), lambda j: (0, j)),
        out_shape=jax.ShapeDtypeStruct((8, n), F32),
        compiler_params=pltpu.CompilerParams(
            dimension_semantics=("arbitrary",),
            vmem_limit_bytes=_vmem_limit(2 * d * tn * 4 + d * tn * 2 + (8 << 20))),
        name="mod",
    )(c8, w_ada, b_ada)


def _normproj_kernel(x_ref, mod_ref, g_ref, w_ref, a_ref, o_ref, w_scr):
    @pl.when(pl.program_id(0) == 0)
    def _():
        w_scr[...] = w_ref[...].astype(BF16)

    sh = mod_ref[0, 0:1, :]
    scale = g_ref[...] * (1.0 + mod_ref[0, 1:2, :])
    for s in range(x_ref.shape[0] // PROJ_SUB):
        rs = slice(s * PROJ_SUB, (s + 1) * PROJ_SUB)
        a = (_rms(x_ref[rs, :]) * scale + sh).astype(BF16)
        a_ref[rs, :] = a
        o_ref[rs, :] = _act_silu(_dot(a, w_scr[...]), 0).astype(BF16)


def _normproj(x2, mod3, g, w, seg, rows_per_batch):
    m, d = x2.shape
    tm, tn = 1024, SEG
    bpb = rows_per_batch // tm
    est = 2 * tm * d * 6 + 2 * d * tn * 4 + d * tn * 2 + 2 * tm * tn * 2 + 8 * PROJ_SUB * d * 4 + (6 << 20)
    return pl.pallas_call(
        _normproj_kernel,
        grid=(m // tm,),
        in_specs=[
            pl.BlockSpec((tm, d), lambda i: (i, 0)),
            pl.BlockSpec((1, mod3.shape[1], d), lambda i: (i // bpb, 0, 0)),
            pl.BlockSpec((1, d), lambda i: (0, 0)),
            pl.BlockSpec((d, tn), lambda i: (0, seg)),
        ],
        out_specs=[pl.BlockSpec((tm, d), lambda i: (i, 0)), pl.BlockSpec((tm, tn), lambda i: (i, 0))],
        out_shape=[jax.ShapeDtypeStruct((m, d), BF16), jax.ShapeDtypeStruct((m, tn), BF16)],
        scratch_shapes=[pltpu.VMEM((d, tn), BF16)],
        compiler_params=pltpu.CompilerParams(
            dimension_semantics=("arbitrary",),
            vmem_limit_bytes=_vmem_limit(est)),
        name="norm_silu",
    )(x2, mod3, g.reshape(1, d), w)


def _act_linear(acc, j):
    return acc * jnp.where(j == 0, A_DIM ** -0.5, 1.0)


def _act_silu(acc, j):
    return acc * _sigmoid(acc)


def _act_gelu(acc, j):
    return 0.5 * acc * (1.0 + lax.erf(acc * (2.0 ** -0.5)))


def _act_sigmoid(acc, j):
    return _sigmoid(acc)


def _act_relu2(acc, j):
    r = jnp.maximum(acc, 0.0)
    return r * r


def _proj_kernel(a_ref, w_ref, *refs, act, has_side, has_mod):
    refs = list(refs)
    side_ref = refs.pop(0) if has_side else None
    mod_in = [refs.pop(0) for _ in range(3)] if has_mod else None
    o_ref = refs.pop(0)
    if has_side:
        refs.pop(0)[...] = side_ref[...].astype(BF16)
    if has_mod:
        refs.pop(0)[...] = _mod_block(*mod_in)
    (w_scr,) = refs
    j = pl.program_id(0)

    @pl.when(pl.program_id(1) == 0)
    def _():
        w_scr[...] = w_ref[...].astype(BF16)

    for s in range(a_ref.shape[0] // PROJ_SUB):
        rs = slice(s * PROJ_SUB, (s + 1) * PROJ_SUB)
        o_ref[rs, :] = act(_dot(a_ref[rs, :], w_scr[...]), j).astype(BF16)


def _proj(a, w, seg0, nseg, act, name, seg_stride=1, tm=2048, side=None, mod_tail=None):
    m, kdim = a.shape
    tn = SEG
    ni = m // tm
    in_specs = [
        pl.BlockSpec((tm, kdim), lambda j, i: (i, 0)),
        pl.BlockSpec((kdim, tn), lambda j, i: (0, seg0 + j * seg_stride)),
    ]
    out_specs = [pl.BlockSpec((tm, tn), lambda j, i: (i, j))]
    out_shape = [jax.ShapeDtypeStruct((m, nseg * tn), BF16)]
    args = [a, w]
    est = 2 * tm * kdim * 2 + 2 * kdim * tn * 4 + kdim * tn * 2 + 2 * tm * tn * 2 + 8 * PROJ_SUB * tn * 4 + (6 << 20)
    if side is not None:
        rows, cols = side.shape
        blk = rows // (nseg * ni)
        assert blk * nseg * ni == rows and blk % 16 == 0
        side_spec = pl.BlockSpec((blk, cols), lambda j, i: (j * ni + i, 0))
        in_specs.append(side_spec)
        out_specs.append(side_spec)
        out_shape.append(jax.ShapeDtypeStruct(side.shape, BF16))
        args.append(side)
        est += 2 * blk * cols * 6
    if mod_tail is not None:
        c8, w_ada, b_ada, col0 = mod_tail
        dm, n_all = w_ada.shape
        mblk = (n_all - col0) // (nseg * ni)
        assert mblk * nseg * ni == n_all - col0 and mblk % 128 == 0 and col0 % mblk == 0
        in_specs += [
            pl.BlockSpec((8, dm), lambda j, i: (0, 0)),
            pl.BlockSpec((dm, mblk), lambda j, i: (0, col0 // mblk + j * ni + i)),
            pl.BlockSpec((1, mblk), lambda j, i: (0, col0 // mblk + j * ni + i)),
        ]
        out_specs.append(pl.BlockSpec((8, mblk), lambda j, i: (0, j * ni + i)))
        out_shape.append(jax.ShapeDtypeStruct((8, n_all - col0), F32))
        args += [c8, w_ada, b_ada]
        est += 2 * dm * mblk * 4 + dm * mblk * 2
    single = side is None and mod_tail is None
    outs = pl.pallas_call(
        functools.partial(_proj_kernel, act=act, has_side=side is not None, has_mod=mod_tail is not None),
        grid=(nseg, ni),
        in_specs=in_specs,
        out_specs=out_specs,
        out_shape=out_shape,
        scratch_shapes=[pltpu.VMEM((kdim, tn), BF16)],
        compiler_params=pltpu.CompilerParams(
            dimension_semantics=("arbitrary", "arbitrary"),
            vmem_limit_bytes=_vmem_limit(est)),
        name=name,
    )(*args)
    return outs[0] if single else outs


def _hgrn_kernel(q_ref, ffw_ref, fbw_ref, v_ref, og_ref, lbl_ref, gn_ref,
                 o_ref, ofw_scr, obw_scr, st_scr, *, heads_per_step, seq_len):
    t = HGRN_CHUNK
    n_chunks = seq_len // t
    half = t // 2
    f_refs = (ffw_ref, fbw_ref)

    lbl = lbl_ref[...]
    c0s, c1s = [], []
    for d in range(2):
        l0 = lbl[2 * d:2 * d + 1, :]
        l1 = lbl[2 * d + 1:2 * d + 2, :]
        mx = jnp.maximum(l0, l1)
        e0 = jnp.exp(l0 - mx)
        lb = e0 / (e0 + jnp.exp(l1 - mx))
        c0s.append(0.5 * (1.0 + lb))
        c1s.append(0.5 * (1.0 - lb))

    st_scr[...] = jnp.zeros_like(st_scr)

    row = lax.broadcasted_iota(jnp.int32, (t, t), 0)
    col = lax.broadcasted_iota(jnp.int32, (t, t), 1)
    masks = (col <= row, col >= row)
    row2 = lax.broadcasted_iota(jnp.int32, (t, 2 * t), 0)
    col2 = lax.broadcasted_iota(jnp.int32, (t, 2 * t), 1) & (t - 1)
    tris = (jnp.where(col2 <= row2, 1.0, 0.0).astype(BF16), jnp.where(col2 >= row2, 1.0, 0.0).astype(BF16))

    def stage0(c, h, d):
        rows = pl.ds(pl.multiple_of(c * t, t), t)
        hs = slice(h * A_DIM, (h + 1) * A_DIM)
        ct = c1s[d][:, hs] * jnp.tanh(0.5 * f_refs[d][rows, hs].astype(F32))
        k = c1s[d][:, hs] - ct
        lf = jnp.log(c0s[d][:, hs] + ct)
        hi = lf.astype(BF16)
        lo = (lf - hi.astype(F32)).astype(BF16)
        return h, d, rows, hs, k, _dot(tris[d], jnp.concatenate([hi, lo], axis=0))

    def stage1(h, d, rows, hs, k, b):
        b = b * LOG2E
        q = q_ref[rows, hs].astype(F32)
        if d == 0:
            b_end = b[t - 1:t, :]
            b_mid = b[half - 1:half, :]
        else:
            b_end = b[0:1, :]
            b_mid = b[half:half + 1, :]
        qd = (q * jnp.exp2(b)).astype(BF16)
        ke = (k * jnp.exp2(b_end - b)).astype(BF16)
        qm = (q * jnp.exp2(b - b_mid)).astype(BF16)
        km = (k * jnp.exp2(b_mid - b)).astype(BF16)
        return h, d, rows, hs, _dot_nt(qm, km), qd, ke, jnp.exp2(b_end)

    def stage2(finalize, h, d, rows, hs, att, qd, ke, decay):
        v = v_ref[rows, hs]
        att = jnp.where(masks[d], att, 0.0).astype(BF16)
        idx = 2 * h + d
        s_t = st_scr[idx]
        o = _dot(att, v) + _dot_nt(qd, s_t.astype(BF16))
        st_scr[idx] = s_t * decay + _dot_tn(v, ke)
        if finalize:
            other = obw_scr if d == 0 else ofw_scr
            y = _rms(o + other[rows, hs]) * gn_ref[...]
            o_ref[rows, hs] = (y * og_ref[rows, hs].astype(F32)).astype(BF16)
        else:
            o_scr = ofw_scr if d == 0 else obw_scr
            o_scr[rows, hs] = o

    units = [(h, d) for h in range(heads_per_step) for d in range(2)]

    def half_sweep(c0, finalize):
        after0, after1 = {}, {}
        n_half = n_chunks // 2
        for step in range(n_half + 2):
            if step < n_half:
                c = c0 + step
                after0[step] = [stage0(c if d == 0 else n_chunks - 1 - c, h, d) for h, d in units]
            if 1 <= step <= n_half:
                after1[step - 1] = [stage1(*vals) for vals in after0.pop(step - 1)]
            if step >= 2:
                for vals in after1.pop(step - 2):
                    stage2(finalize, *vals)

    half_sweep(0, False)
    half_sweep(n_chunks // 2, True)


def _hgrn(p_lin, p_og, lb_logits4, g_norm, batch, seq_len):
    hps = 2
    wblk = hps * A_DIM
    nblk = A_WIDTH // wblk

    def seg_spec(seg):
        return pl.BlockSpec((seq_len, wblk), lambda b, p: (b, seg * nblk + p))

    est = 2 * 6 * seq_len * wblk * 2 + 2 * seq_len * wblk * 4 + (6 << 20)
    return pl.pallas_call(
        functools.partial(_hgrn_kernel, heads_per_step=hps, seq_len=seq_len),
        grid=(batch, nblk),
        in_specs=[
            seg_spec(0), seg_spec(1), seg_spec(2), seg_spec(3),
            seg_spec(0),
            pl.BlockSpec((4, wblk), lambda b, p: (0, p)),
            pl.BlockSpec((1, A_DIM), lambda b, p: (0, 0)),
        ],
        out_specs=pl.BlockSpec((seq_len, wblk), lambda b, p: (b, p)),
        out_shape=jax.ShapeDtypeStruct((batch * seq_len, A_WIDTH), BF16),
        scratch_shapes=[
            pltpu.VMEM((seq_len, wblk), F32),
            pltpu.VMEM((seq_len, wblk), F32),
            pltpu.VMEM((2 * hps, A_DIM, A_DIM), F32),
        ],
        compiler_params=pltpu.CompilerParams(
            dimension_semantics=("arbitrary", "arbitrary"),
            vmem_limit_bytes=_vmem_limit(est)),
        name="hgrn",
    )(p_lin, p_lin, p_lin, p_lin, p_og, lb_logits4, g_norm.reshape(1, A_DIM))


def _sgu_kernel(u_ref, v_ref, g_ref, ws_ref, bs_ref, o_ref, *, chunks_per_step):
    v = v_ref[...].astype(F32)
    mu = jnp.mean(v, axis=-1, keepdims=True)
    dlt = v - mu
    y = (dlt * lax.rsqrt(jnp.mean(dlt * dlt, axis=-1, keepdims=True) + EPS) * g_ref[...]).astype(BF16)
    for n in range(chunks_per_step):
        rs = slice(n * B_CHUNK, (n + 1) * B_CHUNK)
        for g in range(B_GROUPS):
            cs = slice(g * 128, (g + 1) * 128)
            vm = _dot(ws_ref[g].astype(BF16), y[rs, cs]) + bs_ref[:, cs]
            o_ref[rs, cs] = (u_ref[rs, cs].astype(F32) * vm).astype(BF16)


def _sgu(p_z, g_v, w_s, bias_full):
    m = p_z.shape[0]
    cps = 8
    tm = cps * B_CHUNK
    return pl.pallas_call(
        functools.partial(_sgu_kernel, chunks_per_step=cps),
        grid=(m // tm,),
        in_specs=[
            pl.BlockSpec((tm, B_WIDTH), lambda i: (i, 0)),
            pl.BlockSpec((tm, B_WIDTH), lambda i: (i, 1)),
            pl.BlockSpec((1, B_WIDTH), lambda i: (0, 0)),
            pl.BlockSpec((B_GROUPS, B_CHUNK, B_CHUNK), lambda i: (0, 0, 0)),
            pl.BlockSpec((B_CHUNK, B_WIDTH), lambda i: (0, 0)),
        ],
        out_specs=pl.BlockSpec((tm, B_WIDTH), lambda i: (i, 0)),
        out_shape=jax.ShapeDtypeStruct((m, B_WIDTH), BF16),
        compiler_params=pltpu.CompilerParams(
            dimension_semantics=("arbitrary",),
            vmem_limit_bytes=_vmem_limit(32 << 20)),
        name="sgu",
    )(p_z, p_z, g_v.reshape(1, B_WIDTH), w_s, bias_full)


def _mix_kernel(hg_ref, sg_ref, ga_ref, gb_ref, x_ref, mod_ref, gpost_ref, gnext_ref,
                wa_ref, wb_ref, wo_ref, h_ref, a_ref, *, sub):
    gt1 = mod_ref[0, 0:1, :]
    sh2 = mod_ref[0, 1:2, :]
    sc2 = mod_ref[0, 2:3, :]
    def branches(rs):
        return rs, _dot(hg_ref[rs, :], wa_ref[...]), _dot(sg_ref[rs, :], wb_ref[...])

    def project(rs, ya, yb):
        merged = ga_ref[rs, :].astype(F32) * ya + gb_ref[rs, :].astype(F32) * yb
        return rs, _dot(merged.astype(BF16), wo_ref[...])

    post_scale = gt1 * gpost_ref[...]
    next_scale = gnext_ref[...] * (1.0 + sc2)

    def finish(rs, mo):
        h = x_ref[rs, :] + _rms(mo) * post_scale
        h_ref[rs, :] = h
        a_ref[rs, :] = (_rms(h) * next_scale + sh2).astype(BF16)

    bounds = [0]
    for size in sub:
        bounds.append(bounds[-1] + size)
    assert bounds[-1] == hg_ref.shape[0]
    n_sub = len(sub)
    after0, after1 = {}, {}
    for step in range(n_sub + 2):
        if step < n_sub:
            after0[step] = branches(slice(bounds[step], bounds[step + 1]))
        if 1 <= step <= n_sub:
            after1[step - 1] = project(*after0.pop(step - 1))
        if step >= 2:
            finish(*after1.pop(step - 2))


def _mix(hg, sg, p_gate, x2, mod3, g_post, g_next, wa, wb, wo, rows_per_batch):
    m, d = x2.shape
    tm, sub = 512, (384, 128)
    bpb = rows_per_batch // tm
    const = lambda shape: pl.BlockSpec(shape, lambda i: (0, 0), pipeline_mode=pl.Buffered(1))
    row = lambda width, col: pl.BlockSpec((tm, width), lambda i: (i, col))
    vec_spec = pl.BlockSpec((1, d), lambda i: (0, 0))
    est = (2 * A_WIDTH * d + d * d) * 2 + 2 * tm * (2 * A_WIDTH * 2 + 2 * d * 2 + d * 4 + d * 4 + d * 2) \
        + 6 * tm * d * 4 + (4 << 20)
    return pl.pallas_call(
        functools.partial(_mix_kernel, sub=sub),
        grid=(m // tm,),
        in_specs=[
            row(A_WIDTH, 0), row(B_WIDTH, 0), row(d, 0), row(d, 1), row(d, 0),
            pl.BlockSpec((1, mod3.shape[1], d), lambda i: (i // bpb, 0, 0)),
            vec_spec, vec_spec,
            const((A_WIDTH, d)), const((B_WIDTH, d)), const((d, d)),
        ],
        out_specs=[row(d, 0), row(d, 0)],
        out_shape=[jax.ShapeDtypeStruct((m, d), F32), jax.ShapeDtypeStruct((m, d), BF16)],
        compiler_params=pltpu.CompilerParams(
            dimension_semantics=("arbitrary",),
            vmem_limit_bytes=_vmem_limit(est)),
        name="mix",
    )(hg, sg, p_gate, p_gate, x2, mod3, g_post.reshape(1, d), g_next.reshape(1, d), wa, wb, wo)


def _resproj_kernel(lhs_ref, w_ref, res_hbm, mod_ref, gpost_ref, h_ref, res_scr, res_sem, *, gate_row, sub):
    i = pl.program_id(0)
    k = pl.program_id(1)
    last = pl.num_programs(1) - 1
    tm = h_ref.shape[0]
    res_copy = pltpu.make_async_copy(res_hbm.at[pl.ds(pl.multiple_of(i * tm, tm), tm), :], res_scr, res_sem)

    @pl.when(k == last - 1)
    def _():
        res_copy.start()

    @pl.when(k == 0)
    def _():
        h_ref[...] = _dot(lhs_ref[...], w_ref[...])

    @pl.when((k > 0) & (k < last))
    def _():
        h_ref[...] += _dot(lhs_ref[...], w_ref[...])

    @pl.when(k == last)
    def _():
        res_copy.wait()
        w = w_ref[...]
        scale = mod_ref[0, gate_row:gate_row + 1, :] * gpost_ref[...]
        for s in range(tm // sub):
            rs = slice(s * sub, (s + 1) * sub)
            acc = h_ref[rs, :] + _dot(lhs_ref[rs, :], w)
            h_ref[rs, :] = res_scr[rs, :] + _rms(acc) * scale


def _resproj(lhs, w, res, mod3, g_post, gate_row, rows_per_batch, tm, tk, name):
    m, kdim = lhs.shape
    d = w.shape[1]
    assert w.dtype == BF16 and kdim // tk >= 2 and kdim % tk == 0 and m % tm == 0
    bpb = rows_per_batch // tm
    est = 2 * tm * tk * 2 + 2 * tk * d * 2 + 2 * tm * d * 4 + tm * d * 4 + (12 << 20)
    return pl.pallas_call(
        functools.partial(_resproj_kernel, gate_row=gate_row, sub=128),
        grid=(m // tm, kdim // tk),
        in_specs=[
            pl.BlockSpec((tm, tk), lambda i, k: (i, k)),
            pl.BlockSpec((tk, d), lambda i, k: (k, 0)),
            pl.BlockSpec(memory_space=pl.ANY),
            pl.BlockSpec((1, mod3.shape[1], d), lambda i, k: (i // bpb, 0, 0)),
            pl.BlockSpec((1, d), lambda i, k: (0, 0)),
        ],
        out_specs=pl.BlockSpec((tm, d), lambda i, k: (i, 0)),
        out_shape=jax.ShapeDtypeStruct((m, d), F32),
        scratch_shapes=[pltpu.VMEM((tm, d), F32), pltpu.SemaphoreType.DMA(())],
        compiler_params=pltpu.CompilerParams(
            dimension_semantics=("arbitrary", "arbitrary"),
            vmem_limit_bytes=_vmem_limit(est)),
        name=name,
    )(lhs, w, res, mod3, g_post.reshape(1, d))


def kernel(x, c, w_ada, b_ada, g_pre_mix, g_post_mix, g_pre_ffn, g_post_ffn, w_in, lb_logits,
           g_hgrn_norm, w_a_out, g_sgu_norm, w_spatial, b_spatial, w_b_out, w_o, w_ff1, w_ff2):
    batch, seq_len, d = x.shape
    depth = w_in.shape[0]
    assert depth == 1 and lb_logits.shape == (2, depth + 1, A_WIDTH)
    assert w_in.shape[2] == 11 * SEG and d == 2 * SEG
    assert g_hgrn_norm.shape == (depth, A_DIM) and w_spatial.shape == (depth, B_GROUPS, B_CHUNK, B_CHUNK)
    assert seq_len % 1024 == 0
    m = batch * seq_len

    c8 = jnp.zeros((8, d), F32).at[:batch].set(c)
    b_ada2 = b_ada[0].reshape(1, N_MOD * d)
    n_mix = 2 * d
    mod_mix = _modulation(c8, w_ada[0], b_ada2, n_mix)[:batch].reshape(batch, 2, d)

    x2 = x.reshape(m, d)
    w_in0 = w_in[0]
    a1, p_og = _normproj(x2, mod_mix, g_pre_mix[0], w_in0, SEG_OG, seq_len)
    p_lin, wa, mod_rest = _proj(a1, w_in0, SEG_Q, 4, _act_linear, "inproj_lin", side=w_a_out[0],
                                mod_tail=(c8, w_ada[0], b_ada2, n_mix))
    mod3 = mod_rest[:batch].reshape(batch, N_MOD - 2, d)
    p_z, wb = _proj(a1, w_in0, SEG_Z, 2, _act_gelu, "inproj_gelu", side=w_b_out[0])
    p_gate, wo = _proj(a1, w_in0, SEG_GATE, 4, _act_sigmoid, "inproj_sig", side=w_o[0])

    hg = _hgrn(p_lin, p_og, lb_logits.reshape(2 * (depth + 1), A_WIDTH), g_hgrn_norm[0], batch, seq_len)

    bias_full = jnp.repeat(b_spatial[0].T, 128, axis=1)
    sg = _sgu(p_z, g_sgu_norm[0], w_spatial[0], bias_full)

    h1, a2 = _mix(hg, sg, p_gate, x2, mod3, g_post_mix[0], g_pre_ffn[0], wa, wb, wo, seq_len)
    hid, w2 = _proj(a2, w_ff1[0], 0, w_ff1.shape[2] // SEG, _act_relu2, "ff1", side=w_ff2[0])
    out = _resproj(hid, w2, h1, mod3, g_post_ffn[0], 3, seq_len, tm=1024, tk=2048, name="ff2")
    return out.reshape(batch, seq_len, d)
```

```python
import functools

import jax
import jax.numpy as jnp
from jax import lax
from jax.experimental import pallas as pl
from jax.experimental.pallas import tpu as pltpu

F32 = jnp.float32
BF16 = jnp.bfloat16
EPS = 1e-6
LOG2E = 1.4426950408889634

A_HEADS = 8
A_DIM = 128
A_WIDTH = A_HEADS * A_DIM
B_GROUPS = 8
B_CHUNK = 128
B_WIDTH = B_GROUPS * 128
N_MOD = 6
SEG = 1024
SEG_Q, SEG_F, SEG_V, SEG_OG, SEG_Z, SEG_GATE = 0, 1, 3, 4, 5, 7

HGRN_CHUNK = 64
PROJ_SUB = 128
V7X_VMEM_BYTES = 64 * 1024 * 1024


def _vmem_limit(estimate_bytes):
    return int(min(estimate_bytes, V7X_VMEM_BYTES - 4 * 1024 * 1024))


def _sigmoid(x):
    return 1.0 / (1.0 + jnp.exp(-x))


def _dot(a, b):
    return jnp.dot(a, b, preferred_element_type=F32)


def _dot_nt(a, b):
    return lax.dot_general(a, b, (((1,), (1,)), ((), ())), preferred_element_type=F32)


def _dot_tn(a, b):
    return lax.dot_general(a, b, (((0,), (0,)), ((), ())), preferred_element_type=F32)


def _rms(x):
    return x * lax.rsqrt(jnp.mean(x * x, axis=-1, keepdims=True) + EPS)


def _mod_block(c_ref, w_ref, b_ref):
    c = c_ref[...]
    s = c * _sigmoid(c)
    return _dot(s.astype(BF16), w_ref[...].astype(BF16)) + b_ref[...]


def _mod_kernel(c_ref, w_ref, b_ref, o_ref):
    o_ref[...] = _mod_block(c_ref, w_ref, b_ref)


def _modulation(c8, w_ada, b_ada, n):
    d = w_ada.shape[0]
    tn = 1024
    return pl.pallas_call(
        _mod_kernel,
        grid=(n // tn,),
        in_specs=[
            pl.BlockSpec((8, d), lambda j: (0, 0)),
            pl.BlockSpec((d, tn), lambda j: (0, j)),
            pl.BlockSpec((1, tn), lambda j: (0, j)),
        ],
        out_specs=pl.BlockSpec((8, tn), lambda j: (0, j)),
        out_shape=jax.ShapeDtypeStruct((8, n), F32),
        compiler_params=pltpu.CompilerParams(
            dimension_semantics=("arbitrary",),
            vmem_limit_bytes=_vmem_limit(2 * d * tn * 4 + d * tn * 2 + (8 << 20))),
        name="mod",
    )(c8, w_ada, b_ada)


def _normproj_kernel(x_ref, mod_ref, g_ref, w_ref, a_ref, o_ref, w_scr):
    @pl.when(pl.program_id(0) == 0)
    def _():
        w_scr[...] = w_ref[...].astype(BF16)

    sh = mod_ref[0, 0:1, :]
    scale = g_ref[...] * (1.0 + mod_ref[0, 1:2, :])
    for s in range(x_ref.shape[0] // PROJ_SUB):
        rs = slice(s * PROJ_SUB, (s + 1) * PROJ_SUB)
        a = (_rms(x_ref[rs, :]) * scale + sh).astype(BF16)
        a_ref[rs, :] = a
        o_ref[rs, :] = _act_silu(_dot(a, w_scr[...]), 0).astype(BF16)


def _normproj(x2, mod3, g, w, seg, rows_per_batch):
    m, d = x2.shape
    tm, tn = 1024, SEG
    bpb = rows_per_batch // tm
    est = 2 * tm * d * 6 + 2 * d * tn * 4 + d * tn * 2 + 2 * tm * tn * 2 + 8 * PROJ_SUB * d * 4 + (6 << 20)
    return pl.pallas_call(
        _normproj_kernel,
        grid=(m // tm,),
        in_specs=[
            pl.BlockSpec((tm, d), lambda i: (i, 0)),
            pl.BlockSpec((1, mod3.shape[1], d), lambda i: (i // bpb, 0, 0)),
            pl.BlockSpec((1, d), lambda i: (0, 0)),
            pl.BlockSpec((d, tn), lambda i: (0, seg)),
        ],
        out_specs=[pl.BlockSpec((tm, d), lambda i: (i, 0)), pl.BlockSpec((tm, tn), lambda i: (i, 0))],
        out_shape=[jax.ShapeDtypeStruct((m, d), BF16), jax.ShapeDtypeStruct((m, tn), BF16)],
        scratch_shapes=[pltpu.VMEM((d, tn), BF16)],
        compiler_params=pltpu.CompilerParams(
            dimension_semantics=("arbitrary",),
            vmem_limit_bytes=_vmem_limit(est)),
        name="norm_silu",
    )(x2, mod3, g.reshape(1, d), w)


def _act_linear(acc, j):
    return acc * jnp.where(j == 0, A_DIM ** -0.5, 1.0)


def _act_silu(acc, j):
    return acc * _sigmoid(acc)


def _act_gelu(acc, j):
    return 0.5 * acc * (1.0 + lax.erf(acc * (2.0 ** -0.5)))


def _act_sigmoid(acc, j):
    return _sigmoid(acc)


def _act_relu2(acc, j):
    r = jnp.maximum(acc, 0.0)
    return r * r


def _proj_kernel(a_ref, w_ref, *refs, act, has_side, has_mod):
    refs = list(refs)
    side_ref = refs.pop(0) if has_side else None
    mod_in = [refs.pop(0) for _ in range(3)] if has_mod else None
    o_ref = refs.pop(0)
    if has_side:
        refs.pop(0)[...] = side_ref[...].astype(BF16)
    if has_mod:
        refs.pop(0)[...] = _mod_block(*mod_in)
    (w_scr,) = refs
    j = pl.program_id(0)

    @pl.when(pl.program_id(1) == 0)
    def _():
        w_scr[...] = w_ref[...].astype(BF16)

    for s in range(a_ref.shape[0] // PROJ_SUB):
        rs = slice(s * PROJ_SUB, (s + 1) * PROJ_SUB)
        o_ref[rs, :] = act(_dot(a_ref[rs, :], w_scr[...]), j).astype(BF16)


def _proj(a, w, seg0, nseg, act, name, seg_stride=1, tm=2048, side=None, mod_tail=None):
    m, kdim = a.shape
    tn = SEG
    ni = m // tm
    in_specs = [
        pl.BlockSpec((tm, kdim), lambda j, i: (i, 0)),
        pl.BlockSpec((kdim, tn), lambda j, i: (0, seg0 + j * seg_stride)),
    ]
    out_specs = [pl.BlockSpec((tm, tn), lambda j, i: (i, j))]
    out_shape = [jax.ShapeDtypeStruct((m, nseg * tn), BF16)]
    args = [a, w]
    est = 2 * tm * kdim * 2 + 2 * kdim * tn * 4 + kdim * tn * 2 + 2 * tm * tn * 2 + 8 * PROJ_SUB * tn * 4 + (6 << 20)
    if side is not None:
        rows, cols = side.shape
        blk = rows // (nseg * ni)
        assert blk * nseg * ni == rows and blk % 16 == 0
        side_spec = pl.BlockSpec((blk, cols), lambda j, i: (j * ni + i, 0))
        in_specs.append(side_spec)
        out_specs.append(side_spec)
        out_shape.append(jax.ShapeDtypeStruct(side.shape, BF16))
        args.append(side)
        est += 2 * blk * cols * 6
    if mod_tail is not None:
        c8, w_ada, b_ada, col0 = mod_tail
        dm, n_all = w_ada.shape
        mblk = (n_all - col0) // (nseg * ni)
        assert mblk * nseg * ni == n_all - col0 and mblk % 128 == 0 and col0 % mblk == 0
        in_specs += [
            pl.BlockSpec((8, dm), lambda j, i: (0, 0)),
            pl.BlockSpec((dm, mblk), lambda j, i: (0, col0 // mblk + j * ni + i)),
            pl.BlockSpec((1, mblk), lambda j, i: (0, col0 // mblk + j * ni + i)),
        ]
        out_specs.append(pl.BlockSpec((8, mblk), lambda j, i: (0, j * ni + i)))
        out_shape.append(jax.ShapeDtypeStruct((8, n_all - col0), F32))
        args += [c8, w_ada, b_ada]
        est += 2 * dm * mblk * 4 + dm * mblk * 2
    single = side is None and mod_tail is None
    outs = pl.pallas_call(
        functools.partial(_proj_kernel, act=act, has_side=side is not None, has_mod=mod_tail is not None),
        grid=(nseg, ni),
        in_specs=in_specs,
        out_specs=out_specs,
        out_shape=out_shape,
        scratch_shapes=[pltpu.VMEM((kdim, tn), BF16)],
        compiler_params=pltpu.CompilerParams(
            dimension_semantics=("arbitrary", "arbitrary"),
            vmem_limit_bytes=_vmem_limit(est)),
        name=name,
    )(*args)
    return outs[0] if single else outs


def _hgrn_kernel(q_ref, ffw_ref, fbw_ref, v_ref, og_ref, lbl_ref, gn_ref,
                 o_ref, ofw_scr, obw_scr, st_scr, *, heads_per_step, seq_len):
    t = HGRN_CHUNK
    n_chunks = seq_len // t
    half = t // 2
    f_refs = (ffw_ref, fbw_ref)

    lbl = lbl_ref[...]
    c0s, c1s = [], []
    for d in range(2):
        l0 = lbl[2 * d:2 * d + 1, :]
        l1 = lbl[2 * d + 1:2 * d + 2, :]
        mx = jnp.maximum(l0, l1)
        e0 = jnp.exp(l0 - mx)
        lb = e0 / (e0 + jnp.exp(l1 - mx))
        c0s.append(0.5 * (1.0 + lb))
        c1s.append(0.5 * (1.0 - lb))

    st_scr[...] = jnp.zeros_like(st_scr)

    row = lax.broadcasted_iota(jnp.int32, (t, t), 0)
    col = lax.broadcasted_iota(jnp.int32, (t, t), 1)
    masks = (col <= row, col >= row)
    row2 = lax.broadcasted_iota(jnp.int32, (t, 2 * t), 0)
    col2 = lax.broadcasted_iota(jnp.int32, (t, 2 * t), 1) & (t - 1)
    tris = (jnp.where(col2 <= row2, 1.0, 0.0).astype(BF16), jnp.where(col2 >= row2, 1.0, 0.0).astype(BF16))

    def stage0(c, h, d):
        rows = pl.ds(pl.multiple_of(c * t, t), t)
        hs = slice(h * A_DIM, (h + 1) * A_DIM)
        ct = c1s[d][:, hs] * jnp.tanh(0.5 * f_refs[d][rows, hs].astype(F32))
        k = c1s[d][:, hs] - ct
        lf = jnp.log(c0s[d][:, hs] + ct)
        hi = lf.astype(BF16)
        lo = (lf - hi.astype(F32)).astype(BF16)
        return h, d, rows, hs, k, _dot(tris[d], jnp.concatenate([hi, lo], axis=0))

    def stage1(h, d, rows, hs, k, b):
        b = b * LOG2E
        q = q_ref[rows, hs].astype(F32)
        if d == 0:
            b_end = b[t - 1:t, :]
            b_mid = b[half - 1:half, :]
        else:
            b_end = b[0:1, :]
            b_mid = b[half:half + 1, :]
        qd = (q * jnp.exp2(b)).astype(BF16)
        ke = (k * jnp.exp2(b_end - b)).astype(BF16)
        qm = (q * jnp.exp2(b - b_mid)).astype(BF16)
        km = (k * jnp.exp2(b_mid - b)).astype(BF16)
        return h, d, rows, hs, _dot_nt(qm, km), qd, ke, jnp.exp2(b_end)

    def stage2(finalize, h, d, rows, hs, att, qd, ke, decay):
        v = v_ref[rows, hs]
        att = jnp.where(masks[d], att, 0.0).astype(BF16)
        idx = 2 * h + d
        s_t = st_scr[idx]
        o = _dot(att, v) + _dot_nt(qd, s_t.astype(BF16))
        st_scr[idx] = s_t * decay + _dot_tn(v, ke)
        if finalize:
            other = obw_scr if d == 0 else ofw_scr
            y = _rms(o + other[rows, hs]) * gn_ref[...]
            o_ref[rows, hs] = (y * og_ref[rows, hs].astype(F32)).astype(BF16)
        else:
            o_scr = ofw_scr if d == 0 else obw_scr
            o_scr[rows, hs] = o

    units = [(h, d) for h in range(heads_per_step) for d in range(2)]

    def half_sweep(c0, finalize):
        after0 = {}
        n_half = n_chunks // 2
        for step in range(n_half + 1):
            if step < n_half:
                c = c0 + step
                after0[step] = [stage0(c if d == 0 else n_chunks - 1 - c, h, d) for h, d in units]
            if step >= 1:
                mids = [stage1(*vals) for vals in after0.pop(step - 1)]
                for vals in mids:
                    stage2(finalize, *vals)

    half_sweep(0, False)
    half_sweep(n_chunks // 2, True)


def _hgrn(p_lin, p_og, lb_logits4, g_norm, batch, seq_len):
    hps = 2
    wblk = hps * A_DIM
    nblk = A_WIDTH // wblk

    def seg_spec(seg):
        return pl.BlockSpec((seq_len, wblk), lambda b, p: (b, seg * nblk + p))

    est = 2 * 6 * seq_len * wblk * 2 + 2 * seq_len * wblk * 4 + (6 << 20)
    return pl.pallas_call(
        functools.partial(_hgrn_kernel, heads_per_step=hps, seq_len=seq_len),
        grid=(batch, nblk),
        in_specs=[
            seg_spec(0), seg_spec(1), seg_spec(2), seg_spec(3),
            seg_spec(0),
            pl.BlockSpec((4, wblk), lambda b, p: (0, p)),
            pl.BlockSpec((1, A_DIM), lambda b, p: (0, 0)),
        ],
        out_specs=pl.BlockSpec((seq_len, wblk), lambda b, p: (b, p)),
        out_shape=jax.ShapeDtypeStruct((batch * seq_len, A_WIDTH), BF16),
        scratch_shapes=[
            pltpu.VMEM((seq_len, wblk), F32),
            pltpu.VMEM((seq_len, wblk), F32),
            pltpu.VMEM((2 * hps, A_DIM, A_DIM), F32),
        ],
        compiler_params=pltpu.CompilerParams(
            dimension_semantics=("arbitrary", "arbitrary"),
            vmem_limit_bytes=_vmem_limit(est)),
        name="hgrn",
    )(p_lin, p_lin, p_lin, p_lin, p_og, lb_logits4, g_norm.reshape(1, A_DIM))


def _sgu_kernel(u_ref, v_ref, g_ref, ws_ref, bs_ref, o_ref, *, chunks_per_step):
    v = v_ref[...].astype(F32)
    mu = jnp.mean(v, axis=-1, keepdims=True)
    dlt = v - mu
    y = (dlt * lax.rsqrt(jnp.mean(dlt * dlt, axis=-1, keepdims=True) + EPS) * g_ref[...]).astype(BF16)
    for n in range(chunks_per_step):
        rs = slice(n * B_CHUNK, (n + 1) * B_CHUNK)
        for g in range(B_GROUPS):
            cs = slice(g * 128, (g + 1) * 128)
            vm = _dot(ws_ref[g].astype(BF16), y[rs, cs]) + bs_ref[:, cs]
            o_ref[rs, cs] = (u_ref[rs, cs].astype(F32) * vm).astype(BF16)


def _sgu(p_z, g_v, w_s, bias_full):
    m = p_z.shape[0]
    cps = 8
    tm = cps * B_CHUNK
    return pl.pallas_call(
        functools.partial(_sgu_kernel, chunks_per_step=cps),
        grid=(m // tm,),
        in_specs=[
            pl.BlockSpec((tm, B_WIDTH), lambda i: (i, 0)),
            pl.BlockSpec((tm, B_WIDTH), lambda i: (i, 1)),
            pl.BlockSpec((1, B_WIDTH), lambda i: (0, 0)),
            pl.BlockSpec((B_GROUPS, B_CHUNK, B_CHUNK), lambda i: (0, 0, 0)),
            pl.BlockSpec((B_CHUNK, B_WIDTH), lambda i: (0, 0)),
        ],
        out_specs=pl.BlockSpec((tm, B_WIDTH), lambda i: (i, 0)),
        out_shape=jax.ShapeDtypeStruct((m, B_WIDTH), BF16),
        compiler_params=pltpu.CompilerParams(
            dimension_semantics=("arbitrary",),
            vmem_limit_bytes=_vmem_limit(32 << 20)),
        name="sgu",
    )(p_z, p_z, g_v.reshape(1, B_WIDTH), w_s, bias_full)


def _mix_kernel(hg_ref, sg_ref, ga_ref, gb_ref, x_ref, mod_ref, gpost_ref, gnext_ref,
                wa_ref, wb_ref, wo_ref, h_ref, a_ref, *, sub):
    gt1 = mod_ref[0, 0:1, :]
    sh2 = mod_ref[0, 1:2, :]
    sc2 = mod_ref[0, 2:3, :]
    def branches(rs):
        return rs, _dot(hg_ref[rs, :], wa_ref[...]), _dot(sg_ref[rs, :], wb_ref[...])

    def project(rs, ya, yb):
        merged = ga_ref[rs, :].astype(F32) * ya + gb_ref[rs, :].astype(F32) * yb
        return rs, _dot(merged.astype(BF16), wo_ref[...])

    post_scale = gt1 * gpost_ref[...]
    next_scale = gnext_ref[...] * (1.0 + sc2)

    def finish(rs, mo):
        h = x_ref[rs, :] + _rms(mo) * post_scale
        h_ref[rs, :] = h
        a_ref[rs, :] = (_rms(h) * next_scale + sh2).astype(BF16)

    bounds = [0]
    for size in sub:
        bounds.append(bounds[-1] + size)
    assert bounds[-1] == hg_ref.shape[0]
    n_sub = len(sub)
    after0, after1 = {}, {}
    for step in range(n_sub + 2):
        if step < n_sub:
            after0[step] = branches(slice(bounds[step], bounds[step + 1]))
        if 1 <= step <= n_sub:
            after1[step - 1] = project(*after0.pop(step - 1))
        if step >= 2:
            finish(*after1.pop(step - 2))


def _mix(hg, sg, p_gate, x2, mod3, g_post, g_next, wa, wb, wo, rows_per_batch):
    m, d = x2.shape
    tm, sub = 512, (384, 128)
    bpb = rows_per_batch // tm
    const = lambda shape: pl.BlockSpec(shape, lambda i: (0, 0), pipeline_mode=pl.Buffered(1))
    row = lambda width, col: pl.BlockSpec((tm, width), lambda i: (i, col))
    vec_spec = pl.BlockSpec((1, d), lambda i: (0, 0))
    est = (2 * A_WIDTH * d + d * d) * 2 + 2 * tm * (2 * A_WIDTH * 2 + 2 * d * 2 + d * 4 + d * 4 + d * 2) \
        + 6 * tm * d * 4 + (4 << 20)
    return pl.pallas_call(
        functools.partial(_mix_kernel, sub=sub),
        grid=(m // tm,),
        in_specs=[
            row(A_WIDTH, 0), row(B_WIDTH, 0), row(d, 0), row(d, 1), row(d, 0),
            pl.BlockSpec((1, mod3.shape[1], d), lambda i: (i // bpb, 0, 0)),
            vec_spec, vec_spec,
            const((A_WIDTH, d)), const((B_WIDTH, d)), const((d, d)),
        ],
        out_specs=[row(d, 0), row(d, 0)],
        out_shape=[jax.ShapeDtypeStruct((m, d), F32), jax.ShapeDtypeStruct((m, d), BF16)],
        compiler_params=pltpu.CompilerParams(
            dimension_semantics=("arbitrary",),
            vmem_limit_bytes=_vmem_limit(est)),
        name="mix",
    )(hg, sg, p_gate, p_gate, x2, mod3, g_post.reshape(1, d), g_next.reshape(1, d), wa, wb, wo)


def _resproj_kernel(lhs_ref, w_ref, res_hbm, mod_ref, gpost_ref, h_ref, res_scr, res_sem, *, gate_row, sub):
    i = pl.program_id(0)
    k = pl.program_id(1)
    last = pl.num_programs(1) - 1
    tm = h_ref.shape[0]
    res_copy = pltpu.make_async_copy(res_hbm.at[pl.ds(pl.multiple_of(i * tm, tm), tm), :], res_scr, res_sem)

    @pl.when(k == last - 1)
    def _():
        res_copy.start()

    @pl.when(k == 0)
    def _():
        h_ref[...] = _dot(lhs_ref[...], w_ref[...])

    @pl.when((k > 0) & (k < last))
    def _():
        h_ref[...] += _dot(lhs_ref[...], w_ref[...])

    @pl.when(k == last)
    def _():
        res_copy.wait()
        w = w_ref[...]
        scale = mod_ref[0, gate_row:gate_row + 1, :] * gpost_ref[...]
        for s in range(tm // sub):
            rs = slice(s * sub, (s + 1) * sub)
            acc = h_ref[rs, :] + _dot(lhs_ref[rs, :], w)
            h_ref[rs, :] = res_scr[rs, :] + _rms(acc) * scale


def _resproj(lhs, w, res, mod3, g_post, gate_row, rows_per_batch, tm, tk, name):
    m, kdim = lhs.shape
    d = w.shape[1]
    assert w.dtype == BF16 and kdim // tk >= 2 and kdim % tk == 0 and m % tm == 0
    bpb = rows_per_batch // tm
    est = 2 * tm * tk * 2 + 2 * tk * d * 2 + 2 * tm * d * 4 + tm * d * 4 + (12 << 20)
    return pl.pallas_call(
        functools.partial(_resproj_kernel, gate_row=gate_row, sub=128),
        grid=(m // tm, kdim // tk),
        in_specs=[
            pl.BlockSpec((tm, tk), lambda i, k: (i, k)),
            pl.BlockSpec((tk, d), lambda i, k: (k, 0)),
            pl.BlockSpec(memory_space=pl.ANY),
            pl.BlockSpec((1, mod3.shape[1], d), lambda i, k: (i // bpb, 0, 0)),
            pl.BlockSpec((1, d), lambda i, k: (0, 0)),
        ],
        out_specs=pl.BlockSpec((tm, d), lambda i, k: (i, 0)),
        out_shape=jax.ShapeDtypeStruct((m, d), F32),
        scratch_shapes=[pltpu.VMEM((tm, d), F32), pltpu.SemaphoreType.DMA(())],
        compiler_params=pltpu.CompilerParams(
            dimension_semantics=("arbitrary", "arbitrary"),
            vmem_limit_bytes=_vmem_limit(est)),
        name=name,
    )(lhs, w, res, mod3, g_post.reshape(1, d))


def kernel(x, c, w_ada, b_ada, g_pre_mix, g_post_mix, g_pre_ffn, g_post_ffn, w_in, lb_logits,
           g_hgrn_norm, w_a_out, g_sgu_norm, w_spatial, b_spatial, w_b_out, w_o, w_ff1, w_ff2):
    batch, seq_len, d = x.shape
    depth = w_in.shape[0]
    assert depth == 1 and lb_logits.shape == (2, depth + 1, A_WIDTH)
    assert w_in.shape[2] == 11 * SEG and d == 2 * SEG
    assert g_hgrn_norm.shape == (depth, A_DIM) and w_spatial.shape == (depth, B_GROUPS, B_CHUNK, B_CHUNK)
    assert seq_len % 1024 == 0
    m = batch * seq_len

    c8 = jnp.zeros((8, d), F32).at[:batch].set(c)
    b_ada2 = b_ada[0].reshape(1, N_MOD * d)
    n_mix = 2 * d
    mod_mix = _modulation(c8, w_ada[0], b_ada2, n_mix)[:batch].reshape(batch, 2, d)

    x2 = x.reshape(m, d)
    w_in0 = w_in[0]
    a1, p_og = _normproj(x2, mod_mix, g_pre_mix[0], w_in0, SEG_OG, seq_len)
    p_lin, wa, mod_rest = _proj(a1, w_in0, SEG_Q, 4, _act_linear, "inproj_lin", side=w_a_out[0],
                                mod_tail=(c8, w_ada[0], b_ada2, n_mix))
    mod3 = mod_rest[:batch].reshape(batch, N_MOD - 2, d)
    p_z, wb = _proj(a1, w_in0, SEG_Z, 2, _act_gelu, "inproj_gelu", side=w_b_out[0])
    p_gate, wo = _proj(a1, w_in0, SEG_GATE, 4, _act_sigmoid, "inproj_sig", side=w_o[0])

    hg = _hgrn(p_lin, p_og, lb_logits.reshape(2 * (depth + 1), A_WIDTH), g_hgrn_norm[0], batch, seq_len)

    bias_full = jnp.repeat(b_spatial[0].T, 128, axis=1)
    sg = _sgu(p_z, g_sgu_norm[0], w_spatial[0], bias_full)

    h1, a2 = _mix(hg, sg, p_gate, x2, mod3, g_post_mix[0], g_pre_ffn[0], wa, wb, wo, seq_len)
    hid, w2 = _proj(a2, w_ff1[0], 0, w_ff1.shape[2] // SEG, _act_relu2, "ff1", side=w_ff2[0])
    out = _resproj(hid, w2, h1, mod3, g_post_ffn[0], 3, seq_len, tm=1024, tk=2048, name="ff2")
    return out.reshape(batch, seq_len, d)
```

```python
import functools

import jax
import jax.numpy as jnp
from jax import lax
from jax.experimental import pallas as pl
from jax.experimental.pallas import tpu as pltpu

F32 = jnp.float32
BF16 = jnp.bfloat16
EPS = 1e-6
LOG2E = 1.4426950408889634

A_HEADS = 8
A_DIM = 128
A_WIDTH = A_HEADS * A_DIM
B_GROUPS = 8
B_CHUNK = 128
B_WIDTH = B_GROUPS * 128
N_MOD = 6
SEG = 1024
SEG_Q, SEG_F, SEG_V, SEG_OG, SEG_Z, SEG_GATE = 0, 1, 3, 4, 5, 7

HGRN_CHUNK = 64
PROJ_SUB = 128
V7X_VMEM_BYTES = 64 * 1024 * 1024


def _vmem_limit(estimate_bytes):
    return int(min(estimate_bytes, V7X_VMEM_BYTES - 4 * 1024 * 1024))


def _sigmoid(x):
    return 1.0 / (1.0 + jnp.exp(-x))


def _dot(a, b):
    return jnp.dot(a, b, preferred_element_type=F32)


def _dot_nt(a, b):
    return lax.dot_general(a, b, (((1,), (1,)), ((), ())), preferred_element_type=F32)


def _dot_tn(a, b):
    return lax.dot_general(a, b, (((0,), (0,)), ((), ())), preferred_element_type=F32)


def _rms(x):
    return x * lax.rsqrt(jnp.mean(x * x, axis=-1, keepdims=True) + EPS)


def _mod_block(c_ref, w_ref, b_ref):
    c = c_ref[...]
    s = c * _sigmoid(c)
    return _dot(s.astype(BF16), w_ref[...].astype(BF16)) + b_ref[...]


def _mod_kernel(c_ref, w_ref, b_ref, o_ref):
    o_ref[...] = _mod_block(c_ref, w_ref, b_ref)


def _modulation(c8, w_ada, b_ada, n):
    d = w_ada.shape[0]
    tn = 1024
    return pl.pallas_call(
        _mod_kernel,
        grid=(n // tn,),
        in_specs=[
            pl.BlockSpec((8, d), lambda j: (0, 0)),
            pl.BlockSpec((d, tn), lambda j: (0, j)),
            pl.BlockSpec((1, tn), lambda j: (0, j)),
        ],
        out_specs=pl.BlockSpec((8, tn), lambda j: (0, j)),
        out_shape=jax.ShapeDtypeStruct((8, n), F32),
        compiler_params=pltpu.CompilerParams(
            dimension_semantics=("arbitrary",),
            vmem_limit_bytes=_vmem_limit(2 * d * tn * 4 + d * tn * 2 + (8 << 20))),
        name="mod",
    )(c8, w_ada, b_ada)


def _normproj_kernel(x_ref, mod_ref, g_ref, w_ref, a_ref, o_ref, w_scr):
    @pl.when(pl.program_id(0) == 0)
    def _():
        w_scr[...] = w_ref[...].astype(BF16)

    sh = mod_ref[0, 0:1, :]
    scale = g_ref[...] * (1.0 + mod_ref[0, 1:2, :])
    for s in range(x_ref.shape[0] // PROJ_SUB):
        rs = slice(s * PROJ_SUB, (s + 1) * PROJ_SUB)
        a = (_rms(x_ref[rs, :]) * scale + sh).astype(BF16)
        a_ref[rs, :] = a
        o_ref[rs, :] = _act_silu(_dot(a, w_scr[...]), 0).astype(BF16)


def _normproj(x2, mod3, g, w, seg, rows_per_batch):
    m, d = x2.shape
    tm, tn = 1024, SEG
    bpb = rows_per_batch // tm
    est = 2 * tm * d * 6 + 2 * d * tn * 4 + d * tn * 2 + 2 * tm * tn * 2 + 8 * PROJ_SUB * d * 4 + (6 << 20)
    return pl.pallas_call(
        _normproj_kernel,
        grid=(m // tm,),
        in_specs=[
            pl.BlockSpec((tm, d), lambda i: (i, 0)),
            pl.BlockSpec((1, mod3.shape[1], d), lambda i: (i // bpb, 0, 0)),
            pl.BlockSpec((1, d), lambda i: (0, 0)),
            pl.BlockSpec((d, tn), lambda i: (0, seg)),
        ],
        out_specs=[pl.BlockSpec((tm, d), lambda i: (i, 0)), pl.BlockSpec((tm, tn), lambda i: (i, 0))],
        out_shape=[jax.ShapeDtypeStruct((m, d), BF16), jax.ShapeDtypeStruct((m, tn), BF16)],
        scratch_shapes=[pltpu.VMEM((d, tn), BF16)],
        compiler_params=pltpu.CompilerParams(
            dimension_semantics=("arbitrary",),
            vmem_limit_bytes=_vmem_limit(est)),
        name="norm_silu",
    )(x2, mod3, g.reshape(1, d), w)


def _act_linear(acc, j):
    return acc * jnp.where(j == 0, A_DIM ** -0.5, 1.0)


def _act_silu(acc, j):
    return acc * _sigmoid(acc)


def _act_gelu(acc, j):
    return 0.5 * acc * (1.0 + lax.erf(acc * (2.0 ** -0.5)))


def _act_sigmoid(acc, j):
    return _sigmoid(acc)


def _act_relu2(acc, j):
    r = jnp.maximum(acc, 0.0)
    return r * r


def _proj_kernel(a_ref, w_ref, *refs, act, has_side, has_mod):
    refs = list(refs)
    side_ref = refs.pop(0) if has_side else None
    mod_in = [refs.pop(0) for _ in range(3)] if has_mod else None
    o_ref = refs.pop(0)
    if has_side:
        refs.pop(0)[...] = side_ref[...].astype(BF16)
    if has_mod:
        refs.pop(0)[...] = _mod_block(*mod_in)
    (w_scr,) = refs
    j = pl.program_id(0)

    @pl.when(pl.program_id(1) == 0)
    def _():
        w_scr[...] = w_ref[...].astype(BF16)

    for s in range(a_ref.shape[0] // PROJ_SUB):
        rs = slice(s * PROJ_SUB, (s + 1) * PROJ_SUB)
        o_ref[rs, :] = act(_dot(a_ref[rs, :], w_scr[...]), j).astype(BF16)


def _proj(a, w, seg0, nseg, act, name, seg_stride=1, tm=2048, side=None, mod_tail=None):
    m, kdim = a.shape
    tn = SEG
    ni = m // tm
    in_specs = [
        pl.BlockSpec((tm, kdim), lambda j, i: (i, 0)),
        pl.BlockSpec((kdim, tn), lambda j, i: (0, seg0 + j * seg_stride)),
    ]
    out_specs = [pl.BlockSpec((tm, tn), lambda j, i: (i, j))]
    out_shape = [jax.ShapeDtypeStruct((m, nseg * tn), BF16)]
    args = [a, w]
    est = 2 * tm * kdim * 2 + 2 * kdim * tn * 4 + kdim * tn * 2 + 2 * tm * tn * 2 + 8 * PROJ_SUB * tn * 4 + (6 << 20)
    if side is not None:
        rows, cols = side.shape
        blk = rows // (nseg * ni)
        assert blk * nseg * ni == rows and blk % 16 == 0
        side_spec = pl.BlockSpec((blk, cols), lambda j, i: (j * ni + i, 0))
        in_specs.append(side_spec)
        out_specs.append(side_spec)
        out_shape.append(jax.ShapeDtypeStruct(side.shape, BF16))
        args.append(side)
        est += 2 * blk * cols * 6
    if mod_tail is not None:
        c8, w_ada, b_ada, col0 = mod_tail
        dm, n_all = w_ada.shape
        mblk = (n_all - col0) // (nseg * ni)
        assert mblk * nseg * ni == n_all - col0 and mblk % 128 == 0 and col0 % mblk == 0
        in_specs += [
            pl.BlockSpec((8, dm), lambda j, i: (0, 0)),
            pl.BlockSpec((dm, mblk), lambda j, i: (0, col0 // mblk + j * ni + i)),
            pl.BlockSpec((1, mblk), lambda j, i: (0, col0 // mblk + j * ni + i)),
        ]
        out_specs.append(pl.BlockSpec((8, mblk), lambda j, i: (0, j * ni + i)))
        out_shape.append(jax.ShapeDtypeStruct((8, n_all - col0), F32))
        args += [c8, w_ada, b_ada]
        est += 2 * dm * mblk * 4 + dm * mblk * 2
    single = side is None and mod_tail is None
    outs = pl.pallas_call(
        functools.partial(_proj_kernel, act=act, has_side=side is not None, has_mod=mod_tail is not None),
        grid=(nseg, ni),
        in_specs=in_specs,
        out_specs=out_specs,
        out_shape=out_shape,
        scratch_shapes=[pltpu.VMEM((kdim, tn), BF16)],
        compiler_params=pltpu.CompilerParams(
            dimension_semantics=("arbitrary", "arbitrary"),
            vmem_limit_bytes=_vmem_limit(est)),
        name=name,
    )(*args)
    return outs[0] if single else outs


def _hgrn_kernel(q_ref, ffw_ref, fbw_ref, v_ref, og_ref, lbl_ref, gn_ref,
                 o_ref, ofw_scr, obw_scr, st_scr, *, heads_per_step, seq_len):
    t = HGRN_CHUNK
    n_chunks = seq_len // t
    half = t // 2
    f_refs = (ffw_ref, fbw_ref)

    lbl = lbl_ref[...]
    c0s, c1s = [], []
    for d in range(2):
        l0 = lbl[2 * d:2 * d + 1, :]
        l1 = lbl[2 * d + 1:2 * d + 2, :]
        mx = jnp.maximum(l0, l1)
        e0 = jnp.exp(l0 - mx)
        lb = e0 / (e0 + jnp.exp(l1 - mx))
        c0s.append(0.5 * (1.0 + lb))
        c1s.append(0.5 * (1.0 - lb))

    st_scr[...] = jnp.zeros_like(st_scr)

    row = lax.broadcasted_iota(jnp.int32, (t, t), 0)
    col = lax.broadcasted_iota(jnp.int32, (t, t), 1)
    masks = (col <= row, col >= row)
    row2 = lax.broadcasted_iota(jnp.int32, (t, 2 * t), 0)
    col2 = lax.broadcasted_iota(jnp.int32, (t, 2 * t), 1) & (t - 1)
    tris = (jnp.where(col2 <= row2, 1.0, 0.0).astype(BF16), jnp.where(col2 >= row2, 1.0, 0.0).astype(BF16))

    def stage0(c, h, d):
        rows = pl.ds(pl.multiple_of(c * t, t), t)
        hs = slice(h * A_DIM, (h + 1) * A_DIM)
        ct = c1s[d][:, hs] * jnp.tanh(0.5 * f_refs[d][rows, hs].astype(F32))
        k = c1s[d][:, hs] - ct
        lf = jnp.log(c0s[d][:, hs] + ct)
        hi = lf.astype(BF16)
        lo = (lf - hi.astype(F32)).astype(BF16)
        return h, d, rows, hs, k, _dot(tris[d], jnp.concatenate([hi, lo], axis=0))

    def stage1(h, d, rows, hs, k, b):
        b = b * LOG2E
        q = q_ref[rows, hs].astype(F32)
        if d == 0:
            b_end = b[t - 1:t, :]
            b_mid = b[half - 1:half, :]
        else:
            b_end = b[0:1, :]
            b_mid = b[half:half + 1, :]
        qd = (q * jnp.exp2(b)).astype(BF16)
        ke = (k * jnp.exp2(b_end - b)).astype(BF16)
        qm = (q * jnp.exp2(b - b_mid)).astype(BF16)
        km = (k * jnp.exp2(b_mid - b)).astype(BF16)
        return h, d, rows, hs, _dot_nt(qm, km), qd, ke, jnp.exp2(b_end)

    def stage2(finalize, h, d, rows, hs, att, qd, ke, decay):
        v = v_ref[rows, hs]
        att = jnp.where(masks[d], att, 0.0).astype(BF16)
        idx = 2 * h + d
        s_t = st_scr[idx]
        o = _dot(att, v) + _dot_nt(qd, s_t.astype(BF16))
        st_scr[idx] = s_t * decay + _dot_tn(v, ke)
        if finalize:
            other = obw_scr if d == 0 else ofw_scr
            y = _rms(o + other[rows, hs]) * gn_ref[...]
            o_ref[rows, hs] = (y * og_ref[rows, hs].astype(F32)).astype(BF16)
        else:
            o_scr = ofw_scr if d == 0 else obw_scr
            o_scr[rows, hs] = o

    units = [(h, d) for h in range(heads_per_step) for d in range(2)]

    def half_sweep(c0, finalize):
        after0, after1 = {}, {}
        n_half = n_chunks // 2
        for step in range(n_half + 2):
            if step < n_half:
                c = c0 + step
                after0[step] = [stage0(c if d == 0 else n_chunks - 1 - c, h, d) for h, d in units]
            if 1 <= step <= n_half:
                after1[step - 1] = [stage1(*vals) for vals in after0.pop(step - 1)]
            if step >= 2:
                for vals in after1.pop(step - 2):
                    stage2(finalize, *vals)

    half_sweep(0, False)
    half_sweep(n_chunks // 2, True)


def _hgrn(p_lin, p_og, lb_logits4, g_norm, batch, seq_len):
    hps = 2
    wblk = hps * A_DIM
    nblk = A_WIDTH // wblk

    def seg_spec(seg):
        return pl.BlockSpec((seq_len, wblk), lambda b, p: (b, seg * nblk + p))

    est = 2 * 6 * seq_len * wblk * 2 + 2 * seq_len * wblk * 4 + (6 << 20)
    return pl.pallas_call(
        functools.partial(_hgrn_kernel, heads_per_step=hps, seq_len=seq_len),
        grid=(batch, nblk),
        in_specs=[
            seg_spec(0), seg_spec(1), seg_spec(2), seg_spec(3),
            seg_spec(0),
            pl.BlockSpec((4, wblk), lambda b, p: (0, p)),
            pl.BlockSpec((1, A_DIM), lambda b, p: (0, 0)),
        ],
        out_specs=pl.BlockSpec((seq_len, wblk), lambda b, p: (b, p)),
        out_shape=jax.ShapeDtypeStruct((batch * seq_len, A_WIDTH), BF16),
        scratch_shapes=[
            pltpu.VMEM((seq_len, wblk), F32),
            pltpu.VMEM((seq_len, wblk), F32),
            pltpu.VMEM((2 * hps, A_DIM, A_DIM), F32),
        ],
        compiler_params=pltpu.CompilerParams(
            dimension_semantics=("arbitrary", "arbitrary"),
            vmem_limit_bytes=_vmem_limit(est)),
        name="hgrn",
    )(p_lin, p_lin, p_lin, p_lin, p_og, lb_logits4, g_norm.reshape(1, A_DIM))


def _sgu_kernel(u_ref, v_ref, g_ref, ws_ref, bs_ref, o_ref, *, chunks_per_step):
    v = v_ref[...].astype(F32)
    mu = jnp.mean(v, axis=-1, keepdims=True)
    dlt = v - mu
    y = (dlt * lax.rsqrt(jnp.mean(dlt * dlt, axis=-1, keepdims=True) + EPS) * g_ref[...]).astype(BF16)
    for n in range(chunks_per_step):
        rs = slice(n * B_CHUNK, (n + 1) * B_CHUNK)
        for g in range(B_GROUPS):
            cs = slice(g * 128, (g + 1) * 128)
            vm = _dot(ws_ref[g].astype(BF16), y[rs, cs]) + bs_ref[:, cs]
            o_ref[rs, cs] = (u_ref[rs, cs].astype(F32) * vm).astype(BF16)


def _sgu(p_z, g_v, w_s, bias_full):
    m = p_z.shape[0]
    cps = 8
    tm = cps * B_CHUNK
    return pl.pallas_call(
        functools.partial(_sgu_kernel, chunks_per_step=cps),
        grid=(m // tm,),
        in_specs=[
            pl.BlockSpec((tm, B_WIDTH), lambda i: (i, 0)),
            pl.BlockSpec((tm, B_WIDTH), lambda i: (i, 1)),
            pl.BlockSpec((1, B_WIDTH), lambda i: (0, 0)),
            pl.BlockSpec((B_GROUPS, B_CHUNK, B_CHUNK), lambda i: (0, 0, 0)),
            pl.BlockSpec((B_CHUNK, B_WIDTH), lambda i: (0, 0)),
        ],
        out_specs=pl.BlockSpec((tm, B_WIDTH), lambda i: (i, 0)),
        out_shape=jax.ShapeDtypeStruct((m, B_WIDTH), BF16),
        compiler_params=pltpu.CompilerParams(
            dimension_semantics=("arbitrary",),
            vmem_limit_bytes=_vmem_limit(32 << 20)),
        name="sgu",
    )(p_z, p_z, g_v.reshape(1, B_WIDTH), w_s, bias_full)


def _mix_kernel(hg_ref, sg_ref, ga_ref, gb_ref, x_ref, mod_ref, gpost_ref, gnext_ref,
                wa_ref, wb_ref, wo_ref, h_ref, a_ref, *, sub):
    gt1 = mod_ref[0, 0:1, :]
    sh2 = mod_ref[0, 1:2, :]
    sc2 = mod_ref[0, 2:3, :]
    def branches(rs):
        return rs, _dot(hg_ref[rs, :], wa_ref[...]), _dot(sg_ref[rs, :], wb_ref[...])

    def project(rs, ya, yb):
        merged = ga_ref[rs, :].astype(F32) * ya + gb_ref[rs, :].astype(F32) * yb
        return rs, _dot(merged.astype(BF16), wo_ref[...])

    post_scale = gt1 * gpost_ref[...]
    next_scale = gnext_ref[...] * (1.0 + sc2)

    def finish(rs, mo):
        h = x_ref[rs, :] + _rms(mo) * post_scale
        h_ref[rs, :] = h
        a_ref[rs, :] = (_rms(h) * next_scale + sh2).astype(BF16)

    bounds = [0]
    for size in sub:
        bounds.append(bounds[-1] + size)
    assert bounds[-1] == hg_ref.shape[0]
    n_sub = len(sub)
    after0, after1 = {}, {}
    for step in range(n_sub + 2):
        if step < n_sub:
            after0[step] = branches(slice(bounds[step], bounds[step + 1]))
        if 1 <= step <= n_sub:
            after1[step - 1] = project(*after0.pop(step - 1))
        if step >= 2:
            finish(*after1.pop(step - 2))


def _mix(hg, sg, p_gate, x2, mod3, g_post, g_next, wa, wb, wo, rows_per_batch):
    m, d = x2.shape
    tm, sub = 512, (384, 128)
    bpb = rows_per_batch // tm
    const = lambda shape: pl.BlockSpec(shape, lambda i: (0, 0), pipeline_mode=pl.Buffered(1))
    row = lambda width, col: pl.BlockSpec((tm, width), lambda i: (i, col))
    vec_spec = pl.BlockSpec((1, d), lambda i: (0, 0))
    est = (2 * A_WIDTH * d + d * d) * 2 + 2 * tm * (2 * A_WIDTH * 2 + 2 * d * 2 + d * 4 + d * 4 + d * 2) \
        + 6 * tm * d * 4 + (4 << 20)
    return pl.pallas_call(
        functools.partial(_mix_kernel, sub=sub),
        grid=(m // tm,),
        in_specs=[
            row(A_WIDTH, 0), row(B_WIDTH, 0), row(d, 0), row(d, 1), row(d, 0),
            pl.BlockSpec((1, mod3.shape[1], d), lambda i: (i // bpb, 0, 0)),
            vec_spec, vec_spec,
            const((A_WIDTH, d)), const((B_WIDTH, d)), const((d, d)),
        ],
        out_specs=[row(d, 0), row(d, 0)],
        out_shape=[jax.ShapeDtypeStruct((m, d), F32), jax.ShapeDtypeStruct((m, d), BF16)],
        compiler_params=pltpu.CompilerParams(
            dimension_semantics=("arbitrary",),
            vmem_limit_bytes=_vmem_limit(est)),
        name="mix",
    )(hg, sg, p_gate, p_gate, x2, mod3, g_post.reshape(1, d), g_next.reshape(1, d), wa, wb, wo)


def _resproj_kernel(lhs_ref, w_ref, res_hbm, mod_ref, gpost_ref, h_ref, res_scr, res_sem, *, gate_row, sub):
    i = pl.program_id(0)
    k = pl.program_id(1)
    last = pl.num_programs(1) - 1
    tm = h_ref.shape[0]
    res_copy = pltpu.make_async_copy(res_hbm.at[pl.ds(pl.multiple_of(i * tm, tm), tm), :], res_scr, res_sem)

    @pl.when(k == last - 1)
    def _():
        res_copy.start()

    @pl.when(k == 0)
    def _():
        h_ref[...] = _dot(lhs_ref[...], w_ref[...])

    @pl.when((k > 0) & (k < last))
    def _():
        h_ref[...] += _dot(lhs_ref[...], w_ref[...])

    @pl.when(k == last)
    def _():
        w = w_ref[...]
        scale = mod_ref[0, gate_row:gate_row + 1, :] * gpost_ref[...]
        for s in range(tm // sub):
            rs = slice(s * sub, (s + 1) * sub)
            acc = h_ref[rs, :] + _dot(lhs_ref[rs, :], w)
            if s == 0:
                res_copy.wait()
            h_ref[rs, :] = res_scr[rs, :] + _rms(acc) * scale


def _resproj(lhs, w, res, mod3, g_post, gate_row, rows_per_batch, tm, tk, name):
    m, kdim = lhs.shape
    d = w.shape[1]
    assert w.dtype == BF16 and kdim // tk >= 2 and kdim % tk == 0 and m % tm == 0
    bpb = rows_per_batch // tm
    est = 2 * tm * tk * 2 + 2 * tk * d * 2 + 2 * tm * d * 4 + tm * d * 4 + (12 << 20)
    return pl.pallas_call(
        functools.partial(_resproj_kernel, gate_row=gate_row, sub=128),
        grid=(m // tm, kdim // tk),
        in_specs=[
            pl.BlockSpec((tm, tk), lambda i, k: (i, k)),
            pl.BlockSpec((tk, d), lambda i, k: (k, 0)),
            pl.BlockSpec(memory_space=pl.ANY),
            pl.BlockSpec((1, mod3.shape[1], d), lambda i, k: (i // bpb, 0, 0)),
            pl.BlockSpec((1, d), lambda i, k: (0, 0)),
        ],
        out_specs=pl.BlockSpec((tm, d), lambda i, k: (i, 0)),
        out_shape=jax.ShapeDtypeStruct((m, d), F32),
        scratch_shapes=[pltpu.VMEM((tm, d), F32), pltpu.SemaphoreType.DMA(())],
        compiler_params=pltpu.CompilerParams(
            dimension_semantics=("arbitrary", "arbitrary"),
            vmem_limit_bytes=_vmem_limit(est)),
        name=name,
    )(lhs, w, res, mod3, g_post.reshape(1, d))


def kernel(x, c, w_ada, b_ada, g_pre_mix, g_post_mix, g_pre_ffn, g_post_ffn, w_in, lb_logits,
           g_hgrn_norm, w_a_out, g_sgu_norm, w_spatial, b_spatial, w_b_out, w_o, w_ff1, w_ff2):
    batch, seq_len, d = x.shape
    depth = w_in.shape[0]
    assert depth == 1 and lb_logits.shape == (2, depth + 1, A_WIDTH)
    assert w_in.shape[2] == 11 * SEG and d == 2 * SEG
    assert g_hgrn_norm.shape == (depth, A_DIM) and w_spatial.shape == (depth, B_GROUPS, B_CHUNK, B_CHUNK)
    assert seq_len % 1024 == 0
    m = batch * seq_len

    c8 = jnp.zeros((8, d), F32).at[:batch].set(c)
    b_ada2 = b_ada[0].reshape(1, N_MOD * d)
    n_mix = 2 * d
    mod_mix = _modulation(c8, w_ada[0], b_ada2, n_mix)[:batch].reshape(batch, 2, d)

    x2 = x.reshape(m, d)
    w_in0 = w_in[0]
    a1, p_og = _normproj(x2, mod_mix, g_pre_mix[0], w_in0, SEG_OG, seq_len)
    p_lin, wa, mod_rest = _proj(a1, w_in0, SEG_Q, 4, _act_linear, "inproj_lin", side=w_a_out[0],
                                mod_tail=(c8, w_ada[0], b_ada2, n_mix))
    mod3 = mod_rest[:batch].reshape(batch, N_MOD - 2, d)
    p_z, wb = _proj(a1, w_in0, SEG_Z, 2, _act_gelu, "inproj_gelu", side=w_b_out[0])
    p_gate, wo = _proj(a1, w_in0, SEG_GATE, 4, _act_sigmoid, "inproj_sig", side=w_o[0])

    hg = _hgrn(p_lin, p_og, lb_logits.reshape(2 * (depth + 1), A_WIDTH), g_hgrn_norm[0], batch, seq_len)

    bias_full = jnp.repeat(b_spatial[0].T, 128, axis=1)
    sg = _sgu(p_z, g_sgu_norm[0], w_spatial[0], bias_full)

    h1, a2 = _mix(hg, sg, p_gate, x2, mod3, g_post_mix[0], g_pre_ffn[0], wa, wb, wo, seq_len)
    hid, w2 = _proj(a2, w_ff1[0], 0, w_ff1.shape[2] // SEG, _act_relu2, "ff1", side=w_ff2[0])
    out = _resproj(hid, w2, h1, mod3, g_post_ffn[0], 3, seq_len, tm=1024, tk=2048, name="ff2")
    return out.reshape(batch, seq_len, d)
```
